```python
import jax, jax.numpy as jnp
from jax import lax
import numpy as np

D_MODEL = 1024
BATCH = 4
SEQ = 8192
DEPTH = 2

GRID_W = 64
CTX_LEN = 256

D_MIX = D_MODEL
GROUP_W = D_MIX // 4

FNET_GROUPS = 4
FNET_CH = GROUP_W // FNET_GROUPS

WIN_HEADS = 4
WIN_KV_HEADS = 2
WIN_GROUP = WIN_HEADS // WIN_KV_HEADS
WIN_HEAD_DIM = GROUP_W // WIN_HEADS
WINDOW = 128
BLOCK = 128

MLA_HEADS = 4
MLA_NOPE = 64
MLA_ROPE = 32
MLA_QK = MLA_NOPE + MLA_ROPE
MLA_V = GROUP_W // MLA_HEADS
MLA_Q_RANK = 256
MLA_KV_RANK = 128

SGU_GROUPS = 4
SGU_CHUNK = 128
SGU_CH = GROUP_W // SGU_GROUPS

D_FF = 3584
N_EXPERTS = 8
TOP_K = 2

ROPE_THETA = 10000.0
EPS = 1e-6
NEG_INF = -1e30

PROJ_SIZES = (GROUP_W, WIN_HEADS * WIN_HEAD_DIM, WIN_KV_HEADS * WIN_HEAD_DIM, WIN_KV_HEADS * WIN_HEAD_DIM,
              MLA_Q_RANK, MLA_KV_RANK, MLA_ROPE, 2 * GROUP_W)
D_IN_PROJ = GROUP_W + WIN_HEADS * WIN_HEAD_DIM + 2 * WIN_KV_HEADS * WIN_HEAD_DIM + MLA_Q_RANK + MLA_KV_RANK + MLA_ROPE + 2 * GROUP_W

kernel_name = 'hybrid_diffusion_prefix_block'


def rms_norm(x, g):
    xf = x.astype(jnp.float32)
    y = xf * lax.rsqrt(jnp.mean(jnp.square(xf), axis=-1, keepdims=True) + EPS)
    return (y * g.astype(jnp.float32)).astype(x.dtype)


def layer_norm(x):
    xf = x.astype(jnp.float32)
    mu = jnp.mean(xf, axis=-1, keepdims=True)
    var = jnp.mean(jnp.square(xf - mu), axis=-1, keepdims=True)
    return ((xf - mu) * lax.rsqrt(var + EPS)).astype(x.dtype)


def adaln(cond, w_mod, b_mod):
    mod = jax.nn.silu(cond) @ w_mod + b_mod
    return [m[:, None, :] for m in jnp.split(mod, 6, axis=-1)]


def modulate(x, shift, scale):
    return x * (1.0 + scale) + shift


def axial_rope_tables(rows, rot_dim):
    n_freq = rot_dim // 4
    inv_freq = ROPE_THETA ** (-jnp.arange(n_freq, dtype=jnp.float32) / n_freq)
    row = jnp.repeat(jnp.arange(rows, dtype=jnp.float32), GRID_W)
    col = jnp.tile(jnp.arange(GRID_W, dtype=jnp.float32), rows)
    ang = jnp.concatenate([row[:, None] * inv_freq, col[:, None] * inv_freq], axis=-1)
    return jnp.cos(ang), jnp.sin(ang)


def apply_rope(x, cos, sin):
    xf = x.astype(jnp.float32).reshape(x.shape[:-1] + (x.shape[-1] // 2, 2))
    x1, x2 = xf[..., 0], xf[..., 1]
    cs, sn = cos[:, None, :], sin[:, None, :]
    out = jnp.stack([x1 * cs - x2 * sn, x1 * sn + x2 * cs], axis=-1)
    return out.reshape(x.shape).astype(x.dtype)


def split_proj(z):
    idx, acc = [], 0
    for s in PROJ_SIZES[:-1]:
        acc += s
        idx.append(acc)
    return jnp.split(z, idx, axis=-1)


def fourier_mix(a, w_f):
    b, n, _ = a.shape
    af = a.astype(jnp.float32).reshape(b, n, FNET_GROUPS, FNET_CH)
    y = jnp.fft.fft2(af, axes=(1, 3), norm='ortho').real
    return y.reshape(b, n, GROUP_W).astype(a.dtype) @ w_f


def sgu_mix(z, w_s, b_s):
    u, v = jnp.split(z, 2, axis=-1)
    v = layer_norm(v)
    b, n, _ = v.shape
    vc = v.reshape(b, n // SGU_CHUNK, SGU_CHUNK, SGU_GROUPS, SGU_CH)
    s = jnp.einsum('gpq,bcqgd->bcpgd', w_s, vc) + b_s.T[None, None, :, :, None]
    return u * s.reshape(b, n, GROUP_W)


def window_attn_latent(q, k, v, kc, vc, sink):
    b, n = q.shape[:2]
    lc = kc.shape[1]
    nb = n // BLOCK
    qb = q.reshape(b, nb, BLOCK, WIN_KV_HEADS, WIN_GROUP, WIN_HEAD_DIM)
    pad = ((0, 0), (BLOCK, BLOCK), (0, 0), (0, 0))
    kp = jnp.pad(k, pad).reshape(b, nb + 2, BLOCK, WIN_KV_HEADS, WIN_HEAD_DIM)
    vp = jnp.pad(v, pad).reshape(b, nb + 2, BLOCK, WIN_KV_HEADS, WIN_HEAD_DIM)
    kband = jnp.concatenate([kp[:, :-2], kp[:, 1:-1], kp[:, 2:]], axis=2)
    vband = jnp.concatenate([vp[:, :-2], vp[:, 1:-1], vp[:, 2:]], axis=2)
    s_win = jnp.einsum('bnqhgd,bnkhd->bnhgqk', qb, kband).astype(jnp.float32)
    qi = jnp.arange(BLOCK)[:, None]
    kj = jnp.arange(3 * BLOCK)[None, :]
    kpos = jnp.arange(nb)[:, None, None] * BLOCK - BLOCK + kj
    valid = (jnp.abs(kj - BLOCK - qi) <= WINDOW)[None] & (kpos >= 0) & (kpos < n)
    s_win = jnp.where(valid[None, :, None, None], s_win, NEG_INF)
    s_ctx = jnp.einsum('bnqhgd,bkhd->bnhgqk', qb, kc).astype(jnp.float32)
    s_sink = jnp.broadcast_to(sink.astype(jnp.float32).reshape(WIN_KV_HEADS, WIN_GROUP)[None, None, :, :, None, None],
                              s_win.shape[:-1] + (1,))
    p = jax.nn.softmax(jnp.concatenate([s_win, s_ctx, s_sink], axis=-1), axis=-1)
    p_win = p[..., :3 * BLOCK].astype(v.dtype)
    p_ctx = p[..., 3 * BLOCK:3 * BLOCK + lc].astype(v.dtype)
    o = jnp.einsum('bnhgqk,bnkhd->bnqhgd', p_win, vband) + jnp.einsum('bnhgqk,bkhd->bnqhgd', p_ctx, vc)
    return o.reshape(b, n, WIN_HEADS * WIN_HEAD_DIM)


def window_attn_context(q, kc, vc, sink):
    b, lc = q.shape[:2]
    s = jnp.einsum('bqhgd,bkhd->bhgqk', q, kc).astype(jnp.float32)
    s_sink = jnp.broadcast_to(sink.astype(jnp.float32).reshape(WIN_KV_HEADS, WIN_GROUP)[None, :, :, None, None],
                              s.shape[:-1] + (1,))
    p = jax.nn.softmax(jnp.concatenate([s, s_sink], axis=-1), axis=-1)[..., :lc].astype(vc.dtype)
    o = jnp.einsum('bhgqk,bkhd->bqhgd', p, vc)
    return o.reshape(b, lc, WIN_HEADS * WIN_HEAD_DIM)


def mla_q(cq, g_q, w_uq):
    b, l = cq.shape[:2]
    q = (rms_norm(cq, g_q) @ w_uq).reshape(b, l, MLA_HEADS, MLA_QK)
    return q[..., :MLA_NOPE], q[..., MLA_NOPE:]


def mla_kv(ckv, kr, g_kv, w_ukv):
    b, l = ckv.shape[:2]
    kv = (rms_norm(ckv, g_kv) @ w_ukv).reshape(b, l, MLA_HEADS, MLA_NOPE + MLA_V)
    k = jnp.concatenate([kv[..., :MLA_NOPE], jnp.broadcast_to(kr, (b, l, MLA_HEADS, MLA_ROPE))], axis=-1)
    return k, kv[..., MLA_NOPE:]


def mla_attn_latent(q, k, v, kc, vc):
    b, n = q.shape[:2]
    nb = n // BLOCK
    k_all = jnp.concatenate([k, kc], axis=1)
    v_all = jnp.concatenate([v, vc], axis=1)
    qb = jnp.moveaxis(q.reshape(b, nb, BLOCK, MLA_HEADS, MLA_QK), 1, 0)

    def one_block(q_blk):
        s = jnp.einsum('bqhd,bkhd->bhqk', q_blk, k_all).astype(jnp.float32)
        p = jax.nn.softmax(s, axis=-1).astype(v_all.dtype)
        return jnp.einsum('bhqk,bkhd->bqhd', p, v_all)

    o = lax.map(one_block, qb)
    return jnp.moveaxis(o, 0, 1).reshape(b, n, MLA_HEADS * MLA_V)


def mla_attn_context(q, kc, vc):
    b, lc = q.shape[:2]
    s = jnp.einsum('bqhd,bkhd->bhqk', q, kc).astype(jnp.float32)
    p = jax.nn.softmax(s, axis=-1).astype(vc.dtype)
    return jnp.einsum('bhqk,bkhd->bqhd', p, vc).reshape(b, lc, MLA_HEADS * MLA_V)


def token_mixer(ax, ac, cos_w, sin_w, cos_m, sin_m, w_in, w_fourier, win_sink, mla_g_q, mla_g_kv,
                mla_w_uq, mla_w_ukv, sgu_w, sgu_b, w_out, need_ctx_out):
    b, n, _ = ax.shape
    lc = ac.shape[1]
    mla_scale = MLA_QK ** -0.5
    a_x, wq_x, wk_x, wv_x, cq_x, ckv_x, kr_x, sg_x = split_proj(ax @ w_in)
    w_parts = split_proj(w_in)
    wk_c, wv_c, ckv_c, kr_c = [ac @ w_parts[i] for i in (2, 3, 5, 6)]
    k_win_c = wk_c.reshape(b, lc, WIN_KV_HEADS, WIN_HEAD_DIM)
    v_win_c = wv_c.reshape(b, lc, WIN_KV_HEADS, WIN_HEAD_DIM)
    k_mla_c, v_mla_c = mla_kv(ckv_c, kr_c[:, :, None, :], mla_g_kv, mla_w_ukv)

    four_x = fourier_mix(a_x, w_fourier)
    q = apply_rope(wq_x.reshape(b, n, WIN_HEADS, WIN_HEAD_DIM), cos_w, sin_w) * (WIN_HEAD_DIM ** -0.5)
    k = apply_rope(wk_x.reshape(b, n, WIN_KV_HEADS, WIN_HEAD_DIM), cos_w, sin_w)
    v = wv_x.reshape(b, n, WIN_KV_HEADS, WIN_HEAD_DIM)
    win_x = window_attn_latent(q.reshape(b, n, WIN_KV_HEADS, WIN_GROUP, WIN_HEAD_DIM), k, v, k_win_c, v_win_c, win_sink)
    qn, qr = mla_q(cq_x, mla_g_q, mla_w_uq)
    q_mla = jnp.concatenate([qn, apply_rope(qr, cos_m, sin_m)], axis=-1) * mla_scale
    k_mla, v_mla = mla_kv(ckv_x, apply_rope(kr_x[:, :, None, :], cos_m, sin_m), mla_g_kv, mla_w_ukv)
    mla_x = mla_attn_latent(q_mla, k_mla, v_mla, k_mla_c, v_mla_c)
    sgu_x = sgu_mix(jax.nn.gelu(sg_x, approximate=False), sgu_w, sgu_b)
    out_x = jnp.concatenate([four_x, win_x, mla_x, sgu_x], axis=-1) @ w_out
    if not need_ctx_out:
        return out_x, None

    a_c, wq_c, cq_c, sg_c = [ac @ w_parts[i] for i in (0, 1, 4, 7)]
    four_c = fourier_mix(a_c, w_fourier)
    q_c = wq_c.reshape(b, lc, WIN_KV_HEADS, WIN_GROUP, WIN_HEAD_DIM) * (WIN_HEAD_DIM ** -0.5)
    win_c = window_attn_context(q_c, k_win_c, v_win_c, win_sink)
    qn_c, qr_c = mla_q(cq_c, mla_g_q, mla_w_uq)
    mla_c = mla_attn_context(jnp.concatenate([qn_c, qr_c], axis=-1) * mla_scale, k_mla_c, v_mla_c)
    sgu_c = sgu_mix(jax.nn.gelu(sg_c, approximate=False), sgu_w, sgu_b)
    out_c = jnp.concatenate([four_c, win_c, mla_c, sgu_c], axis=-1) @ w_out
    return out_x, out_c


def swiglu(t, w_gate, w_up, w_down):
    return (jax.nn.silu(t @ w_gate) * (t @ w_up)) @ w_down


def moe_ffn(x, w_router, w_gate, w_up, w_down):
    b, n, d = x.shape
    t = x.reshape(-1, d)
    logits = (t @ w_router).astype(jnp.float32)
    top_val, top_idx = lax.top_k(logits, TOP_K)
    top_w = jax.nn.softmax(top_val, axis=-1)
    comb = jnp.sum(jax.nn.one_hot(top_idx, N_EXPERTS, dtype=jnp.float32) * top_w[..., None], axis=1)
    y = jnp.zeros_like(t)
    for e in range(N_EXPERTS):
        y = y + comb[:, e:e + 1].astype(t.dtype) * swiglu(t, w_gate[e], w_up[e], w_down[e])
    return y.reshape(b, n, d)


def channel_mixer(t, layer, ffn_w_gate, ffn_w_up, ffn_w_down, moe_w_router, moe_w_gate, moe_w_up, moe_w_down):
    i = layer // 2
    if layer % 2 == 0:
        return swiglu(t, ffn_w_gate[i], ffn_w_up[i], ffn_w_down[i])
    return moe_ffn(t, moe_w_router[i], moe_w_gate[i], moe_w_up[i], moe_w_down[i])


def setup_inputs(seed: int = 0) -> dict:
    key = jax.random.key(seed)
    ks = jax.random.split(key, 27)
    nl, nd, nm = DEPTH, (DEPTH + 1) // 2, DEPTH // 2

    def nrm(k, shape, scale):
        return scale * jax.random.normal(k, shape, jnp.float32)

    def gain(k, shape):
        return 1.0 + 0.05 * jax.random.normal(k, shape, jnp.float32)

    return {
        'x': nrm(ks[0], (BATCH, SEQ, D_MODEL), 1.0),
        'c': nrm(ks[1], (BATCH, D_MODEL), 1.0),
        'ctx': nrm(ks[2], (BATCH, CTX_LEN, D_MODEL), 1.0),
        'c_ctx': nrm(ks[3], (D_MODEL,), 1.0),
        'w_mod': nrm(ks[4], (nl, D_MODEL, 6 * D_MODEL), D_MODEL ** -0.5),
        'b_mod': nrm(ks[5], (nl, 6 * D_MODEL), 0.02),
        'g_pre_mix': gain(ks[6], (nl, D_MODEL)),
        'g_post_mix': gain(ks[7], (nl, D_MODEL)),
        'g_pre_ffn': gain(ks[8], (nl, D_MODEL)),
        'g_post_ffn': gain(ks[9], (nl, D_MODEL)),
        'w_in': nrm(ks[10], (nl, D_MODEL, D_IN_PROJ), D_MODEL ** -0.5),
        'w_out': nrm(ks[11], (nl, D_MIX, D_MODEL), D_MIX ** -0.5),
        'w_fourier': nrm(ks[12], (nl, GROUP_W, GROUP_W), GROUP_W ** -0.5),
        'win_sink': nrm(ks[13], (nl, WIN_HEADS), 1.0),
        'mla_g_q': gain(ks[14], (nl, MLA_Q_RANK)),
        'mla_g_kv': gain(ks[15], (nl, MLA_KV_RANK)),
        'mla_w_uq': nrm(ks[16], (nl, MLA_Q_RANK, MLA_HEADS * MLA_QK), MLA_Q_RANK ** -0.5),
        'mla_w_ukv': nrm(ks[17], (nl, MLA_KV_RANK, MLA_HEADS * (MLA_NOPE + MLA_V)), MLA_KV_RANK ** -0.5),
        'sgu_w': nrm(ks[18], (nl, SGU_GROUPS, SGU_CHUNK, SGU_CHUNK), SGU_CHUNK ** -0.5),
        'sgu_b': 1.0 + nrm(ks[19], (nl, SGU_GROUPS, SGU_CHUNK), 0.1),
        'ffn_w_gate': nrm(ks[20], (nd, D_MODEL, D_FF), D_MODEL ** -0.5),
        'ffn_w_up': nrm(ks[21], (nd, D_MODEL, D_FF), D_MODEL ** -0.5),
        'ffn_w_down': nrm(ks[22], (nd, D_FF, D_MODEL), D_FF ** -0.5),
        'moe_w_router': nrm(ks[23], (nm, D_MODEL, N_EXPERTS), D_MODEL ** -0.5),
        'moe_w_gate': nrm(ks[24], (nm, N_EXPERTS, D_MODEL, D_FF), D_MODEL ** -0.5),
        'moe_w_up': nrm(ks[25], (nm, N_EXPERTS, D_MODEL, D_FF), D_MODEL ** -0.5),
        'moe_w_down': nrm(ks[26], (nm, N_EXPERTS, D_FF, D_MODEL), D_FF ** -0.5),
    }


def reference(x, c, ctx, c_ctx, w_mod, b_mod, g_pre_mix, g_post_mix, g_pre_ffn, g_post_ffn,
              w_in, w_out, w_fourier, win_sink, mla_g_q, mla_g_kv, mla_w_uq, mla_w_ukv,
              sgu_w, sgu_b, ffn_w_gate, ffn_w_up, ffn_w_down,
              moe_w_router, moe_w_gate, moe_w_up, moe_w_down):
    rows = x.shape[1] // GRID_W
    cos_w, sin_w = axial_rope_tables(rows, WIN_HEAD_DIM)
    cos_m, sin_m = axial_rope_tables(rows, MLA_ROPE)
    hx, hc = x, ctx
    for layer in range(DEPTH):
        last = layer == DEPTH - 1
        sh_a, sc_a, gt_a, sh_f, sc_f, gt_f = adaln(c, w_mod[layer], b_mod[layer])
        csh_a, csc_a, cgt_a, csh_f, csc_f, cgt_f = adaln(c_ctx[None, :], w_mod[layer], b_mod[layer])
        ax = modulate(rms_norm(hx, g_pre_mix[layer]), sh_a, sc_a)
        ac = modulate(rms_norm(hc, g_pre_mix[layer]), csh_a, csc_a)
        mx, mc = token_mixer(ax, ac, cos_w, sin_w, cos_m, sin_m, w_in[layer], w_fourier[layer], win_sink[layer],
                             mla_g_q[layer], mla_g_kv[layer], mla_w_uq[layer], mla_w_ukv[layer],
                             sgu_w[layer], sgu_b[layer], w_out[layer], not last)
        hx = hx + gt_a * rms_norm(mx, g_post_mix[layer])
        fx = modulate(rms_norm(hx, g_pre_ffn[layer]), sh_f, sc_f)
        hx = hx + gt_f * rms_norm(channel_mixer(fx, layer, ffn_w_gate, ffn_w_up, ffn_w_down, moe_w_router,
                                                moe_w_gate, moe_w_up, moe_w_down), g_post_ffn[layer])
        if not last:
            hc = hc + cgt_a * rms_norm(mc, g_post_mix[layer])
            fc = modulate(rms_norm(hc, g_pre_ffn[layer]), csh_f, csc_f)
            hc = hc + cgt_f * rms_norm(channel_mixer(fc, layer, ffn_w_gate, ffn_w_up, ffn_w_down, moe_w_router,
                                                    moe_w_gate, moe_w_up, moe_w_down), g_post_ffn[layer])
    return hx
```

```python
import functools

import numpy as np
import jax
import jax.numpy as jnp
from jax import lax
from jax.experimental import pallas as pl
from jax.experimental.pallas import tpu as pltpu

F32 = jnp.float32
BF16 = jnp.bfloat16
HIGHEST = lax.Precision.HIGHEST

GRID_W = 64
GROUP_W = 256
FNET_GROUPS = 4
FNET_CH = 64
WIN_HEADS = 4
WIN_KV_HEADS = 2
WIN_HEAD_DIM = 64
WINDOW = 128
BLOCK = 128
MLA_HEADS = 4
MLA_NOPE = 64
MLA_ROPE = 32
MLA_QK = MLA_NOPE + MLA_ROPE
MLA_V = 64
MLA_Q_RANK = 256
MLA_KV_RANK = 128
SGU_GROUPS = 4
SGU_CHUNK = 128
SGU_CH = 64
N_EXPERTS = 8
ROPE_THETA = 10000.0
EPS = 1e-6
NEG_INF = -1e30

LANES = 128
MLA_HEAD_PAD = 128
FFT_N1 = 128
VMEM_LIMIT = 56 * 1024 * 1024

_O_A, _O_Q, _O_QR, _O_K, _O_KR, _O_V, _O_CQ, _O_CKV, _O_KRP, _O_KRPR, _O_SG, _O_END = (
    0, 256, 512, 768, 896, 1024, 1152, 1408, 1536, 1664, 1792, 2304)


def _cparams(*sem):
    return pltpu.CompilerParams(dimension_semantics=sem, vmem_limit_bytes=VMEM_LIMIT)


def _full(shape):
    nd = len(shape)
    return pl.BlockSpec(shape, lambda *_: (0,) * nd)


def _dot(a, b):
    return jnp.dot(a, b, preferred_element_type=F32)


def _dot_nt(a, b):
    return lax.dot_general(a, b, (((1,), (1,)), ((), ())), preferred_element_type=F32)


def _dot2(hi, lo, x):
    return _dot(hi, x) + _dot(lo, x)


def _split_hi_lo(m):
    m = np.asarray(m, np.float32)
    hi = jnp.asarray(m, F32).astype(BF16)
    lo = (jnp.asarray(m, F32) - hi.astype(F32)).astype(BF16)
    return hi, lo


def _rms(x, g):
    return x * lax.rsqrt(jnp.mean(x * x, axis=-1, keepdims=True) + EPS) * g


def _mod_kernel(c_ref, w_ref, b_ref, o_ref):
    c = c_ref[...]
    s = c / (1.0 + jnp.exp(-c))
    o_ref[...] = jnp.dot(s, w_ref[...], preferred_element_type=F32, precision=HIGHEST) + b_ref[...]


def _modulation(cond, w_mod, b_mod):
    r, d = cond.shape
    n6 = w_mod.shape[1]
    tn = 512
    return pl.pallas_call(
        _mod_kernel,
        grid=(n6 // tn,),
        in_specs=[_full((r, d)), pl.BlockSpec((d, tn), lambda j: (0, j)), pl.BlockSpec((1, tn), lambda j: (0, j))],
        out_specs=pl.BlockSpec((r, tn), lambda j: (0, j)),
        out_shape=jax.ShapeDtypeStruct((r, n6), F32),
        compiler_params=_cparams("arbitrary"),
        name="adaln_mod",
    )(cond, w_mod, b_mod.reshape(1, n6))


def _inproj_kernel(h_ref, g_ref, sh_ref, sc_ref, w_ref, cw_ref, sw_ref, cm_ref, sm_ref,
                   gq_ref, gkv_ref, wq1_ref, wq2_ref, wk_ref, wv_ref,
                   a_o, qw_o, kw_o, vw_o, qm_o, km_o, vm_o, u_o, vn_o):
    x = h_ref[...]
    ax = (_rms(x, g_ref[...]) * (1.0 + sc_ref[0]) + sh_ref[0]).astype(BF16)

    def proj(lo, hi):
        return _dot(ax, w_ref[:, lo:hi])

    a_o[...] = proj(_O_A, _O_Q).astype(BF16)
    cw = cw_ref[...]
    sw = sw_ref[...]
    qw = (proj(_O_Q, _O_QR) * cw + proj(_O_QR, _O_K) * sw) * (WIN_HEAD_DIM ** -0.5)
    qw_o[...] = qw.astype(BF16)
    nk = WIN_KV_HEADS * WIN_HEAD_DIM
    kw_o[...] = (proj(_O_K, _O_KR) * cw[:, :nk] + proj(_O_KR, _O_V) * sw[:, :nk]).astype(BF16)
    vw_o[...] = proj(_O_V, _O_CQ).astype(BF16)

    cm = cm_ref[...]
    sm = sm_ref[...]
    cqn = _rms(proj(_O_CQ, _O_CKV), gq_ref[...]).astype(BF16)
    y1 = _dot(cqn, wq1_ref[...])
    y2 = _dot(cqn, wq2_ref[...])
    scale = MLA_QK ** -0.5
    for h in range(MLA_HEADS):
        sl = slice(h * MLA_HEAD_PAD, (h + 1) * MLA_HEAD_PAD)
        qm_o[:, sl] = ((y1[:, sl] * cm + y2[:, sl] * sm) * scale).astype(BF16)

    ckvn = _rms(proj(_O_CKV, _O_KRP), gkv_ref[...]).astype(BF16)
    krp = proj(_O_KRP, _O_KRPR) * cm + proj(_O_KRPR, _O_SG) * sm
    kk = _dot(ckvn, wk_ref[...])
    for h in range(MLA_HEADS):
        sl = slice(h * MLA_HEAD_PAD, (h + 1) * MLA_HEAD_PAD)
        km_o[:, sl] = (kk[:, sl] + krp).astype(BF16)
    vm_o[...] = _dot(ckvn, wv_ref[...]).astype(BF16)

    sg = proj(_O_SG, _O_END)
    gl = 0.5 * sg * (1.0 + lax.erf(sg * (2.0 ** -0.5)))
    u_o[...] = gl[:, :GROUP_W].astype(BF16)
    v = gl[:, GROUP_W:]
    mu = jnp.mean(v, axis=-1, keepdims=True)
    vc = v - mu
    var = jnp.mean(vc * vc, axis=-1, keepdims=True)
    vn_o[...] = (vc * lax.rsqrt(var + EPS)).astype(BF16)


def _inproj(h, g, shift, scale, w_aug, tables, gq, gkv, wq1, wq2, wk, wv, per_batch, tm):
    t, d = h.shape
    nmod = shift.shape[0]
    tpb = per_batch // tm
    cw, sw, cm, sm = tables

    def mod_idx(i):
        return ((i // tpb) if nmod > 1 else 0, 0, 0)

    tok = lambda w: pl.BlockSpec((tm, w), lambda i: (i, 0))
    tab = lambda w: pl.BlockSpec((tm, w), lambda i: (i % tpb, 0))
    in_specs = [tok(d), _full((1, d)), pl.BlockSpec((1, 1, d), mod_idx), pl.BlockSpec((1, 1, d), mod_idx),
                _full(w_aug.shape), tab(256), tab(256), tab(LANES), tab(LANES),
                _full((1, MLA_Q_RANK)), _full((1, MLA_KV_RANK)),
                _full(wq1.shape), _full(wq2.shape), _full(wk.shape), _full(wv.shape)]
    widths = (256, 256, 128, 128, 512, 512, 256, 256, 256)
    return pl.pallas_call(
        _inproj_kernel,
        grid=(t // tm,),
        in_specs=in_specs,
        out_specs=[tok(w) for w in widths],
        out_shape=[jax.ShapeDtypeStruct((t, w), BF16) for w in widths],
        compiler_params=_cparams("parallel"),
        name="inproj",
    )(h, g.reshape(1, d), shift, scale, w_aug, cw, sw, cm, sm, gq.reshape(1, -1), gkv.reshape(1, -1), wq1, wq2, wk, wv)


def _fft1_kernel(x_ref, mh_ref, ml_ref, tr_ref, ti_ref, or_ref, oi_ref):
    x = x_ref[0]
    z = _dot2(mh_ref[...], ml_ref[...], x)
    ar, ai = z[:FFT_N1], z[FFT_N1:]
    tr, ti = tr_ref[...], ti_ref[...]
    or_ref[0] = (ar * tr - ai * ti).astype(BF16)
    oi_ref[0] = (ar * ti + ai * tr).astype(BF16)


def _fft2_kernel(r_ref, i_ref, ch_ref, cl_ref, sh_ref, sl_ref, bch_ref, bcl_ref, bsh_ref, bsl_ref, wf_ref, o_ref, *, nj):
    ch, cl, sh, sl = ch_ref[...], cl_ref[...], sh_ref[...], sl_ref[...]
    for j in range(nj):
        r = r_ref[0, j]
        i = i_ref[0, j]
        gr = (_dot2(ch, cl, r) + _dot2(sh, sl, i)).astype(BF16)
        gi = (_dot2(ch, cl, i) - _dot2(sh, sl, r)).astype(BF16)
        y = _dot(gr, bch_ref[...]) + _dot(gr, bcl_ref[...]) + _dot(gi, bsh_ref[...]) + _dot(gi, bsl_ref[...])
        o_ref[0, j] = _dot(y.astype(BF16), wf_ref[...]).astype(BF16)


def _dft_small_kernel(a_ref, ch_ref, cl_ref, sh_ref, sl_ref, bch_ref, bcl_ref, bsh_ref, bsl_ref, wf_ref, o_ref):
    a = a_ref[0]
    gr = _dot2(ch_ref[...], cl_ref[...], a).astype(BF16)
    gi = (-_dot2(sh_ref[...], sl_ref[...], a)).astype(BF16)
    y = _dot(gr, bch_ref[...]) + _dot(gr, bcl_ref[...]) + _dot(gi, bsh_ref[...]) + _dot(gi, bsl_ref[...])
    o_ref[0] = _dot(y.astype(BF16), wf_ref[...]).astype(BF16)


def _cos_sin(n):
    k = np.arange(n)
    ang = 2.0 * np.pi * ((k[:, None] * k[None, :]) % n) / n
    return np.cos(ang), np.sin(ang)


def _channel_dft(norm):
    c, s = _cos_sin(FNET_CH)
    bc = np.kron(np.eye(FNET_GROUPS), c) * norm
    bs = np.kron(np.eye(FNET_GROUPS), s) * norm
    return _split_hi_lo(bc) + _split_hi_lo(bs)


def _fourier_latent(a, w_f):
    b, n, gw = a.shape
    n2 = n // FFT_N1
    c1, s1 = _cos_sin(FFT_N1)
    mh, ml = _split_hi_lo(np.concatenate([c1, -s1], axis=0))
    k1 = np.arange(FFT_N1)[:, None]
    t2 = np.arange(n2)[None, :]
    ang = 2.0 * np.pi * ((k1 * t2) % n) / n
    tr = jnp.repeat(jnp.asarray(np.cos(ang), F32), gw, axis=1)
    ti = jnp.repeat(jnp.asarray(-np.sin(ang), F32), gw, axis=1)
    cols = n2 * gw
    tc = min(cols, 4096)
    x2 = a.reshape(b, FFT_N1, cols)
    blk = pl.BlockSpec((1, FFT_N1, tc), lambda bi, ci: (bi, 0, ci))
    tblk = pl.BlockSpec((FFT_N1, tc), lambda bi, ci: (0, ci))
    o_r, o_i = pl.pallas_call(
        _fft1_kernel,
        grid=(b, cols // tc),
        in_specs=[blk, _full(mh.shape), _full(ml.shape), tblk, tblk],
        out_specs=[blk, blk],
        out_shape=[jax.ShapeDtypeStruct((b, FFT_N1, cols), BF16)] * 2,
        compiler_params=_cparams("parallel", "parallel"),
        name="fft_stage1",
    )(x2, mh, ml, tr, ti)

    c2, s2 = _cos_sin(n2)
    consts = _split_hi_lo(c2) + _split_hi_lo(s2) + _channel_dft((n * FNET_CH) ** -0.5)
    nj = 8
    r4 = o_r.reshape(b, FFT_N1, n2, gw)
    i4 = o_i.reshape(b, FFT_N1, n2, gw)
    blk4 = pl.BlockSpec((1, nj, n2, gw), lambda bi, ki: (bi, ki, 0, 0))
    out = pl.pallas_call(
        functools.partial(_fft2_kernel, nj=nj),
        grid=(b, FFT_N1 // nj),
        in_specs=[blk4, blk4] + [_full(m.shape) for m in consts] + [_full(w_f.shape)],
        out_specs=blk4,
        out_shape=jax.ShapeDtypeStruct((b, FFT_N1, n2, gw), BF16),
        compiler_params=_cparams("parallel", "parallel"),
        name="fft_stage2",
    )(r4, i4, *consts, w_f)
    return jnp.swapaxes(out, 1, 2).reshape(b, n, gw)


def _fourier_context(a, w_f):
    b, n, gw = a.shape
    c, s = _cos_sin(n)
    consts = _split_hi_lo(c) + _split_hi_lo(s) + _channel_dft((n * FNET_CH) ** -0.5)
    blk = pl.BlockSpec((1, n, gw), lambda bi: (bi, 0, 0))
    return pl.pallas_call(
        _dft_small_kernel,
        grid=(b,),
        in_specs=[blk] + [_full(m.shape) for m in consts] + [_full(w_f.shape)],
        out_specs=blk,
        out_shape=jax.ShapeDtypeStruct((b, n, gw), BF16),
        compiler_params=_cparams("parallel"),
        name="dft_context",
    )(a, *consts, w_f)


def _win_heads(q_blk, k_win, v_win, valid, kc, vc, sink_ref):
    rows = lax.broadcasted_iota(jnp.int32, (WIN_KV_HEADS * BLOCK, 1), 0)
    outs = [None] * WIN_HEADS
    d = WIN_HEAD_DIM
    for hk in range(WIN_KV_HEADS):
        h0 = 2 * hk
        q2 = jnp.concatenate([q_blk[:, h0 * d:(h0 + 1) * d], q_blk[:, (h0 + 1) * d:(h0 + 2) * d]], axis=0)
        ksl = slice(hk * d, (hk + 1) * d)
        sink = jnp.where(rows < BLOCK, sink_ref[h0], sink_ref[h0 + 1])
        s_c = _dot_nt(q2, kc[:, ksl])
        m = jnp.maximum(jnp.max(s_c, axis=-1, keepdims=True), sink)
        if k_win is not None:
            s_w = jnp.where(valid, _dot_nt(q2, k_win[:, ksl]), NEG_INF)
            m = jnp.maximum(m, jnp.max(s_w, axis=-1, keepdims=True))
        p_c = jnp.exp(s_c - m)
        den = jnp.sum(p_c, axis=-1, keepdims=True) + jnp.exp(sink - m)
        o = _dot(p_c.astype(BF16), vc[:, ksl])
        if k_win is not None:
            p_w = jnp.exp(s_w - m)
            den = den + jnp.sum(p_w, axis=-1, keepdims=True)
            o = o + _dot(p_w.astype(BF16), v_win[:, ksl])
        o = o / den
        outs[h0] = o[:BLOCK]
        outs[h0 + 1] = o[BLOCK:]
    return jnp.concatenate(outs, axis=1)


def _win_kernel(sink_ref, q_ref, kp_ref, k_ref, kn_ref, vp_ref, v_ref, vn_ref, kc_ref, vc_ref, o_ref, *, nblk, n):
    i = pl.program_id(1)
    kfull = jnp.concatenate([kp_ref[0], k_ref[0], kn_ref[0]], axis=0)
    vfull = jnp.concatenate([vp_ref[0], v_ref[0], vn_ref[0]], axis=0)
    kc = kc_ref[0]
    vc = vc_ref[0]
    shape = (WIN_KV_HEADS * BLOCK, 3 * BLOCK)
    qi = lax.broadcasted_iota(jnp.int32, shape, 0) % BLOCK
    kj = lax.broadcasted_iota(jnp.int32, shape, 1)
    band = jnp.abs(kj - BLOCK - qi) <= WINDOW
    for j in range(nblk):
        kpos = (i * nblk + j - 1) * BLOCK + kj
        valid = band & (kpos >= 0) & (kpos < n)
        o = _win_heads(q_ref[0, j * BLOCK:(j + 1) * BLOCK, :], kfull[j * BLOCK:(j + 3) * BLOCK],
                       vfull[j * BLOCK:(j + 3) * BLOCK], valid, kc, vc, sink_ref)
        o_ref[0, j * BLOCK:(j + 1) * BLOCK, :] = o.astype(BF16)


def _winctx_kernel(sink_ref, q_ref, kc_ref, vc_ref, o_ref, *, nblk):
    kc = kc_ref[0]
    vc = vc_ref[0]
    for j in range(nblk):
        o = _win_heads(q_ref[0, j * BLOCK:(j + 1) * BLOCK, :], None, None, None, kc, vc, sink_ref)
        o_ref[0, j * BLOCK:(j + 1) * BLOCK, :] = o.astype(BF16)


def _window_latent(q, k, v, kc, vc, sink):
    b, n, _ = q.shape
    lc = kc.shape[1]
    nblk = min(8, n // BLOCK)
    tq = nblk * BLOCK
    nb = n // BLOCK
    kvw = WIN_KV_HEADS * WIN_HEAD_DIM
    cur = lambda w: pl.BlockSpec((1, tq, w), lambda bi, i: (bi, i, 0))
    prev = pl.BlockSpec((1, BLOCK, kvw), lambda bi, i: (bi, jnp.maximum(i * nblk - 1, 0), 0))
    nxt = pl.BlockSpec((1, BLOCK, kvw), lambda bi, i: (bi, jnp.minimum((i + 1) * nblk, nb - 1), 0))
    ctx = pl.BlockSpec((1, lc, kvw), lambda bi, i: (bi, 0, 0))
    smem = pl.BlockSpec(memory_space=pltpu.SMEM)
    return pl.pallas_call(
        functools.partial(_win_kernel, nblk=nblk, n=n),
        grid=(b, n // tq),
        in_specs=[smem, cur(256), prev, cur(kvw), nxt, prev, cur(kvw), nxt, ctx, ctx],
        out_specs=cur(256),
        out_shape=jax.ShapeDtypeStruct((b, n, 256), BF16),
        compiler_params=_cparams("parallel", "parallel"),
        name="window_attn",
    )(sink, q, k, k, k, v, v, v, kc, vc)


def _window_context(q, kc, vc, sink):
    b, lc, _ = q.shape
    kvw = WIN_KV_HEADS * WIN_HEAD_DIM
    blk = lambda w: pl.BlockSpec((1, lc, w), lambda bi: (bi, 0, 0))
    smem = pl.BlockSpec(memory_space=pltpu.SMEM)
    return pl.pallas_call(
        functools.partial(_winctx_kernel, nblk=lc // BLOCK),
        grid=(b,),
        in_specs=[smem, blk(256), blk(kvw), blk(kvw)],
        out_specs=blk(256),
        out_shape=jax.ShapeDtypeStruct((b, lc, 256), BF16),
        compiler_params=_cparams("parallel"),
        name="window_attn_ctx",
    )(sink, q, kc, vc)


def _mla_kernel(*refs, tk, nk, has_ctx):
    if has_ctx:
        q_ref, k_ref, v_ref, kc_ref, vc_ref, o_ref = refs
    else:
        q_ref, k_ref, v_ref, o_ref = refs
    tq = q_ref.shape[1]
    outs = []
    for hh in range(2):
        qsl = slice(hh * MLA_HEAD_PAD, (hh + 1) * MLA_HEAD_PAD)
        vsl = slice(hh * MLA_V, (hh + 1) * MLA_V)
        q = q_ref[0, :, qsl]

        def update(carry, k, v):
            m, l, acc = carry
            s = _dot_nt(q, k)
            m_new = jnp.maximum(m, jnp.max(s, axis=-1, keepdims=True))
            alpha = jnp.exp(m - m_new)
            p = jnp.exp(s - m_new)
            l = alpha * l + jnp.sum(p, axis=-1, keepdims=True)
            acc = alpha * acc + _dot(p.astype(BF16), v)
            return m_new, l, acc

        def step(c, carry):
            off = pl.multiple_of(c * tk, tk)
            return update(carry, k_ref[0, pl.ds(off, tk), qsl], v_ref[0, pl.ds(off, tk), vsl])

        carry = (jnp.full((tq, 1), NEG_INF, F32), jnp.zeros((tq, 1), F32), jnp.zeros((tq, MLA_V), F32))
        carry = lax.fori_loop(0, nk, step, carry)
        if has_ctx:
            carry = update(carry, kc_ref[0, :, qsl], vc_ref[0, :, vsl])
        _, l, acc = carry
        outs.append(acc / l)
    o_ref[0] = jnp.concatenate(outs, axis=1).astype(BF16)


def _mla_attention(q, k, v, kc=None, vc=None):
    b, nq, _ = q.shape
    nkeys = k.shape[1]
    tq = min(256, nq)
    tk = min(512, nkeys)
    has_ctx = kc is not None
    hp = MLA_HEADS // 2
    in_specs = [pl.BlockSpec((1, tq, 2 * MLA_HEAD_PAD), lambda bi, h, i: (bi, i, h)),
                pl.BlockSpec((1, nkeys, 2 * MLA_HEAD_PAD), lambda bi, h, i: (bi, 0, h)),
                pl.BlockSpec((1, nkeys, 2 * MLA_V), lambda bi, h, i: (bi, 0, h))]
    args = [q, k, v]
    if has_ctx:
        lc = kc.shape[1]
        in_specs += [pl.BlockSpec((1, lc, 2 * MLA_HEAD_PAD), lambda bi, h, i: (bi, 0, h)),
                     pl.BlockSpec((1, lc, 2 * MLA_V), lambda bi, h, i: (bi, 0, h))]
        args += [kc, vc]
    return pl.pallas_call(
        functools.partial(_mla_kernel, tk=tk, nk=nkeys // tk, has_ctx=has_ctx),
        grid=(b, hp, nq // tq),
        in_specs=in_specs,
        out_specs=pl.BlockSpec((1, tq, 2 * MLA_V), lambda bi, h, i: (bi, i, h)),
        out_shape=jax.ShapeDtypeStruct((b, nq, MLA_HEADS * MLA_V), BF16),
        compiler_params=_cparams("parallel", "parallel", "arbitrary"),
        name="mla_attn",
    )(*args)


def _sgu_kernel(u_ref, v_ref, w_ref, b_ref, o_ref, *, nchunk):
    lane = lax.broadcasted_iota(jnp.int32, (SGU_CHUNK, GROUP_W), 1) // SGU_CH
    bias = b_ref[...]
    for c in range(nchunk):
        sl = slice(c * SGU_CHUNK, (c + 1) * SGU_CHUNK)
        v = v_ref[0, sl, :]
        s = bias
        for g in range(SGU_GROUPS):
            s = s + _dot(w_ref[g], jnp.where(lane == g, v, jnp.zeros_like(v)))
        o_ref[0, sl, :] = (u_ref[0, sl, :].astype(F32) * s).astype(BF16)


def _sgu(u, vn, w_s, bias_full):
    b, n, gw = u.shape
    nchunk = min(8, n // SGU_CHUNK)
    tn = nchunk * SGU_CHUNK
    blk = pl.BlockSpec((1, tn, gw), lambda bi, i: (bi, i, 0))
    return pl.pallas_call(
        functools.partial(_sgu_kernel, nchunk=nchunk),
        grid=(b, n // tn),
        in_specs=[blk, blk, _full(w_s.shape), _full(bias_full.shape)],
        out_specs=blk,
        out_shape=jax.ShapeDtypeStruct((b, n, gw), BF16),
        compiler_params=_cparams("parallel", "parallel"),
        name="sgu",
    )(u, vn, w_s, bias_full)


def _outproj_kernel(*refs, with_router):
    if with_router:
        (p0, p1, p2, p3, w_ref, h_ref, gate_ref, gpost_ref, gpre_ref, sh_ref, sc_ref, wr_ref,
         hn_o, fx_o, comb_o) = refs
    else:
        p0, p1, p2, p3, w_ref, h_ref, gate_ref, gpost_ref, gpre_ref, sh_ref, sc_ref, hn_o, fx_o = refs
    mx = None
    for idx, p in enumerate((p0, p1, p2, p3)):
        t = _dot(p[...], w_ref[idx * GROUP_W:(idx + 1) * GROUP_W, :])
        mx = t if mx is None else mx + t
    hn = h_ref[...] + gate_ref[0] * _rms(mx, gpost_ref[...])
    hn_o[...] = hn
    fx = _rms(hn, gpre_ref[...]) * (1.0 + sc_ref[0]) + sh_ref[0]
    fx_o[...] = fx.astype(BF16)
    if with_router:
        logits = jnp.dot(fx, wr_ref[...], preferred_element_type=F32, precision=HIGHEST)
        lane = lax.broadcasted_iota(jnp.int32, logits.shape, 1).astype(F32)
        lg = jnp.where(lane < N_EXPERTS, logits, -jnp.inf)
        m1 = jnp.max(lg, axis=-1, keepdims=True)
        i1 = jnp.min(jnp.where(lg == m1, lane, float(LANES)), axis=-1, keepdims=True)
        lg2 = jnp.where(lane == i1, -jnp.inf, lg)
        m2 = jnp.max(lg2, axis=-1, keepdims=True)
        i2 = jnp.min(jnp.where(lg2 == m2, lane, float(LANES)), axis=-1, keepdims=True)
        e2 = jnp.exp(m2 - m1)
        den = 1.0 + e2
        comb_o[...] = jnp.where(lane == i1, 1.0 / den, 0.0) + jnp.where(lane == i2, e2 / den, 0.0)


def _outproj(parts, w_out, h, gate, g_post, g_pre, shift, scale, per_batch, tm, w_router=None):
    t, d = h.shape
    nmod = gate.shape[0]
    tpb = per_batch // tm
    with_router = w_router is not None

    def mod_idx(i):
        return ((i // tpb) if nmod > 1 else 0, 0, 0)

    tok = lambda w: pl.BlockSpec((tm, w), lambda i: (i, 0))
    mod = pl.BlockSpec((1, 1, d), mod_idx)
    in_specs = [tok(GROUP_W)] * 4 + [_full(w_out.shape), tok(d), mod, _full((1, d)), _full((1, d)), mod, mod]
    args = list(parts) + [w_out, h, gate, g_post.reshape(1, d), g_pre.reshape(1, d), shift, scale]
    out_specs = [tok(d), tok(d)]
    out_shape = [jax.ShapeDtypeStruct((t, d), F32), jax.ShapeDtypeStruct((t, d), BF16)]
    if with_router:
        in_specs.append(_full(w_router.shape))
        args.append(w_router)
        out_specs.append(tok(LANES))
        out_shape.append(jax.ShapeDtypeStruct((t, LANES), F32))
    return pl.pallas_call(
        functools.partial(_outproj_kernel, with_router=with_router),
        grid=(t // tm,),
        in_specs=in_specs,
        out_specs=out_specs,
        out_shape=out_shape,
        compiler_params=_cparams("parallel"),
        name="outproj",
    )(*args)


def _ffn_kernel(x_ref, wg_ref, wu_ref, wd_ref, h_ref, gate_ref, gpost_ref, o_ref, acc_ref):
    f = pl.program_id(1)

    @pl.when(f == 0)
    def _():
        acc_ref[...] = jnp.zeros_like(acc_ref)

    x = x_ref[...]
    g = _dot(x, wg_ref[...])
    u = _dot(x, wu_ref[...])
    hid = (g / (1.0 + jnp.exp(-g)) * u).astype(BF16)
    acc_ref[...] += _dot(hid, wd_ref[...])

    @pl.when(f == pl.num_programs(1) - 1)
    def _():
        o_ref[...] = h_ref[...] + gate_ref[0] * _rms(acc_ref[...], gpost_ref[...])


def _ffn_dense(fx, wg, wu, wd, h, gate, g_post, per_batch, tm, tf):
    t, d = h.shape
    dff = wg.shape[1]
    nmod = gate.shape[0]
    tpb = per_batch // tm

    def mod_idx(i, f):
        return ((i // tpb) if nmod > 1 else 0, 0, 0)

    tok = pl.BlockSpec((tm, d), lambda i, f: (i, 0))
    return pl.pallas_call(
        _ffn_kernel,
        grid=(t // tm, dff // tf),
        in_specs=[tok, pl.BlockSpec((d, tf), lambda i, f: (0, f)), pl.BlockSpec((d, tf), lambda i, f: (0, f)),
                  pl.BlockSpec((tf, d), lambda i, f: (f, 0)), tok, pl.BlockSpec((1, 1, d), mod_idx),
                  pl.BlockSpec((1, d), lambda i, f: (0, 0))],
        out_specs=tok,
        out_shape=jax.ShapeDtypeStruct((t, d), F32),
        scratch_shapes=[pltpu.VMEM((tm, d), F32)],
        compiler_params=_cparams("parallel", "arbitrary"),
        name="ffn_dense",
    )(fx, wg, wu, wd, h, gate, g_post.reshape(1, d))


def _moe_kernel(x_ref, comb_ref, wg_ref, wu_ref, wd_ref, h_ref, gate_ref, gpost_ref, o_ref, acc_ref, acce_ref):
    e = pl.program_id(1)
    f = pl.program_id(2)
    nf = pl.num_programs(2)

    @pl.when((e == 0) & (f == 0))
    def _():
        acc_ref[...] = jnp.zeros_like(acc_ref)

    @pl.when(f == 0)
    def _():
        acce_ref[...] = jnp.zeros_like(acce_ref)

    x = x_ref[...]
    g = _dot(x, wg_ref[0])
    u = _dot(x, wu_ref[0])
    hid = (g / (1.0 + jnp.exp(-g)) * u).astype(BF16)
    acce_ref[...] += _dot(hid, wd_ref[0])

    @pl.when(f == nf - 1)
    def _():
        comb = comb_ref[...]
        lane = lax.broadcasted_iota(jnp.int32, comb.shape, 1)
        ce = jnp.sum(jnp.where(lane == e, comb, 0.0), axis=-1, keepdims=True)
        acc_ref[...] += ce * acce_ref[...]

    @pl.when((f == nf - 1) & (e == pl.num_programs(1) - 1))
    def _():
        o_ref[...] = h_ref[...] + gate_ref[0] * _rms(acc_ref[...], gpost_ref[...])


def _ffn_moe(fx, comb, wg, wu, wd, h, gate, g_post, per_batch, tm, tf):
    t, d = h.shape
    ne, _, dff = wg.shape
    nmod = gate.shape[0]
    tpb = per_batch // tm

    def mod_idx(i, e, f):
        return ((i // tpb) if nmod > 1 else 0, 0, 0)

    tok = pl.BlockSpec((tm, d), lambda i, e, f: (i, 0))
    return pl.pallas_call(
        _moe_kernel,
        grid=(t // tm, ne, dff // tf),
        in_specs=[tok, pl.BlockSpec((tm, LANES), lambda i, e, f: (i, 0)),
                  pl.BlockSpec((1, d, tf), lambda i, e, f: (e, 0, f)), pl.BlockSpec((1, d, tf), lambda i, e, f: (e, 0, f)),
                  pl.BlockSpec((1, tf, d), lambda i, e, f: (e, f, 0)), tok, pl.BlockSpec((1, 1, d), mod_idx),
                  pl.BlockSpec((1, d), lambda i, e, f: (0, 0))],
        out_specs=tok,
        out_shape=jax.ShapeDtypeStruct((t, d), F32),
        scratch_shapes=[pltpu.VMEM((tm, d), F32), pltpu.VMEM((tm, d), F32)],
        compiler_params=_cparams("parallel", "arbitrary", "arbitrary"),
        name="ffn_moe",
    )(fx, comb, wg, wu, wd, h, gate, g_post.reshape(1, d))


def _rope_tables(n, rot_dim):
    n_freq = rot_dim // 4
    inv_freq = ROPE_THETA ** (-jnp.arange(n_freq, dtype=F32) / n_freq)
    rows = n // GRID_W
    row = jnp.repeat(jnp.arange(rows, dtype=F32), GRID_W)
    col = jnp.tile(jnp.arange(GRID_W, dtype=F32), rows)
    ang = jnp.concatenate([row[:, None] * inv_freq, col[:, None] * inv_freq], axis=-1)
    return jnp.repeat(jnp.cos(ang), 2, axis=-1), jnp.repeat(jnp.sin(ang), 2, axis=-1)


def _pair_rot(w):
    k, c = w.shape
    wp = w.reshape(k, c // 2, 2)
    return jnp.stack([-wp[..., 1], wp[..., 0]], axis=-1).reshape(k, c)


def _latent_tables(n):
    cw, sw = _rope_tables(n, WIN_HEAD_DIM)
    cm, sm = _rope_tables(n, MLA_ROPE)
    one = jnp.ones((n, MLA_NOPE), F32)
    zero_n = jnp.zeros((n, MLA_NOPE), F32)
    zero_p = jnp.zeros((n, MLA_HEAD_PAD - MLA_QK), F32)
    return (jnp.tile(cw, (1, WIN_HEADS)), jnp.tile(sw, (1, WIN_HEADS)),
            jnp.concatenate([one, cm, zero_p], axis=1), jnp.concatenate([zero_n, sm, zero_p], axis=1))


def _context_tables(lc):
    one_w = jnp.ones((lc, WIN_HEADS * WIN_HEAD_DIM), F32)
    cm = jnp.concatenate([jnp.ones((lc, MLA_QK), F32), jnp.zeros((lc, MLA_HEAD_PAD - MLA_QK), F32)], axis=1)
    return one_w, jnp.zeros_like(one_w), cm, jnp.zeros_like(cm)


def _prep_layer_weights(w_in, mla_w_uq, mla_w_ukv):
    d = w_in.shape[0]
    o = np.cumsum((0, 256, 256, 128, 128, 256, 128, 32, 512))
    a, q, k, v, cq, ckv, kr, sg = [w_in[:, o[i]:o[i + 1]] for i in range(8)]
    zl = jnp.zeros((d, MLA_NOPE), w_in.dtype)
    zr = jnp.zeros((d, MLA_HEAD_PAD - MLA_QK), w_in.dtype)
    krp = jnp.concatenate([zl, kr, zr], axis=1)
    krpr = jnp.concatenate([zl, _pair_rot(kr), zr], axis=1)
    w_aug = jnp.concatenate([a, q, _pair_rot(q), k, _pair_rot(k), v, cq, ckv, krp, krpr, sg], axis=1).astype(BF16)

    wq = mla_w_uq.reshape(MLA_Q_RANK, MLA_HEADS, MLA_QK)
    zq = jnp.zeros((MLA_Q_RANK, MLA_HEADS, MLA_HEAD_PAD - MLA_QK), wq.dtype)
    wq1 = jnp.concatenate([wq, zq], axis=-1).reshape(MLA_Q_RANK, MLA_HEADS * MLA_HEAD_PAD).astype(BF16)
    rope_rot = _pair_rot(wq[..., MLA_NOPE:].reshape(MLA_Q_RANK, MLA_HEADS * MLA_ROPE)).reshape(MLA_Q_RANK, MLA_HEADS, MLA_ROPE)
    wq2 = jnp.concatenate([jnp.zeros((MLA_Q_RANK, MLA_HEADS, MLA_NOPE), wq.dtype), rope_rot, zq], axis=-1)
    wq2 = wq2.reshape(MLA_Q_RANK, MLA_HEADS * MLA_HEAD_PAD).astype(BF16)

    wkv = mla_w_ukv.reshape(MLA_KV_RANK, MLA_HEADS, MLA_NOPE + MLA_V)
    zk = jnp.zeros((MLA_KV_RANK, MLA_HEADS, MLA_HEAD_PAD - MLA_NOPE), wkv.dtype)
    wk = jnp.concatenate([wkv[..., :MLA_NOPE], zk], axis=-1).reshape(MLA_KV_RANK, MLA_HEADS * MLA_HEAD_PAD).astype(BF16)
    wv = wkv[..., MLA_NOPE:].reshape(MLA_KV_RANK, MLA_HEADS * MLA_V).astype(BF16)
    return w_aug, wq1, wq2, wk, wv


def _token_tile(per_batch, pref):
    return min(pref, per_batch)


def kernel(x, c, ctx, c_ctx, w_mod, b_mod, g_pre_mix, g_post_mix, g_pre_ffn, g_post_ffn, w_in, w_out, w_fourier, win_sink, mla_g_q, mla_g_kv, mla_w_uq, mla_w_ukv, sgu_w, sgu_b, ffn_w_gate, ffn_w_up, ffn_w_down, moe_w_router, moe_w_gate, moe_w_up, moe_w_down):
    b, n, d = x.shape
    lc = ctx.shape[1]
    depth = w_mod.shape[0]
    tx, tcx = b * n, b * lc
    tm_x = _token_tile(n, 512)
    tm_c = _token_tile(lc, 512)
    tab_x = _latent_tables(n)
    tab_c = _context_tables(lc)
    cond = jnp.concatenate([c, c_ctx[None, :], jnp.zeros((8 - (b + 1) % 8 if (b + 1) % 8 else 0, d), F32)], axis=0)

    hx = x.reshape(tx, d)
    hc = ctx.reshape(tcx, d)
    for layer in range(depth):
        last = layer == depth - 1
        mod = _modulation(cond, w_mod[layer], b_mod[layer])
        mx6 = mod[:b].reshape(b, 1, 6, d)
        mc6 = mod[b:b + 1].reshape(1, 1, 6, d)
        sh_a, sc_a, gt_a, sh_f, sc_f, gt_f = [mx6[:, :, i] for i in range(6)]
        csh_a, csc_a, cgt_a, csh_f, csc_f, cgt_f = [mc6[:, :, i] for i in range(6)]

        w_aug, wq1, wq2, wk, wv = _prep_layer_weights(w_in[layer], mla_w_uq[layer], mla_w_ukv[layer])
        w_f = w_fourier[layer].astype(BF16)
        w_o = w_out[layer].astype(BF16)
        w_s = sgu_w[layer].astype(BF16)
        bias_full = jnp.repeat(sgu_b[layer].T, SGU_CH, axis=1)
        sink = win_sink[layer]
        gq, gkv = mla_g_q[layer], mla_g_kv[layer]

        px = _inproj(hx, g_pre_mix[layer], sh_a, sc_a, w_aug, tab_x, gq, gkv, wq1, wq2, wk, wv, n, tm_x)
        pc = _inproj(hc, g_pre_mix[layer], csh_a, csc_a, w_aug, tab_c, gq, gkv, wq1, wq2, wk, wv, lc, tm_c)
        a_x, qw_x, kw_x, vw_x, qm_x, km_x, vm_x, u_x, vn_x = [t.reshape(b, n, -1) for t in px]
        a_c, qw_c, kw_c, vw_c, qm_c, km_c, vm_c, u_c, vn_c = [t.reshape(b, lc, -1) for t in pc]

        four_x = _fourier_latent(a_x, w_f)
        win_x = _window_latent(qw_x, kw_x, vw_x, kw_c, vw_c, sink)
        mla_x = _mla_attention(qm_x, km_x, vm_x, km_c, vm_c)
        sgu_x = _sgu(u_x, vn_x, w_s, bias_full)
        parts_x = [t.reshape(tx, GROUP_W) for t in (four_x, win_x, mla_x, sgu_x)]

        i = layer // 2
        dense = layer % 2 == 0
        w_router = None
        if not dense:
            w_router = jnp.concatenate([moe_w_router[i], jnp.zeros((d, LANES - N_EXPERTS), F32)], axis=1)
        res = _outproj(parts_x, w_o, hx, gt_a, g_post_mix[layer], g_pre_ffn[layer], sh_f, sc_f, n, tm_x, w_router)
        if dense:
            wg, wu, wd = ffn_w_gate[i].astype(BF16), ffn_w_up[i].astype(BF16), ffn_w_down[i].astype(BF16)
            hx = _ffn_dense(res[1], wg, wu, wd, res[0], gt_f, g_post_ffn[layer], n, min(1024, n), 512)
        else:
            wg, wu, wd = moe_w_gate[i].astype(BF16), moe_w_up[i].astype(BF16), moe_w_down[i].astype(BF16)
            hx = _ffn_moe(res[1], res[2], wg, wu, wd, res[0], gt_f, g_post_ffn[layer], n, min(1024, n), 512)

        if not last:
            four_c = _fourier_context(a_c, w_f)
            win_c = _window_context(qw_c, kw_c, vw_c, sink)
            mla_c = _mla_attention(qm_c, km_c, vm_c)
            sgu_c = _sgu(u_c, vn_c, w_s, bias_full)
            parts_c = [t.reshape(tcx, GROUP_W) for t in (four_c, win_c, mla_c, sgu_c)]
            res_c = _outproj(parts_c, w_o, hc, cgt_a, g_post_mix[layer], g_pre_ffn[layer], csh_f, csc_f, lc, tm_c, w_router)
            if dense:
                hc = _ffn_dense(res_c[1], wg, wu, wd, res_c[0], cgt_f, g_post_ffn[layer], lc, tm_c, 512)
            else:
                hc = _ffn_moe(res_c[1], res_c[2], wg, wu, wd, res_c[0], cgt_f, g_post_ffn[layer], lc, tm_c, 512)
    return hx.reshape(b, n, d)
```

```python
import functools

import numpy as np
import jax
import jax.numpy as jnp
from jax import lax
from jax.experimental import pallas as pl
from jax.experimental.pallas import tpu as pltpu

F32 = jnp.float32
BF16 = jnp.bfloat16
HIGHEST = lax.Precision.HIGHEST

GRID_W = 64
GROUP_W = 256
FNET_GROUPS = 4
FNET_CH = 64
WIN_HEADS = 4
WIN_KV_HEADS = 2
WIN_HEAD_DIM = 64
WINDOW = 128
BLOCK = 128
MLA_HEADS = 4
MLA_NOPE = 64
MLA_ROPE = 32
MLA_QK = MLA_NOPE + MLA_ROPE
MLA_V = 64
MLA_Q_RANK = 256
MLA_KV_RANK = 128
SGU_GROUPS = 4
SGU_CHUNK = 128
SGU_CH = 64
N_EXPERTS = 8
ROPE_THETA = 10000.0
EPS = 1e-6
NEG_INF = -1e30
LOG2E = 1.4426950408889634

LANES = 128
MLA_HEAD_PAD = 128
FFT_N1 = 128
MOE_TILE = 1024
VMEM_LIMIT = 56 * 1024 * 1024

_O_A, _O_Q, _O_QR, _O_K, _O_KR, _O_V, _O_CQ, _O_CKV, _O_KRP, _O_KRPR, _O_SG, _O_END = (
    0, 256, 512, 768, 896, 1024, 1152, 1408, 1536, 1664, 1792, 2304)


def _cparams(*sem):
    return pltpu.CompilerParams(dimension_semantics=sem, vmem_limit_bytes=VMEM_LIMIT)


def _full(shape):
    nd = len(shape)
    return pl.BlockSpec(shape, lambda *_: (0,) * nd)


def _dot(a, b):
    return jnp.dot(a, b, preferred_element_type=F32)


def _dot_nt(a, b):
    return lax.dot_general(a, b, (((1,), (1,)), ((), ())), preferred_element_type=F32)


def _dot2(hi, lo, x):
    return _dot(hi, x) + _dot(lo, x)


def _split_hi_lo(m):
    m = np.asarray(m, np.float32)
    hi = jnp.asarray(m, F32).astype(BF16)
    lo = (jnp.asarray(m, F32) - hi.astype(F32)).astype(BF16)
    return hi, lo


def _rms(x, g):
    return x * lax.rsqrt(jnp.mean(x * x, axis=-1, keepdims=True) + EPS) * g


def _mod_kernel(c_ref, w_ref, b_ref, o_ref):
    c = c_ref[...]
    s = c / (1.0 + jnp.exp(-c))
    o_ref[...] = jnp.dot(s, w_ref[...], preferred_element_type=F32, precision=HIGHEST) + b_ref[...]


def _modulation(cond, w_mod, b_mod):
    r, d = cond.shape
    n6 = w_mod.shape[1]
    tn = 512
    return pl.pallas_call(
        _mod_kernel,
        grid=(n6 // tn,),
        in_specs=[_full((r, d)), pl.BlockSpec((d, tn), lambda j: (0, j)), pl.BlockSpec((1, tn), lambda j: (0, j))],
        out_specs=pl.BlockSpec((r, tn), lambda j: (0, j)),
        out_shape=jax.ShapeDtypeStruct((r, n6), F32),
        compiler_params=_cparams("arbitrary"),
        name="adaln_mod",
    )(cond, w_mod, b_mod.reshape(1, n6))


def _inproj_kernel(h_ref, g_ref, sh_ref, sc_ref, w_ref, cw_ref, sw_ref, cm_ref, sm_ref,
                   gq_ref, gkv_ref, wq1_ref, wq2_ref, wk_ref, wv_ref,
                   a_o, qw_o, kw_o, vw_o, qm_o, km_o, vm_o, u_o, vn_o):
    x = h_ref[...]
    ax = (_rms(x, g_ref[...]) * (1.0 + sc_ref[0]) + sh_ref[0]).astype(BF16)

    def proj(lo, hi):
        return _dot(ax, w_ref[:, lo:hi])

    a_o[...] = proj(_O_A, _O_Q).astype(BF16)
    cw = cw_ref[...]
    sw = sw_ref[...]
    qw = (proj(_O_Q, _O_QR) * cw + proj(_O_QR, _O_K) * sw) * (WIN_HEAD_DIM ** -0.5)
    qw_o[...] = qw.astype(BF16)
    nk = WIN_KV_HEADS * WIN_HEAD_DIM
    kw_o[...] = (proj(_O_K, _O_KR) * cw[:, :nk] + proj(_O_KR, _O_V) * sw[:, :nk]).astype(BF16)
    vw_o[...] = proj(_O_V, _O_CQ).astype(BF16)

    cm = cm_ref[...]
    sm = sm_ref[...]
    cqn = _rms(proj(_O_CQ, _O_CKV), gq_ref[...]).astype(BF16)
    y1 = _dot(cqn, wq1_ref[...])
    y2 = _dot(cqn, wq2_ref[...])
    scale = (MLA_QK ** -0.5) * LOG2E
    for h in range(MLA_HEADS):
        sl = slice(h * MLA_HEAD_PAD, (h + 1) * MLA_HEAD_PAD)
        qm_o[:, sl] = ((y1[:, sl] * cm + y2[:, sl] * sm) * scale).astype(BF16)

    ckvn = _rms(proj(_O_CKV, _O_KRP), gkv_ref[...]).astype(BF16)
    krp = proj(_O_KRP, _O_KRPR) * cm + proj(_O_KRPR, _O_SG) * sm
    kk = _dot(ckvn, wk_ref[...])
    for h in range(MLA_HEADS):
        sl = slice(h * MLA_HEAD_PAD, (h + 1) * MLA_HEAD_PAD)
        km_o[:, sl] = (kk[:, sl] + krp).astype(BF16)
    vv = _dot(ckvn, wv_ref[...])
    lane = lax.broadcasted_iota(jnp.int32, vv.shape, 1) % MLA_HEAD_PAD
    vm_o[...] = jnp.where(lane == MLA_V, 1.0, vv).astype(BF16)

    sg = proj(_O_SG, _O_END)
    gl = 0.5 * sg * (1.0 + lax.erf(sg * (2.0 ** -0.5)))
    u_o[...] = gl[:, :GROUP_W].astype(BF16)
    v = gl[:, GROUP_W:]
    mu = jnp.mean(v, axis=-1, keepdims=True)
    vc = v - mu
    var = jnp.mean(vc * vc, axis=-1, keepdims=True)
    vn_o[...] = (vc * lax.rsqrt(var + EPS)).astype(BF16)


def _inproj(h, g, shift, scale, w_aug, tables, gq, gkv, wq1, wq2, wk, wv, per_batch, tm):
    t, d = h.shape
    nmod = shift.shape[0]
    tpb = per_batch // tm
    cw, sw, cm, sm = tables

    def mod_idx(i):
        return ((i // tpb) if nmod > 1 else 0, 0, 0)

    tok = lambda w: pl.BlockSpec((tm, w), lambda i: (i, 0))
    tab = lambda w: pl.BlockSpec((tm, w), lambda i: (i % tpb, 0))
    in_specs = [tok(d), _full((1, d)), pl.BlockSpec((1, 1, d), mod_idx), pl.BlockSpec((1, 1, d), mod_idx),
                _full(w_aug.shape), tab(256), tab(256), tab(LANES), tab(LANES),
                _full((1, MLA_Q_RANK)), _full((1, MLA_KV_RANK)),
                _full(wq1.shape), _full(wq2.shape), _full(wk.shape), _full(wv.shape)]
    widths = (256, 256, 128, 128, 512, 512, 512, 256, 256)
    return pl.pallas_call(
        _inproj_kernel,
        grid=(t // tm,),
        in_specs=in_specs,
        out_specs=[tok(w) for w in widths],
        out_shape=[jax.ShapeDtypeStruct((t, w), BF16) for w in widths],
        compiler_params=_cparams("parallel"),
        name="inproj",
    )(h, g.reshape(1, d), shift, scale, w_aug, cw, sw, cm, sm, gq.reshape(1, -1), gkv.reshape(1, -1), wq1, wq2, wk, wv)


def _fft1_kernel(x_ref, mh_ref, ml_ref, tr_ref, ti_ref, or_ref, oi_ref):
    x = x_ref[0]
    z = _dot2(mh_ref[...], ml_ref[...], x)
    ar, ai = z[:FFT_N1], z[FFT_N1:]
    tr, ti = tr_ref[...], ti_ref[...]
    or_ref[0] = (ar * tr - ai * ti).astype(BF16)
    oi_ref[0] = (ar * ti + ai * tr).astype(BF16)


def _fft2_kernel(r_ref, i_ref, ch_ref, cl_ref, sh_ref, sl_ref, bch_ref, bcl_ref, bsh_ref, bsl_ref, wf_ref, o_ref, *, nj):
    ch, cl, sh, sl = ch_ref[...], cl_ref[...], sh_ref[...], sl_ref[...]
    for j in range(nj):
        r = r_ref[0, j]
        i = i_ref[0, j]
        gr = (_dot2(ch, cl, r) + _dot2(sh, sl, i)).astype(BF16)
        gi = (_dot2(ch, cl, i) - _dot2(sh, sl, r)).astype(BF16)
        y = _dot(gr, bch_ref[...]) + _dot(gr, bcl_ref[...]) + _dot(gi, bsh_ref[...]) + _dot(gi, bsl_ref[...])
        o_ref[0, j] = _dot(y.astype(BF16), wf_ref[...]).astype(BF16)


def _dft_small_kernel(a_ref, ch_ref, cl_ref, sh_ref, sl_ref, bch_ref, bcl_ref, bsh_ref, bsl_ref, wf_ref, o_ref):
    a = a_ref[0]
    gr = _dot2(ch_ref[...], cl_ref[...], a).astype(BF16)
    gi = (-_dot2(sh_ref[...], sl_ref[...], a)).astype(BF16)
    y = _dot(gr, bch_ref[...]) + _dot(gr, bcl_ref[...]) + _dot(gi, bsh_ref[...]) + _dot(gi, bsl_ref[...])
    o_ref[0] = _dot(y.astype(BF16), wf_ref[...]).astype(BF16)


def _cos_sin(n):
    k = np.arange(n)
    ang = 2.0 * np.pi * ((k[:, None] * k[None, :]) % n) / n
    return np.cos(ang), np.sin(ang)


def _channel_dft(norm):
    c, s = _cos_sin(FNET_CH)
    bc = np.kron(np.eye(FNET_GROUPS), c) * norm
    bs = np.kron(np.eye(FNET_GROUPS), s) * norm
    return _split_hi_lo(bc) + _split_hi_lo(bs)


def _fourier_latent(a, w_f):
    b, n, gw = a.shape
    n2 = n // FFT_N1
    c1, s1 = _cos_sin(FFT_N1)
    mh, ml = _split_hi_lo(np.concatenate([c1, -s1], axis=0))
    k1 = np.arange(FFT_N1)[:, None]
    t2 = np.arange(n2)[None, :]
    ang = 2.0 * np.pi * ((k1 * t2) % n) / n
    tr = jnp.repeat(jnp.asarray(np.cos(ang), F32), gw, axis=1)
    ti = jnp.repeat(jnp.asarray(-np.sin(ang), F32), gw, axis=1)
    cols = n2 * gw
    tc = min(cols, 4096)
    x2 = a.reshape(b, FFT_N1, cols)
    blk = pl.BlockSpec((1, FFT_N1, tc), lambda bi, ci: (bi, 0, ci))
    tblk = pl.BlockSpec((FFT_N1, tc), lambda bi, ci: (0, ci))
    o_r, o_i = pl.pallas_call(
        _fft1_kernel,
        grid=(b, cols // tc),
        in_specs=[blk, _full(mh.shape), _full(ml.shape), tblk, tblk],
        out_specs=[blk, blk],
        out_shape=[jax.ShapeDtypeStruct((b, FFT_N1, cols), BF16)] * 2,
        compiler_params=_cparams("parallel", "parallel"),
        name="fft_stage1",
    )(x2, mh, ml, tr, ti)

    c2, s2 = _cos_sin(n2)
    consts = _split_hi_lo(c2) + _split_hi_lo(s2) + _channel_dft((n * FNET_CH) ** -0.5)
    nj = 8
    r4 = o_r.reshape(b, FFT_N1, n2, gw)
    i4 = o_i.reshape(b, FFT_N1, n2, gw)
    blk4 = pl.BlockSpec((1, nj, n2, gw), lambda bi, ki: (bi, ki, 0, 0))
    out = pl.pallas_call(
        functools.partial(_fft2_kernel, nj=nj),
        grid=(b, FFT_N1 // nj),
        in_specs=[blk4, blk4] + [_full(m.shape) for m in consts] + [_full(w_f.shape)],
        out_specs=blk4,
        out_shape=jax.ShapeDtypeStruct((b, FFT_N1, n2, gw), BF16),
        compiler_params=_cparams("parallel", "parallel"),
        name="fft_stage2",
    )(r4, i4, *consts, w_f)
    return jnp.swapaxes(out, 1, 2).reshape(b, n, gw)


def _fourier_context(a, w_f):
    b, n, gw = a.shape
    c, s = _cos_sin(n)
    consts = _split_hi_lo(c) + _split_hi_lo(s) + _channel_dft((n * FNET_CH) ** -0.5)
    blk = pl.BlockSpec((1, n, gw), lambda bi: (bi, 0, 0))
    return pl.pallas_call(
        _dft_small_kernel,
        grid=(b,),
        in_specs=[blk] + [_full(m.shape) for m in consts] + [_full(w_f.shape)],
        out_specs=blk,
        out_shape=jax.ShapeDtypeStruct((b, n, gw), BF16),
        compiler_params=_cparams("parallel"),
        name="dft_context",
    )(a, *consts, w_f)


def _win_heads(q_blk, k_win, v_win, valid, kc, vc, sink_ref):
    rows = lax.broadcasted_iota(jnp.int32, (WIN_KV_HEADS * BLOCK, 1), 0)
    outs = [None] * WIN_HEADS
    d = WIN_HEAD_DIM
    for hk in range(WIN_KV_HEADS):
        h0 = 2 * hk
        q2 = jnp.concatenate([q_blk[:, h0 * d:(h0 + 1) * d], q_blk[:, (h0 + 1) * d:(h0 + 2) * d]], axis=0)
        ksl = slice(hk * d, (hk + 1) * d)
        sink = jnp.where(rows < BLOCK, sink_ref[h0], sink_ref[h0 + 1])
        s_c = _dot_nt(q2, kc[:, ksl])
        m = jnp.maximum(jnp.max(s_c, axis=-1, keepdims=True), sink)
        if k_win is not None:
            s_w = jnp.where(valid, _dot_nt(q2, k_win[:, ksl]), NEG_INF)
            m = jnp.maximum(m, jnp.max(s_w, axis=-1, keepdims=True))
        p_c = jnp.exp(s_c - m)
        den = jnp.sum(p_c, axis=-1, keepdims=True) + jnp.exp(sink - m)
        o = _dot(p_c.astype(BF16), vc[:, ksl])
        if k_win is not None:
            p_w = jnp.exp(s_w - m)
            den = den + jnp.sum(p_w, axis=-1, keepdims=True)
            o = o + _dot(p_w.astype(BF16), v_win[:, ksl])
        o = o / den
        outs[h0] = o[:BLOCK]
        outs[h0 + 1] = o[BLOCK:]
    return jnp.concatenate(outs, axis=1)


def _win_kernel(sink_ref, q_ref, kp_ref, k_ref, kn_ref, vp_ref, v_ref, vn_ref, kc_ref, vc_ref, o_ref, *, nblk, n):
    i = pl.program_id(1)
    kfull = jnp.concatenate([kp_ref[0], k_ref[0], kn_ref[0]], axis=0)
    vfull = jnp.concatenate([vp_ref[0], v_ref[0], vn_ref[0]], axis=0)
    kc = kc_ref[0]
    vc = vc_ref[0]
    shape = (WIN_KV_HEADS * BLOCK, 3 * BLOCK)
    qi = lax.broadcasted_iota(jnp.int32, shape, 0) % BLOCK
    kj = lax.broadcasted_iota(jnp.int32, shape, 1)
    band = jnp.abs(kj - BLOCK - qi) <= WINDOW
    for j in range(nblk):
        kpos = (i * nblk + j - 1) * BLOCK + kj
        valid = band & (kpos >= 0) & (kpos < n)
        o = _win_heads(q_ref[0, j * BLOCK:(j + 1) * BLOCK, :], kfull[j * BLOCK:(j + 3) * BLOCK],
                       vfull[j * BLOCK:(j + 3) * BLOCK], valid, kc, vc, sink_ref)
        o_ref[0, j * BLOCK:(j + 1) * BLOCK, :] = o.astype(BF16)


def _winctx_kernel(sink_ref, q_ref, kc_ref, vc_ref, o_ref, *, nblk):
    kc = kc_ref[0]
    vc = vc_ref[0]
    for j in range(nblk):
        o = _win_heads(q_ref[0, j * BLOCK:(j + 1) * BLOCK, :], None, None, None, kc, vc, sink_ref)
        o_ref[0, j * BLOCK:(j + 1) * BLOCK, :] = o.astype(BF16)


def _window_latent(q, k, v, kc, vc, sink):
    b, n, _ = q.shape
    lc = kc.shape[1]
    nblk = min(8, n // BLOCK)
    tq = nblk * BLOCK
    nb = n // BLOCK
    kvw = WIN_KV_HEADS * WIN_HEAD_DIM
    cur = lambda w: pl.BlockSpec((1, tq, w), lambda bi, i: (bi, i, 0))
    prev = pl.BlockSpec((1, BLOCK, kvw), lambda bi, i: (bi, jnp.maximum(i * nblk - 1, 0), 0))
    nxt = pl.BlockSpec((1, BLOCK, kvw), lambda bi, i: (bi, jnp.minimum((i + 1) * nblk, nb - 1), 0))
    ctx = pl.BlockSpec((1, lc, kvw), lambda bi, i: (bi, 0, 0))
    smem = pl.BlockSpec(memory_space=pltpu.SMEM)
    return pl.pallas_call(
        functools.partial(_win_kernel, nblk=nblk, n=n),
        grid=(b, n // tq),
        in_specs=[smem, cur(256), prev, cur(kvw), nxt, prev, cur(kvw), nxt, ctx, ctx],
        out_specs=cur(256),
        out_shape=jax.ShapeDtypeStruct((b, n, 256), BF16),
        compiler_params=_cparams("parallel", "parallel"),
        name="window_attn",
    )(sink, q, k, k, k, v, v, v, kc, vc)


def _window_context(q, kc, vc, sink):
    b, lc, _ = q.shape
    kvw = WIN_KV_HEADS * WIN_HEAD_DIM
    blk = lambda w: pl.BlockSpec((1, lc, w), lambda bi: (bi, 0, 0))
    smem = pl.BlockSpec(memory_space=pltpu.SMEM)
    return pl.pallas_call(
        functools.partial(_winctx_kernel, nblk=lc // BLOCK),
        grid=(b,),
        in_specs=[smem, blk(256), blk(kvw), blk(kvw)],
        out_specs=blk(256),
        out_shape=jax.ShapeDtypeStruct((b, lc, 256), BF16),
        compiler_params=_cparams("parallel"),
        name="window_attn_ctx",
    )(sink, q, kc, vc)


def _mla_reduce(state, s, v):
    m, acc = state
    m_new = jnp.maximum(m, jnp.max(s, axis=-1, keepdims=True))
    alpha = jnp.exp2(m - m_new)
    p = jnp.exp2(s - m_new).astype(BF16)
    return m_new, alpha * acc + _dot(p, v)


def _mla_finish(acc):
    return (acc[:, :MLA_V] / acc[:, MLA_V:MLA_V + 1]).astype(BF16)


def _mla_kernel(q_ref, k_ref, v_ref, kc_ref, vc_ref, o_ref, s_scr, sc_scr, *, tk, nk):
    tq = q_ref.shape[1]
    hsl = [slice(h * MLA_HEAD_PAD, (h + 1) * MLA_HEAD_PAD) for h in range(2)]
    qs = [q_ref[0, :, sl] for sl in hsl]

    def scores(c, slot):
        off = pl.multiple_of(c * tk, tk)
        for h in range(2):
            s_scr[slot, h] = _dot_nt(qs[h], k_ref[0, pl.ds(off, tk), hsl[h]])

    def reduce(c, slot, states):
        off = pl.multiple_of(c * tk, tk)
        return tuple(_mla_reduce(states[h], s_scr[slot, h], v_ref[0, pl.ds(off, tk), hsl[h]]) for h in range(2))

    scores(0, 0)

    def body(i, states):
        scores(2 * i + 1, 1)
        states = reduce(2 * i, 0, states)
        scores(2 * i + 2, 0)
        return reduce(2 * i + 1, 1, states)

    init = tuple((jnp.full((tq, 1), NEG_INF, F32), jnp.zeros((tq, MLA_HEAD_PAD), F32)) for _ in range(2))
    states = lax.fori_loop(0, nk // 2 - 1, body, init)
    scores(nk - 1, 1)
    states = reduce(nk - 2, 0, states)
    for h in range(2):
        sc_scr[h] = _dot_nt(qs[h], kc_ref[0, :, hsl[h]])
    states = reduce(nk - 1, 1, states)
    for h in range(2):
        _, acc = _mla_reduce(states[h], sc_scr[h], vc_ref[0, :, hsl[h]])
        o_ref[0, :, h * MLA_V:(h + 1) * MLA_V] = _mla_finish(acc)


def _mla_small_kernel(q_ref, k_ref, v_ref, o_ref):
    tq = q_ref.shape[1]
    for h in range(2):
        sl = slice(h * MLA_HEAD_PAD, (h + 1) * MLA_HEAD_PAD)
        init = (jnp.full((tq, 1), NEG_INF, F32), jnp.zeros((tq, MLA_HEAD_PAD), F32))
        _, acc = _mla_reduce(init, _dot_nt(q_ref[0, :, sl], k_ref[0, :, sl]), v_ref[0, :, sl])
        o_ref[0, :, h * MLA_V:(h + 1) * MLA_V] = _mla_finish(acc)


def _mla_attention(q, k, v, kc, vc):
    b, nq, _ = q.shape
    nkeys = k.shape[1]
    lc = kc.shape[1]
    tq = min(512, nq)
    tk = min(1024, nkeys // 2)
    assert nkeys % (2 * tk) == 0 and nq % tq == 0
    w2 = 2 * MLA_HEAD_PAD
    res = lambda n: pl.BlockSpec((1, n, w2), lambda bi, h, i: (bi, 0, h))
    return pl.pallas_call(
        functools.partial(_mla_kernel, tk=tk, nk=nkeys // tk),
        grid=(b, MLA_HEADS // 2, nq // tq),
        in_specs=[pl.BlockSpec((1, tq, w2), lambda bi, h, i: (bi, i, h)), res(nkeys), res(nkeys), res(lc), res(lc)],
        out_specs=pl.BlockSpec((1, tq, 2 * MLA_V), lambda bi, h, i: (bi, i, h)),
        out_shape=jax.ShapeDtypeStruct((b, nq, MLA_HEADS * MLA_V), BF16),
        scratch_shapes=[pltpu.VMEM((2, 2, tq, tk), F32), pltpu.VMEM((2, tq, lc), F32)],
        compiler_params=_cparams("parallel", "parallel", "arbitrary"),
        name="mla_attn",
    )(q, k, v, kc, vc)


def _mla_attention_small(q, k, v):
    b, n, _ = q.shape
    w2 = 2 * MLA_HEAD_PAD
    blk = pl.BlockSpec((1, n, w2), lambda bi, h: (bi, 0, h))
    return pl.pallas_call(
        _mla_small_kernel,
        grid=(b, MLA_HEADS // 2),
        in_specs=[blk, blk, blk],
        out_specs=pl.BlockSpec((1, n, 2 * MLA_V), lambda bi, h: (bi, 0, h)),
        out_shape=jax.ShapeDtypeStruct((b, n, MLA_HEADS * MLA_V), BF16),
        compiler_params=_cparams("parallel", "parallel"),
        name="mla_attn_ctx",
    )(q, k, v)


def _sgu_kernel(u_ref, v_ref, w_ref, b_ref, o_ref, *, nchunk):
    lane = lax.broadcasted_iota(jnp.int32, (SGU_CHUNK, GROUP_W), 1) // SGU_CH
    bias = b_ref[...]
    for c in range(nchunk):
        sl = slice(c * SGU_CHUNK, (c + 1) * SGU_CHUNK)
        v = v_ref[0, sl, :]
        s = bias
        for g in range(SGU_GROUPS):
            s = s + _dot(w_ref[g], jnp.where(lane == g, v, jnp.zeros_like(v)))
        o_ref[0, sl, :] = (u_ref[0, sl, :].astype(F32) * s).astype(BF16)


def _sgu(u, vn, w_s, bias_full):
    b, n, gw = u.shape
    nchunk = min(8, n // SGU_CHUNK)
    tn = nchunk * SGU_CHUNK
    blk = pl.BlockSpec((1, tn, gw), lambda bi, i: (bi, i, 0))
    return pl.pallas_call(
        functools.partial(_sgu_kernel, nchunk=nchunk),
        grid=(b, n // tn),
        in_specs=[blk, blk, _full(w_s.shape), _full(bias_full.shape)],
        out_specs=blk,
        out_shape=jax.ShapeDtypeStruct((b, n, gw), BF16),
        compiler_params=_cparams("parallel", "parallel"),
        name="sgu",
    )(u, vn, w_s, bias_full)


def _outproj_kernel(*refs, with_router):
    if with_router:
        (p0, p1, p2, p3, w_ref, h_ref, gate_ref, gpost_ref, gpre_ref, sh_ref, sc_ref, wr_ref,
         hn_o, fx_o, comb_o) = refs
    else:
        p0, p1, p2, p3, w_ref, h_ref, gate_ref, gpost_ref, gpre_ref, sh_ref, sc_ref, hn_o, fx_o = refs
    mx = None
    for idx, p in enumerate((p0, p1, p2, p3)):
        t = _dot(p[...], w_ref[idx * GROUP_W:(idx + 1) * GROUP_W, :])
        mx = t if mx is None else mx + t
    hn = h_ref[...] + gate_ref[0] * _rms(mx, gpost_ref[...])
    hn_o[...] = hn
    fx = _rms(hn, gpre_ref[...]) * (1.0 + sc_ref[0]) + sh_ref[0]
    fx_o[...] = fx.astype(fx_o.dtype)
    if with_router:
        fx_hi = fx.astype(BF16)
        fx_lo = (fx - fx_hi.astype(F32)).astype(BF16)
        logits = _dot(fx_hi, wr_ref[0]) + (_dot(fx_lo, wr_ref[0]) + _dot(fx_hi, wr_ref[1]))
        lane = lax.broadcasted_iota(jnp.int32, logits.shape, 1).astype(F32)
        lg = jnp.where(lane < N_EXPERTS, logits, -jnp.inf)
        m1 = jnp.max(lg, axis=-1, keepdims=True)
        i1 = jnp.min(jnp.where(lg == m1, lane, float(LANES)), axis=-1, keepdims=True)
        lg2 = jnp.where(lane == i1, -jnp.inf, lg)
        m2 = jnp.max(lg2, axis=-1, keepdims=True)
        i2 = jnp.min(jnp.where(lg2 == m2, lane, float(LANES)), axis=-1, keepdims=True)
        e2 = jnp.exp(m2 - m1)
        den = 1.0 + e2
        flag = (lane == i1 + N_EXPERTS) | (lane == i2 + N_EXPERTS)
        comb_o[...] = (jnp.where(lane == i1, 1.0 / den, 0.0) + jnp.where(lane == i2, e2 / den, 0.0)
                       + jnp.where(flag, 1.0, 0.0))


def _outproj(parts, w_out, h, gate, g_post, g_pre, shift, scale, per_batch, tm, w_router=None):
    t, d = h.shape
    nmod = gate.shape[0]
    tpb = per_batch // tm
    with_router = w_router is not None

    def mod_idx(i):
        return ((i // tpb) if nmod > 1 else 0, 0, 0)

    tok = lambda w: pl.BlockSpec((tm, w), lambda i: (i, 0))
    mod = pl.BlockSpec((1, 1, d), mod_idx)
    in_specs = [tok(GROUP_W)] * 4 + [_full(w_out.shape), tok(d), mod, _full((1, d)), _full((1, d)), mod, mod]
    args = list(parts) + [w_out, h, gate, g_post.reshape(1, d), g_pre.reshape(1, d), shift, scale]
    out_specs = [tok(d), tok(d)]
    out_shape = [jax.ShapeDtypeStruct((t, d), F32), jax.ShapeDtypeStruct((t, d), F32 if with_router else BF16)]
    if with_router:
        in_specs.append(_full(w_router.shape))
        args.append(w_router)
        out_specs.append(tok(LANES))
        out_shape.append(jax.ShapeDtypeStruct((t, LANES), F32))
    return pl.pallas_call(
        functools.partial(_outproj_kernel, with_router=with_router),
        grid=(t // tm,),
        in_specs=in_specs,
        out_specs=out_specs,
        out_shape=out_shape,
        compiler_params=_cparams("parallel"),
        name="outproj",
    )(*args)


def _ffn_kernel(x_ref, wg_ref, wu_ref, wd_ref, h_ref, gate_ref, gpost_ref, o_ref, acc_ref):
    f = pl.program_id(1)

    @pl.when(f == 0)
    def _():
        acc_ref[...] = jnp.zeros_like(acc_ref)

    x = x_ref[...]
    g = _dot(x, wg_ref[...])
    u = _dot(x, wu_ref[...])
    hid = (g / (1.0 + jnp.exp(-g)) * u).astype(BF16)
    acc_ref[...] += _dot(hid, wd_ref[...])

    @pl.when(f == pl.num_programs(1) - 1)
    def _():
        o_ref[...] = h_ref[...] + gate_ref[0] * _rms(acc_ref[...], gpost_ref[...])


def _ffn_dense(fx, wg, wu, wd, h, gate, g_post, per_batch, tm, tf):
    t, d = h.shape
    dff = wg.shape[1]
    nmod = gate.shape[0]
    tpb = per_batch // tm

    def mod_idx(i, f):
        return ((i // tpb) if nmod > 1 else 0, 0, 0)

    tok = pl.BlockSpec((tm, d), lambda i, f: (i, 0))
    return pl.pallas_call(
        _ffn_kernel,
        grid=(t // tm, dff // tf),
        in_specs=[tok, pl.BlockSpec((d, tf), lambda i, f: (0, f)), pl.BlockSpec((d, tf), lambda i, f: (0, f)),
                  pl.BlockSpec((tf, d), lambda i, f: (f, 0)), tok, pl.BlockSpec((1, 1, d), mod_idx),
                  pl.BlockSpec((1, d), lambda i, f: (0, 0))],
        out_specs=tok,
        out_shape=jax.ShapeDtypeStruct((t, d), F32),
        scratch_shapes=[pltpu.VMEM((tm, d), F32)],
        compiler_params=_cparams("parallel", "arbitrary"),
        name="ffn_dense",
    )(fx, wg, wu, wd, h, gate, g_post.reshape(1, d))


def _route(comb, tile):
    t = comb.shape[0]
    sel = (comb[:, N_EXPERTS:2 * N_EXPERTS] > 0.5).astype(jnp.int32)
    rank = jnp.cumsum(sel, axis=0) - sel
    padded = (jnp.sum(sel, axis=0) + tile - 1) // tile * tile
    ends = jnp.cumsum(padded)
    pos_e = (ends - padded)[None, :] + rank
    e_lo = jnp.argmax(sel, axis=1)
    e_hi = N_EXPERTS - 1 - jnp.argmax(sel[:, ::-1], axis=1)
    take = lambda a, e: jnp.take_along_axis(a, e[:, None], axis=1)[:, 0]
    pos = jnp.stack([take(pos_e, e_lo), take(pos_e, e_hi)], axis=1).reshape(2 * t).astype(jnp.int32)
    w = comb[:, :N_EXPERTS]
    wts = jnp.pad(jnp.stack([take(w, e_lo), take(w, e_hi)], axis=1), ((0, 0), (0, LANES - 2)))
    ntiles = -(-2 * t // tile) + N_EXPERTS
    nact = (ends[-1] // tile).astype(jnp.int32)
    tile_id = jnp.arange(ntiles, dtype=jnp.int32)
    te = jnp.minimum(jnp.searchsorted(ends, tile_id * tile, side="right"), N_EXPERTS - 1).astype(jnp.int32)
    te = jnp.where(tile_id < nact, te, te[nact - 1])
    return pos, wts, te, nact.reshape(1), ntiles


def _dispatch_kernel(pos_ref, x_ref, xs_in_ref, xs_ref, sem, *, tm):
    del xs_in_ref
    base = pl.program_id(0) * tm

    def row_copy(t, p):
        return pltpu.make_async_copy(x_ref.at[pl.ds(t, 1)], xs_ref.at[pl.ds(p, 1)], sem)

    def issue(t, carry):
        row_copy(t, pos_ref[2 * (base + t)]).start()
        row_copy(t, pos_ref[2 * (base + t) + 1]).start()
        return carry

    def drain(t, carry):
        row_copy(0, 0).wait()
        row_copy(0, 0).wait()
        return carry

    lax.fori_loop(0, tm, issue, 0, unroll=8)
    lax.fori_loop(0, tm, drain, 0, unroll=8)


def _gmm_kernel(te_ref, nact_ref, x_ref, wg_ref, wu_ref, wd_ref, o_ref, xb_ref, acc_ref):
    del te_ref
    f = pl.program_id(1)
    active = pl.program_id(0) < nact_ref[0]

    @pl.when(jnp.logical_not(active) & (f == 0))
    def _():
        o_ref[...] = jnp.zeros_like(o_ref)

    @pl.when(active)
    def _():
        @pl.when(f == 0)
        def _():
            xb_ref[...] = x_ref[...].astype(BF16)
            acc_ref[...] = jnp.zeros_like(acc_ref)

        x = xb_ref[...]
        g = _dot(x, wg_ref[0])
        u = _dot(x, wu_ref[0])
        hid = (g / (1.0 + jnp.exp(-g)) * u).astype(BF16)
        acc_ref[...] += _dot(hid, wd_ref[0])

        @pl.when(f == pl.num_programs(1) - 1)
        def _():
            o_ref[...] = acc_ref[...]


def _combine_kernel(pos_ref, ys_ref, w_ref, h_ref, gate_ref, gpost_ref, o_ref, buf, sem, *, tm):
    base = pl.program_id(0) * tm

    def row_copy(choice, t, p):
        return pltpu.make_async_copy(ys_ref.at[pl.ds(p, 1)], buf.at[choice, pl.ds(t, 1)], sem)

    def issue(t, carry):
        row_copy(0, t, pos_ref[2 * (base + t)]).start()
        row_copy(1, t, pos_ref[2 * (base + t) + 1]).start()
        return carry

    def drain(t, carry):
        row_copy(0, 0, 0).wait()
        row_copy(1, 0, 0).wait()
        return carry

    lax.fori_loop(0, tm, issue, 0, unroll=8)
    lax.fori_loop(0, tm, drain, 0, unroll=8)
    w = w_ref[...]
    y = w[:, 0:1] * buf[0] + w[:, 1:2] * buf[1]
    o_ref[...] = h_ref[...] + gate_ref[0] * _rms(y, gpost_ref[...])


def _ffn_moe(fx, comb, wg, wu, wd, h, gate, g_post, per_batch, tf):
    t, d = h.shape
    dff = wg.shape[2]
    nf = dff // tf
    tile = MOE_TILE
    pos, wts, te, nact, ntiles = _route(comb, tile)
    ts = ntiles * tile
    any_spec = pl.BlockSpec(memory_space=pl.ANY)

    tm_d = min(512, per_batch)
    xs = pl.pallas_call(
        functools.partial(_dispatch_kernel, tm=tm_d),
        grid_spec=pltpu.PrefetchScalarGridSpec(
            num_scalar_prefetch=1, grid=(t // tm_d,),
            in_specs=[pl.BlockSpec((tm_d, d), lambda i, p: (i, 0)), any_spec],
            out_specs=any_spec,
            scratch_shapes=[pltpu.SemaphoreType.DMA(())]),
        out_shape=jax.ShapeDtypeStruct((ts, d), F32),
        input_output_aliases={2: 0},
        compiler_params=_cparams("arbitrary"),
        name="moe_dispatch",
    )(pos, fx, jnp.zeros((ts, d), F32))

    row = lambda i, f, te_r, na_r: (jnp.minimum(i, na_r[0] - 1), 0)
    fidx = lambda i, f, na_r: jnp.where(i < na_r[0], f, nf - 1)
    ys = pl.pallas_call(
        _gmm_kernel,
        grid_spec=pltpu.PrefetchScalarGridSpec(
            num_scalar_prefetch=2, grid=(ntiles, nf),
            in_specs=[pl.BlockSpec((tile, d), row),
                      pl.BlockSpec((1, d, tf), lambda i, f, te_r, na_r: (te_r[i], 0, fidx(i, f, na_r))),
                      pl.BlockSpec((1, d, tf), lambda i, f, te_r, na_r: (te_r[i], 0, fidx(i, f, na_r))),
                      pl.BlockSpec((1, tf, d), lambda i, f, te_r, na_r: (te_r[i], fidx(i, f, na_r), 0))],
            out_specs=pl.BlockSpec((tile, d), lambda i, f, te_r, na_r: (i, 0)),
            scratch_shapes=[pltpu.VMEM((tile, d), BF16), pltpu.VMEM((tile, d), F32)]),
        out_shape=jax.ShapeDtypeStruct((ts, d), F32),
        compiler_params=_cparams("arbitrary", "arbitrary"),
        name="moe_experts",
    )(te, nact, xs, wg, wu, wd)

    tm_c = min(256, per_batch)
    nmod = gate.shape[0]
    tpb = per_batch // tm_c
    tok = lambda w: pl.BlockSpec((tm_c, w), lambda i, p: (i, 0))
    return pl.pallas_call(
        functools.partial(_combine_kernel, tm=tm_c),
        grid_spec=pltpu.PrefetchScalarGridSpec(
            num_scalar_prefetch=1, grid=(t // tm_c,),
            in_specs=[any_spec, tok(LANES), tok(d),
                      pl.BlockSpec((1, 1, d), lambda i, p: ((i // tpb) if nmod > 1 else 0, 0, 0)),
                      pl.BlockSpec((1, d), lambda i, p: (0, 0))],
            out_specs=tok(d),
            scratch_shapes=[pltpu.VMEM((2, tm_c, d), F32), pltpu.SemaphoreType.DMA(())]),
        out_shape=jax.ShapeDtypeStruct((t, d), F32),
        compiler_params=_cparams("arbitrary"),
        name="moe_combine",
    )(pos, ys, wts, h, gate, g_post.reshape(1, d))


def _rope_tables(n, rot_dim):
    n_freq = rot_dim // 4
    inv_freq = ROPE_THETA ** (-jnp.arange(n_freq, dtype=F32) / n_freq)
    rows = n // GRID_W
    row = jnp.repeat(jnp.arange(rows, dtype=F32), GRID_W)
    col = jnp.tile(jnp.arange(GRID_W, dtype=F32), rows)
    ang = jnp.concatenate([row[:, None] * inv_freq, col[:, None] * inv_freq], axis=-1)
    return jnp.repeat(jnp.cos(ang), 2, axis=-1), jnp.repeat(jnp.sin(ang), 2, axis=-1)


def _pair_rot(w):
    k, c = w.shape
    wp = w.reshape(k, c // 2, 2)
    return jnp.stack([-wp[..., 1], wp[..., 0]], axis=-1).reshape(k, c)


def _latent_tables(n):
    cw, sw = _rope_tables(n, WIN_HEAD_DIM)
    cm, sm = _rope_tables(n, MLA_ROPE)
    one = jnp.ones((n, MLA_NOPE), F32)
    zero_n = jnp.zeros((n, MLA_NOPE), F32)
    zero_p = jnp.zeros((n, MLA_HEAD_PAD - MLA_QK), F32)
    return (jnp.tile(cw, (1, WIN_HEADS)), jnp.tile(sw, (1, WIN_HEADS)),
            jnp.concatenate([one, cm, zero_p], axis=1), jnp.concatenate([zero_n, sm, zero_p], axis=1))


def _context_tables(lc):
    one_w = jnp.ones((lc, WIN_HEADS * WIN_HEAD_DIM), F32)
    cm = jnp.concatenate([jnp.ones((lc, MLA_QK), F32), jnp.zeros((lc, MLA_HEAD_PAD - MLA_QK), F32)], axis=1)
    return one_w, jnp.zeros_like(one_w), cm, jnp.zeros_like(cm)


def _prep_layer_weights(w_in, mla_w_uq, mla_w_ukv):
    d = w_in.shape[0]
    o = np.cumsum((0, 256, 256, 128, 128, 256, 128, 32, 512))
    a, q, k, v, cq, ckv, kr, sg = [w_in[:, o[i]:o[i + 1]] for i in range(8)]
    zl = jnp.zeros((d, MLA_NOPE), w_in.dtype)
    zr = jnp.zeros((d, MLA_HEAD_PAD - MLA_QK), w_in.dtype)
    krp = jnp.concatenate([zl, kr, zr], axis=1)
    krpr = jnp.concatenate([zl, _pair_rot(kr), zr], axis=1)
    w_aug = jnp.concatenate([a, q, _pair_rot(q), k, _pair_rot(k), v, cq, ckv, krp, krpr, sg], axis=1).astype(BF16)

    wq = mla_w_uq.reshape(MLA_Q_RANK, MLA_HEADS, MLA_QK)
    zq = jnp.zeros((MLA_Q_RANK, MLA_HEADS, MLA_HEAD_PAD - MLA_QK), wq.dtype)
    wq1 = jnp.concatenate([wq, zq], axis=-1).reshape(MLA_Q_RANK, MLA_HEADS * MLA_HEAD_PAD).astype(BF16)
    rope_rot = _pair_rot(wq[..., MLA_NOPE:].reshape(MLA_Q_RANK, MLA_HEADS * MLA_ROPE)).reshape(MLA_Q_RANK, MLA_HEADS, MLA_ROPE)
    wq2 = jnp.concatenate([jnp.zeros((MLA_Q_RANK, MLA_HEADS, MLA_NOPE), wq.dtype), rope_rot, zq], axis=-1)
    wq2 = wq2.reshape(MLA_Q_RANK, MLA_HEADS * MLA_HEAD_PAD).astype(BF16)

    wkv = mla_w_ukv.reshape(MLA_KV_RANK, MLA_HEADS, MLA_NOPE + MLA_V)
    zk = jnp.zeros((MLA_KV_RANK, MLA_HEADS, MLA_HEAD_PAD - MLA_NOPE), wkv.dtype)
    wk = jnp.concatenate([wkv[..., :MLA_NOPE], zk], axis=-1).reshape(MLA_KV_RANK, MLA_HEADS * MLA_HEAD_PAD).astype(BF16)
    zv = jnp.zeros((MLA_KV_RANK, MLA_HEADS, MLA_HEAD_PAD - MLA_V), wkv.dtype)
    wv = jnp.concatenate([wkv[..., MLA_NOPE:], zv], axis=-1).reshape(MLA_KV_RANK, MLA_HEADS * MLA_HEAD_PAD).astype(BF16)
    return w_aug, wq1, wq2, wk, wv


def _token_tile(per_batch, pref):
    return min(pref, per_batch)


def kernel(x, c, ctx, c_ctx, w_mod, b_mod, g_pre_mix, g_post_mix, g_pre_ffn, g_post_ffn, w_in, w_out, w_fourier, win_sink, mla_g_q, mla_g_kv, mla_w_uq, mla_w_ukv, sgu_w, sgu_b, ffn_w_gate, ffn_w_up, ffn_w_down, moe_w_router, moe_w_gate, moe_w_up, moe_w_down):
    b, n, d = x.shape
    lc = ctx.shape[1]
    depth = w_mod.shape[0]
    tx, tcx = b * n, b * lc
    tm_x = _token_tile(n, 512)
    tm_c = _token_tile(lc, 512)
    tab_x = _latent_tables(n)
    tab_c = _context_tables(lc)
    cond = jnp.concatenate([c, c_ctx[None, :], jnp.zeros((8 - (b + 1) % 8 if (b + 1) % 8 else 0, d), F32)], axis=0)

    hx = x.reshape(tx, d)
    hc = ctx.reshape(tcx, d)
    for layer in range(depth):
        last = layer == depth - 1
        mod = _modulation(cond, w_mod[layer], b_mod[layer])
        mx6 = mod[:b].reshape(b, 1, 6, d)
        mc6 = mod[b:b + 1].reshape(1, 1, 6, d)
        sh_a, sc_a, gt_a, sh_f, sc_f, gt_f = [mx6[:, :, i] for i in range(6)]
        csh_a, csc_a, cgt_a, csh_f, csc_f, cgt_f = [mc6[:, :, i] for i in range(6)]

        w_aug, wq1, wq2, wk, wv = _prep_layer_weights(w_in[layer], mla_w_uq[layer], mla_w_ukv[layer])
        w_f = w_fourier[layer].astype(BF16)
        w_o = w_out[layer].astype(BF16)
        w_s = sgu_w[layer].astype(BF16)
        bias_full = jnp.repeat(sgu_b[layer].T, SGU_CH, axis=1)
        sink = win_sink[layer]
        gq, gkv = mla_g_q[layer], mla_g_kv[layer]

        px = _inproj(hx, g_pre_mix[layer], sh_a, sc_a, w_aug, tab_x, gq, gkv, wq1, wq2, wk, wv, n, tm_x)
        pc = _inproj(hc, g_pre_mix[layer], csh_a, csc_a, w_aug, tab_c, gq, gkv, wq1, wq2, wk, wv, lc, tm_c)
        a_x, qw_x, kw_x, vw_x, qm_x, km_x, vm_x, u_x, vn_x = [t.reshape(b, n, -1) for t in px]
        a_c, qw_c, kw_c, vw_c, qm_c, km_c, vm_c, u_c, vn_c = [t.reshape(b, lc, -1) for t in pc]

        four_x = _fourier_latent(a_x, w_f)
        win_x = _window_latent(qw_x, kw_x, vw_x, kw_c, vw_c, sink)
        mla_x = _mla_attention(qm_x, km_x, vm_x, km_c, vm_c)
        sgu_x = _sgu(u_x, vn_x, w_s, bias_full)
        parts_x = [t.reshape(tx, GROUP_W) for t in (four_x, win_x, mla_x, sgu_x)]

        i = layer // 2
        dense = layer % 2 == 0
        w_router = None
        if not dense:
            w_r = jnp.concatenate([moe_w_router[i], jnp.zeros((d, LANES - N_EXPERTS), F32)], axis=1)
            w_r_hi = w_r.astype(BF16)
            w_router = jnp.stack([w_r_hi, (w_r - w_r_hi.astype(F32)).astype(BF16)])
        res = _outproj(parts_x, w_o, hx, gt_a, g_post_mix[layer], g_pre_ffn[layer], sh_f, sc_f, n, tm_x, w_router)
        if dense:
            wg, wu, wd = ffn_w_gate[i].astype(BF16), ffn_w_up[i].astype(BF16), ffn_w_down[i].astype(BF16)
            hx = _ffn_dense(res[1], wg, wu, wd, res[0], gt_f, g_post_ffn[layer], n, min(1024, n), 512)
        else:
            wg, wu, wd = moe_w_gate[i].astype(BF16), moe_w_up[i].astype(BF16), moe_w_down[i].astype(BF16)
            hx = _ffn_moe(res[1], res[2], wg, wu, wd, res[0], gt_f, g_post_ffn[layer], n, 512)

        if not last:
            four_c = _fourier_context(a_c, w_f)
            win_c = _window_context(qw_c, kw_c, vw_c, sink)
            mla_c = _mla_attention_small(qm_c, km_c, vm_c)
            sgu_c = _sgu(u_c, vn_c, w_s, bias_full)
            parts_c = [t.reshape(tcx, GROUP_W) for t in (four_c, win_c, mla_c, sgu_c)]
            res_c = _outproj(parts_c, w_o, hc, cgt_a, g_post_mix[layer], g_pre_ffn[layer], csh_f, csc_f, lc, tm_c, w_router)
            if dense:
                hc = _ffn_dense(res_c[1], wg, wu, wd, res_c[0], cgt_f, g_post_ffn[layer], lc, tm_c, 512)
            else:
                hc = _ffn_moe(res_c[1], res_c[2], wg, wu, wd, res_c[0], cgt_f, g_post_ffn[layer], lc, 512)
    return hx.reshape(b, n, d)
```

```python
import functools

import numpy as np
import jax
import jax.numpy as jnp
from jax import lax
from jax.experimental import pallas as pl
from jax.experimental.pallas import tpu as pltpu

F32 = jnp.float32
BF16 = jnp.bfloat16
HIGHEST = lax.Precision.HIGHEST

GRID_W = 64
GROUP_W = 256
FNET_GROUPS = 4
FNET_CH = 64
WIN_HEADS = 4
WIN_KV_HEADS = 2
WIN_HEAD_DIM = 64
WINDOW = 128
BLOCK = 128
MLA_HEADS = 4
MLA_NOPE = 64
MLA_ROPE = 32
MLA_QK = MLA_NOPE + MLA_ROPE
MLA_V = 64
MLA_Q_RANK = 256
MLA_KV_RANK = 128
SGU_GROUPS = 4
SGU_CHUNK = 128
SGU_CH = 64
N_EXPERTS = 8
ROPE_THETA = 10000.0
EPS = 1e-6
NEG_INF = -1e30
LOG2E = 1.4426950408889634

LANES = 128
MLA_HEAD_PAD = 128
FFT_N1 = 128
MOE_TILE = 1024
VMEM_LIMIT = 56 * 1024 * 1024

_O_A, _O_Q, _O_QR, _O_K, _O_KR, _O_V, _O_CQ, _O_CKV, _O_KRP, _O_KRPR, _O_SG, _O_END = (
    0, 256, 512, 768, 896, 1024, 1152, 1408, 1536, 1664, 1792, 2304)


def _cparams(*sem):
    return pltpu.CompilerParams(dimension_semantics=sem, vmem_limit_bytes=VMEM_LIMIT)


def _full(shape):
    nd = len(shape)
    return pl.BlockSpec(shape, lambda *_: (0,) * nd)


def _dot(a, b):
    return jnp.dot(a, b, preferred_element_type=F32)


def _dot_nt(a, b):
    return lax.dot_general(a, b, (((1,), (1,)), ((), ())), preferred_element_type=F32)


def _dot2(hi, lo, x):
    return _dot(hi, x) + _dot(lo, x)


def _split_hi_lo(m):
    m = np.asarray(m, np.float32)
    hi = jnp.asarray(m, F32).astype(BF16)
    lo = (jnp.asarray(m, F32) - hi.astype(F32)).astype(BF16)
    return hi, lo


def _rms(x, g):
    return x * lax.rsqrt(jnp.mean(x * x, axis=-1, keepdims=True) + EPS) * g


def _mod_kernel(c_ref, w_ref, b_ref, o_ref):
    c = c_ref[...]
    s = c / (1.0 + jnp.exp(-c))
    o_ref[...] = jnp.dot(s, w_ref[...], preferred_element_type=F32, precision=HIGHEST) + b_ref[...]


def _modulation(cond, w_mod, b_mod):
    r, d = cond.shape
    n6 = w_mod.shape[1]
    tn = 512
    return pl.pallas_call(
        _mod_kernel,
        grid=(n6 // tn,),
        in_specs=[_full((r, d)), pl.BlockSpec((d, tn), lambda j: (0, j)), pl.BlockSpec((1, tn), lambda j: (0, j))],
        out_specs=pl.BlockSpec((r, tn), lambda j: (0, j)),
        out_shape=jax.ShapeDtypeStruct((r, n6), F32),
        compiler_params=_cparams("arbitrary"),
        name="adaln_mod",
    )(cond, w_mod, b_mod.reshape(1, n6))


def _inproj_kernel(h_ref, g_ref, sh_ref, sc_ref, w_ref, cw_ref, sw_ref, cm_ref, sm_ref,
                   gq_ref, gkv_ref, wq1_ref, wq2_ref, wk_ref, wv_ref,
                   a_o, qw_o, kw_o, vw_o, qm_o, km_o, vm_o, u_o, vn_o):
    x = h_ref[...]
    ax = (_rms(x, g_ref[...]) * (1.0 + sc_ref[0]) + sh_ref[0]).astype(BF16)

    def proj(lo, hi):
        return _dot(ax, w_ref[:, lo:hi])

    a_o[...] = proj(_O_A, _O_Q).astype(BF16)
    cw = cw_ref[...]
    sw = sw_ref[...]
    qw = (proj(_O_Q, _O_QR) * cw + proj(_O_QR, _O_K) * sw) * (WIN_HEAD_DIM ** -0.5)
    qw_o[...] = qw.astype(BF16)
    nk = WIN_KV_HEADS * WIN_HEAD_DIM
    kw_o[...] = (proj(_O_K, _O_KR) * cw[:, :nk] + proj(_O_KR, _O_V) * sw[:, :nk]).astype(BF16)
    vw_o[...] = proj(_O_V, _O_CQ).astype(BF16)

    cm = cm_ref[...]
    sm = sm_ref[...]
    cqn = _rms(proj(_O_CQ, _O_CKV), gq_ref[...]).astype(BF16)
    y1 = _dot(cqn, wq1_ref[...])
    y2 = _dot(cqn, wq2_ref[...])
    scale = (MLA_QK ** -0.5) * LOG2E
    for h in range(MLA_HEADS):
        sl = slice(h * MLA_HEAD_PAD, (h + 1) * MLA_HEAD_PAD)
        qm_o[:, sl] = ((y1[:, sl] * cm + y2[:, sl] * sm) * scale).astype(BF16)

    ckvn = _rms(proj(_O_CKV, _O_KRP), gkv_ref[...]).astype(BF16)
    krp = proj(_O_KRP, _O_KRPR) * cm + proj(_O_KRPR, _O_SG) * sm
    kk = _dot(ckvn, wk_ref[...])
    for h in range(MLA_HEADS):
        sl = slice(h * MLA_HEAD_PAD, (h + 1) * MLA_HEAD_PAD)
        km_o[:, sl] = (kk[:, sl] + krp).astype(BF16)
    vv = _dot(ckvn, wv_ref[...])
    lane = lax.broadcasted_iota(jnp.int32, vv.shape, 1) % MLA_HEAD_PAD
    vm_o[...] = jnp.where(lane == MLA_V, 1.0, vv).astype(BF16)

    sg = proj(_O_SG, _O_END)
    gl = 0.5 * sg * (1.0 + lax.erf(sg * (2.0 ** -0.5)))
    u_o[...] = gl[:, :GROUP_W].astype(BF16)
    v = gl[:, GROUP_W:]
    mu = jnp.mean(v, axis=-1, keepdims=True)
    vc = v - mu
    var = jnp.mean(vc * vc, axis=-1, keepdims=True)
    vn_o[...] = (vc * lax.rsqrt(var + EPS)).astype(BF16)


def _inproj(h, g, shift, scale, w_aug, tables, gq, gkv, wq1, wq2, wk, wv, per_batch, tm):
    t, d = h.shape
    nmod = shift.shape[0]
    tpb = per_batch // tm
    cw, sw, cm, sm = tables

    def mod_idx(i):
        return ((i // tpb) if nmod > 1 else 0, 0, 0)

    tok = lambda w: pl.BlockSpec((tm, w), lambda i: (i, 0))
    tab = lambda w: pl.BlockSpec((tm, w), lambda i: (i % tpb, 0))
    in_specs = [tok(d), _full((1, d)), pl.BlockSpec((1, 1, d), mod_idx), pl.BlockSpec((1, 1, d), mod_idx),
                _full(w_aug.shape), tab(256), tab(256), tab(LANES), tab(LANES),
                _full((1, MLA_Q_RANK)), _full((1, MLA_KV_RANK)),
                _full(wq1.shape), _full(wq2.shape), _full(wk.shape), _full(wv.shape)]
    widths = (256, 256, 128, 128, 512, 512, 512, 256, 256)
    return pl.pallas_call(
        _inproj_kernel,
        grid=(t // tm,),
        in_specs=in_specs,
        out_specs=[tok(w) for w in widths],
        out_shape=[jax.ShapeDtypeStruct((t, w), BF16) for w in widths],
        compiler_params=_cparams("parallel"),
        name="inproj",
    )(h, g.reshape(1, d), shift, scale, w_aug, cw, sw, cm, sm, gq.reshape(1, -1), gkv.reshape(1, -1), wq1, wq2, wk, wv)


def _fft1_kernel(x_ref, mh_ref, ml_ref, tr_ref, ti_ref, or_ref, oi_ref):
    x = x_ref[0]
    z = _dot2(mh_ref[...], ml_ref[...], x)
    ar, ai = z[:FFT_N1], z[FFT_N1:]
    tr, ti = tr_ref[...], ti_ref[...]
    or_ref[0] = (ar * tr - ai * ti).astype(BF16)
    oi_ref[0] = (ar * ti + ai * tr).astype(BF16)


def _fft2_kernel(r_ref, i_ref, cs_ref, wch_ref, wcl_ref, wf_ref, o_ref, g_scr, *, nj, n2):
    cs = cs_ref[...]
    gw = o_ref.shape[-1]
    for j in range(nj):
        zr = _dot(cs, r_ref[0, j])
        zi = _dot(cs, i_ref[0, j])
        c_r, s_r = zr[:n2] + zr[n2:2 * n2], zr[2 * n2:3 * n2] + zr[3 * n2:]
        c_i, s_i = zi[:n2] + zi[n2:2 * n2], zi[2 * n2:3 * n2] + zi[3 * n2:]
        g_scr[j * n2:(j + 1) * n2, :gw] = (c_r + s_i).astype(BF16)
        g_scr[j * n2:(j + 1) * n2, gw:] = (c_i - s_r).astype(BF16)
    g = g_scr[...]
    y = _dot(g, wch_ref[...]) + _dot(g, wcl_ref[...])
    o = _dot(y.astype(BF16), wf_ref[...]).astype(BF16)
    for j in range(nj):
        o_ref[0, j] = o[j * n2:(j + 1) * n2]


def _dft_small_kernel(a_ref, ch_ref, cl_ref, sh_ref, sl_ref, bch_ref, bcl_ref, bsh_ref, bsl_ref, wf_ref, o_ref):
    a = a_ref[0]
    gr = _dot2(ch_ref[...], cl_ref[...], a).astype(BF16)
    gi = (-_dot2(sh_ref[...], sl_ref[...], a)).astype(BF16)
    y = _dot(gr, bch_ref[...]) + _dot(gr, bcl_ref[...]) + _dot(gi, bsh_ref[...]) + _dot(gi, bsl_ref[...])
    o_ref[0] = _dot(y.astype(BF16), wf_ref[...]).astype(BF16)


def _cos_sin(n):
    k = np.arange(n)
    ang = 2.0 * np.pi * ((k[:, None] * k[None, :]) % n) / n
    return np.cos(ang), np.sin(ang)


def _channel_dft(norm):
    c, s = _cos_sin(FNET_CH)
    bc = np.kron(np.eye(FNET_GROUPS), c) * norm
    bs = np.kron(np.eye(FNET_GROUPS), s) * norm
    return _split_hi_lo(bc) + _split_hi_lo(bs)


def _fourier_latent(a, w_f):
    b, n, gw = a.shape
    n2 = n // FFT_N1
    c1, s1 = _cos_sin(FFT_N1)
    mh, ml = _split_hi_lo(np.concatenate([c1, -s1], axis=0))
    k1 = np.arange(FFT_N1)[:, None]
    t2 = np.arange(n2)[None, :]
    ang = 2.0 * np.pi * ((k1 * t2) % n) / n
    tr = jnp.repeat(jnp.asarray(np.cos(ang), F32), gw, axis=1)
    ti = jnp.repeat(jnp.asarray(-np.sin(ang), F32), gw, axis=1)
    cols = n2 * gw
    tc = min(cols, 4096)
    x2 = a.reshape(b, FFT_N1, cols)
    blk = pl.BlockSpec((1, FFT_N1, tc), lambda bi, ci: (bi, 0, ci))
    tblk = pl.BlockSpec((FFT_N1, tc), lambda bi, ci: (0, ci))
    o_r, o_i = pl.pallas_call(
        _fft1_kernel,
        grid=(b, cols // tc),
        in_specs=[blk, _full(mh.shape), _full(ml.shape), tblk, tblk],
        out_specs=[blk, blk],
        out_shape=[jax.ShapeDtypeStruct((b, FFT_N1, cols), BF16)] * 2,
        compiler_params=_cparams("parallel", "parallel"),
        name="fft_stage1",
    )(x2, mh, ml, tr, ti)

    c2, s2 = _cos_sin(n2)
    cs = jnp.concatenate(_split_hi_lo(c2) + _split_hi_lo(s2), axis=0)
    bch, bcl, bsh, bsl = _channel_dft((n * FNET_CH) ** -0.5)
    wch = jnp.concatenate([bch, bsh], axis=0)
    wcl = jnp.concatenate([bcl, bsl], axis=0)
    nj = 8
    r4 = o_r.reshape(b, FFT_N1, n2, gw)
    i4 = o_i.reshape(b, FFT_N1, n2, gw)
    blk4 = pl.BlockSpec((1, nj, n2, gw), lambda bi, ki: (bi, ki, 0, 0))
    out = pl.pallas_call(
        functools.partial(_fft2_kernel, nj=nj, n2=n2),
        grid=(b, FFT_N1 // nj),
        in_specs=[blk4, blk4, _full(cs.shape), _full(wch.shape), _full(wcl.shape), _full(w_f.shape)],
        out_specs=blk4,
        out_shape=jax.ShapeDtypeStruct((b, FFT_N1, n2, gw), BF16),
        scratch_shapes=[pltpu.VMEM((nj * n2, 2 * gw), BF16)],
        compiler_params=_cparams("parallel", "parallel"),
        name="fft_stage2",
    )(r4, i4, cs, wch, wcl, w_f)
    return jnp.swapaxes(out, 1, 2).reshape(b, n, gw)


def _fourier_context(a, w_f):
    b, n, gw = a.shape
    c, s = _cos_sin(n)
    consts = _split_hi_lo(c) + _split_hi_lo(s) + _channel_dft((n * FNET_CH) ** -0.5)
    blk = pl.BlockSpec((1, n, gw), lambda bi: (bi, 0, 0))
    return pl.pallas_call(
        _dft_small_kernel,
        grid=(b,),
        in_specs=[blk] + [_full(m.shape) for m in consts] + [_full(w_f.shape)],
        out_specs=blk,
        out_shape=jax.ShapeDtypeStruct((b, n, gw), BF16),
        compiler_params=_cparams("parallel"),
        name="dft_context",
    )(a, *consts, w_f)


def _win_heads(q_blk, k_win, v_win, valid, kc, vc, sink_ref):
    rows = lax.broadcasted_iota(jnp.int32, (WIN_KV_HEADS * BLOCK, 1), 0)
    outs = [None] * WIN_HEADS
    d = WIN_HEAD_DIM
    for hk in range(WIN_KV_HEADS):
        h0 = 2 * hk
        q2 = jnp.concatenate([q_blk[:, h0 * d:(h0 + 1) * d], q_blk[:, (h0 + 1) * d:(h0 + 2) * d]], axis=0)
        ksl = slice(hk * d, (hk + 1) * d)
        sink = jnp.where(rows < BLOCK, sink_ref[h0], sink_ref[h0 + 1])
        s_c = _dot_nt(q2, kc[:, ksl])
        m = jnp.maximum(jnp.max(s_c, axis=-1, keepdims=True), sink)
        if k_win is not None:
            s_w = jnp.where(valid, _dot_nt(q2, k_win[:, ksl]), NEG_INF)
            m = jnp.maximum(m, jnp.max(s_w, axis=-1, keepdims=True))
        p_c = jnp.exp(s_c - m)
        den = jnp.sum(p_c, axis=-1, keepdims=True) + jnp.exp(sink - m)
        o = _dot(p_c.astype(BF16), vc[:, ksl])
        if k_win is not None:
            p_w = jnp.exp(s_w - m)
            den = den + jnp.sum(p_w, axis=-1, keepdims=True)
            o = o + _dot(p_w.astype(BF16), v_win[:, ksl])
        o = o / den
        outs[h0] = o[:BLOCK]
        outs[h0 + 1] = o[BLOCK:]
    return jnp.concatenate(outs, axis=1)


def _win_kernel(sink_ref, q_ref, kp_ref, k_ref, kn_ref, vp_ref, v_ref, vn_ref, kc_ref, vc_ref, o_ref, *, nblk, n):
    i = pl.program_id(1)
    kfull = jnp.concatenate([kp_ref[0], k_ref[0], kn_ref[0]], axis=0)
    vfull = jnp.concatenate([vp_ref[0], v_ref[0], vn_ref[0]], axis=0)
    kc = kc_ref[0]
    vc = vc_ref[0]
    shape = (WIN_KV_HEADS * BLOCK, 3 * BLOCK)
    qi = lax.broadcasted_iota(jnp.int32, shape, 0) % BLOCK
    kj = lax.broadcasted_iota(jnp.int32, shape, 1)
    band = jnp.abs(kj - BLOCK - qi) <= WINDOW
    for j in range(nblk):
        kpos = (i * nblk + j - 1) * BLOCK + kj
        valid = band & (kpos >= 0) & (kpos < n)
        o = _win_heads(q_ref[0, j * BLOCK:(j + 1) * BLOCK, :], kfull[j * BLOCK:(j + 3) * BLOCK],
                       vfull[j * BLOCK:(j + 3) * BLOCK], valid, kc, vc, sink_ref)
        o_ref[0, j * BLOCK:(j + 1) * BLOCK, :] = o.astype(BF16)


def _winctx_kernel(sink_ref, q_ref, kc_ref, vc_ref, o_ref, *, nblk):
    kc = kc_ref[0]
    vc = vc_ref[0]
    for j in range(nblk):
        o = _win_heads(q_ref[0, j * BLOCK:(j + 1) * BLOCK, :], None, None, None, kc, vc, sink_ref)
        o_ref[0, j * BLOCK:(j + 1) * BLOCK, :] = o.astype(BF16)


def _window_latent(q, k, v, kc, vc, sink):
    b, n, _ = q.shape
    lc = kc.shape[1]
    nblk = min(8, n // BLOCK)
    tq = nblk * BLOCK
    nb = n // BLOCK
    kvw = WIN_KV_HEADS * WIN_HEAD_DIM
    cur = lambda w: pl.BlockSpec((1, tq, w), lambda bi, i: (bi, i, 0))
    prev = pl.BlockSpec((1, BLOCK, kvw), lambda bi, i: (bi, jnp.maximum(i * nblk - 1, 0), 0))
    nxt = pl.BlockSpec((1, BLOCK, kvw), lambda bi, i: (bi, jnp.minimum((i + 1) * nblk, nb - 1), 0))
    ctx = pl.BlockSpec((1, lc, kvw), lambda bi, i: (bi, 0, 0))
    smem = pl.BlockSpec(memory_space=pltpu.SMEM)
    return pl.pallas_call(
        functools.partial(_win_kernel, nblk=nblk, n=n),
        grid=(b, n // tq),
        in_specs=[smem, cur(256), prev, cur(kvw), nxt, prev, cur(kvw), nxt, ctx, ctx],
        out_specs=cur(256),
        out_shape=jax.ShapeDtypeStruct((b, n, 256), BF16),
        compiler_params=_cparams("parallel", "parallel"),
        name="window_attn",
    )(sink, q, k, k, k, v, v, v, kc, vc)


def _window_context(q, kc, vc, sink):
    b, lc, _ = q.shape
    kvw = WIN_KV_HEADS * WIN_HEAD_DIM
    blk = lambda w: pl.BlockSpec((1, lc, w), lambda bi: (bi, 0, 0))
    smem = pl.BlockSpec(memory_space=pltpu.SMEM)
    return pl.pallas_call(
        functools.partial(_winctx_kernel, nblk=lc // BLOCK),
        grid=(b,),
        in_specs=[smem, blk(256), blk(kvw), blk(kvw)],
        out_specs=blk(256),
        out_shape=jax.ShapeDtypeStruct((b, lc, 256), BF16),
        compiler_params=_cparams("parallel"),
        name="window_attn_ctx",
    )(sink, q, kc, vc)


def _mla_reduce(state, s, v):
    m, acc = state
    m_new = jnp.maximum(m, jnp.max(s, axis=-1, keepdims=True))
    alpha = jnp.exp2(m - m_new)
    p = jnp.exp2(s - m_new).astype(BF16)
    return m_new, alpha * acc + _dot(p, v)


def _mla_finish(acc):
    return (acc[:, :MLA_V] / acc[:, MLA_V:MLA_V + 1]).astype(BF16)


def _mla_kernel(q_ref, k_ref, v_ref, kc_ref, vc_ref, o_ref, s_scr, sc_scr, *, tk, nk):
    tq = q_ref.shape[1]
    hsl = [slice(h * MLA_HEAD_PAD, (h + 1) * MLA_HEAD_PAD) for h in range(2)]
    qs = [q_ref[0, :, sl] for sl in hsl]

    def scores(c, slot):
        off = pl.multiple_of(c * tk, tk)
        for h in range(2):
            s_scr[slot, h] = _dot_nt(qs[h], k_ref[0, pl.ds(off, tk), hsl[h]])

    def reduce(c, slot, states):
        off = pl.multiple_of(c * tk, tk)
        return tuple(_mla_reduce(states[h], s_scr[slot, h], v_ref[0, pl.ds(off, tk), hsl[h]]) for h in range(2))

    scores(0, 0)

    def body(i, states):
        scores(2 * i + 1, 1)
        states = reduce(2 * i, 0, states)
        scores(2 * i + 2, 0)
        return reduce(2 * i + 1, 1, states)

    init = tuple((jnp.full((tq, 1), NEG_INF, F32), jnp.zeros((tq, MLA_HEAD_PAD), F32)) for _ in range(2))
    states = lax.fori_loop(0, nk // 2 - 1, body, init)
    scores(nk - 1, 1)
    states = reduce(nk - 2, 0, states)
    for h in range(2):
        sc_scr[h] = _dot_nt(qs[h], kc_ref[0, :, hsl[h]])
    states = reduce(nk - 1, 1, states)
    for h in range(2):
        _, acc = _mla_reduce(states[h], sc_scr[h], vc_ref[0, :, hsl[h]])
        o_ref[0, :, h * MLA_V:(h + 1) * MLA_V] = _mla_finish(acc)


def _mla_small_kernel(q_ref, k_ref, v_ref, o_ref):
    tq = q_ref.shape[1]
    for h in range(2):
        sl = slice(h * MLA_HEAD_PAD, (h + 1) * MLA_HEAD_PAD)
        init = (jnp.full((tq, 1), NEG_INF, F32), jnp.zeros((tq, MLA_HEAD_PAD), F32))
        _, acc = _mla_reduce(init, _dot_nt(q_ref[0, :, sl], k_ref[0, :, sl]), v_ref[0, :, sl])
        o_ref[0, :, h * MLA_V:(h + 1) * MLA_V] = _mla_finish(acc)


def _mla_attention(q, k, v, kc, vc):
    b, nq, _ = q.shape
    nkeys = k.shape[1]
    lc = kc.shape[1]
    tq = min(512, nq)
    tk = min(1024, nkeys // 2)
    assert nkeys % (2 * tk) == 0 and nq % tq == 0
    w2 = 2 * MLA_HEAD_PAD
    res = lambda n: pl.BlockSpec((1, n, w2), lambda bi, h, i: (bi, 0, h))
    return pl.pallas_call(
        functools.partial(_mla_kernel, tk=tk, nk=nkeys // tk),
        grid=(b, MLA_HEADS // 2, nq // tq),
        in_specs=[pl.BlockSpec((1, tq, w2), lambda bi, h, i: (bi, i, h)), res(nkeys), res(nkeys), res(lc), res(lc)],
        out_specs=pl.BlockSpec((1, tq, 2 * MLA_V), lambda bi, h, i: (bi, i, h)),
        out_shape=jax.ShapeDtypeStruct((b, nq, MLA_HEADS * MLA_V), BF16),
        scratch_shapes=[pltpu.VMEM((2, 2, tq, tk), F32), pltpu.VMEM((2, tq, lc), F32)],
        compiler_params=_cparams("parallel", "parallel", "arbitrary"),
        name="mla_attn",
    )(q, k, v, kc, vc)


def _mla_attention_small(q, k, v):
    b, n, _ = q.shape
    w2 = 2 * MLA_HEAD_PAD
    blk = pl.BlockSpec((1, n, w2), lambda bi, h: (bi, 0, h))
    return pl.pallas_call(
        _mla_small_kernel,
        grid=(b, MLA_HEADS // 2),
        in_specs=[blk, blk, blk],
        out_specs=pl.BlockSpec((1, n, 2 * MLA_V), lambda bi, h: (bi, 0, h)),
        out_shape=jax.ShapeDtypeStruct((b, n, MLA_HEADS * MLA_V), BF16),
        compiler_params=_cparams("parallel", "parallel"),
        name="mla_attn_ctx",
    )(q, k, v)


def _sgu_kernel(u_ref, v_ref, w_ref, b_ref, o_ref, *, nchunk):
    lane = lax.broadcasted_iota(jnp.int32, (SGU_CHUNK, GROUP_W), 1) // SGU_CH
    bias = b_ref[...]
    for c in range(nchunk):
        sl = slice(c * SGU_CHUNK, (c + 1) * SGU_CHUNK)
        v = v_ref[0, sl, :]
        s = bias
        for g in range(SGU_GROUPS):
            s = s + _dot(w_ref[g], jnp.where(lane == g, v, jnp.zeros_like(v)))
        o_ref[0, sl, :] = (u_ref[0, sl, :].astype(F32) * s).astype(BF16)


def _sgu(u, vn, w_s, bias_full):
    b, n, gw = u.shape
    nchunk = min(8, n // SGU_CHUNK)
    tn = nchunk * SGU_CHUNK
    blk = pl.BlockSpec((1, tn, gw), lambda bi, i: (bi, i, 0))
    return pl.pallas_call(
        functools.partial(_sgu_kernel, nchunk=nchunk),
        grid=(b, n // tn),
        in_specs=[blk, blk, _full(w_s.shape), _full(bias_full.shape)],
        out_specs=blk,
        out_shape=jax.ShapeDtypeStruct((b, n, gw), BF16),
        compiler_params=_cparams("parallel", "parallel"),
        name="sgu",
    )(u, vn, w_s, bias_full)


def _outproj_kernel(*refs, with_router):
    if with_router:
        (p0, p1, p2, p3, w_ref, h_ref, gate_ref, gpost_ref, gpre_ref, sh_ref, sc_ref, wr_ref,
         hn_o, fx_o, comb_o) = refs
    else:
        p0, p1, p2, p3, w_ref, h_ref, gate_ref, gpost_ref, gpre_ref, sh_ref, sc_ref, hn_o, fx_o = refs
    mx = None
    for idx, p in enumerate((p0, p1, p2, p3)):
        t = _dot(p[...], w_ref[idx * GROUP_W:(idx + 1) * GROUP_W, :])
        mx = t if mx is None else mx + t
    hn = h_ref[...] + gate_ref[0] * _rms(mx, gpost_ref[...])
    hn_o[...] = hn
    fx = _rms(hn, gpre_ref[...]) * (1.0 + sc_ref[0]) + sh_ref[0]
    fx_o[...] = fx.astype(fx_o.dtype)
    if with_router:
        fx_hi = fx.astype(BF16)
        fx_lo = (fx - fx_hi.astype(F32)).astype(BF16)
        logits = _dot(fx_hi, wr_ref[0]) + (_dot(fx_lo, wr_ref[0]) + _dot(fx_hi, wr_ref[1]))
        lane = lax.broadcasted_iota(jnp.int32, logits.shape, 1).astype(F32)
        lg = jnp.where(lane < N_EXPERTS, logits, -jnp.inf)
        m1 = jnp.max(lg, axis=-1, keepdims=True)
        i1 = jnp.min(jnp.where(lg == m1, lane, float(LANES)), axis=-1, keepdims=True)
        lg2 = jnp.where(lane == i1, -jnp.inf, lg)
        m2 = jnp.max(lg2, axis=-1, keepdims=True)
        i2 = jnp.min(jnp.where(lg2 == m2, lane, float(LANES)), axis=-1, keepdims=True)
        e2 = jnp.exp(m2 - m1)
        den = 1.0 + e2
        flag = (lane == i1 + N_EXPERTS) | (lane == i2 + N_EXPERTS)
        comb_o[...] = (jnp.where(lane == i1, 1.0 / den, 0.0) + jnp.where(lane == i2, e2 / den, 0.0)
                       + jnp.where(flag, 1.0, 0.0))


def _outproj(parts, w_out, h, gate, g_post, g_pre, shift, scale, per_batch, tm, w_router=None):
    t, d = h.shape
    nmod = gate.shape[0]
    tpb = per_batch // tm
    with_router = w_router is not None

    def mod_idx(i):
        return ((i // tpb) if nmod > 1 else 0, 0, 0)

    tok = lambda w: pl.BlockSpec((tm, w), lambda i: (i, 0))
    mod = pl.BlockSpec((1, 1, d), mod_idx)
    in_specs = [tok(GROUP_W)] * 4 + [_full(w_out.shape), tok(d), mod, _full((1, d)), _full((1, d)), mod, mod]
    args = list(parts) + [w_out, h, gate, g_post.reshape(1, d), g_pre.reshape(1, d), shift, scale]
    out_specs = [tok(d), tok(d)]
    out_shape = [jax.ShapeDtypeStruct((t, d), F32), jax.ShapeDtypeStruct((t, d), F32 if with_router else BF16)]
    if with_router:
        in_specs.append(_full(w_router.shape))
        args.append(w_router)
        out_specs.append(tok(LANES))
        out_shape.append(jax.ShapeDtypeStruct((t, LANES), F32))
    return pl.pallas_call(
        functools.partial(_outproj_kernel, with_router=with_router),
        grid=(t // tm,),
        in_specs=in_specs,
        out_specs=out_specs,
        out_shape=out_shape,
        compiler_params=_cparams("parallel"),
        name="outproj",
    )(*args)


def _ffn_kernel(x_ref, wg_ref, wu_ref, wd_ref, h_ref, gate_ref, gpost_ref, o_ref, acc_ref):
    f = pl.program_id(1)

    @pl.when(f == 0)
    def _():
        acc_ref[...] = jnp.zeros_like(acc_ref)

    x = x_ref[...]
    g = _dot(x, wg_ref[...])
    u = _dot(x, wu_ref[...])
    hid = (g / (1.0 + jnp.exp(-g)) * u).astype(BF16)
    acc_ref[...] += _dot(hid, wd_ref[...])

    @pl.when(f == pl.num_programs(1) - 1)
    def _():
        o_ref[...] = h_ref[...] + gate_ref[0] * _rms(acc_ref[...], gpost_ref[...])


def _ffn_dense(fx, wg, wu, wd, h, gate, g_post, per_batch, tm, tf):
    t, d = h.shape
    dff = wg.shape[1]
    nmod = gate.shape[0]
    tpb = per_batch // tm

    def mod_idx(i, f):
        return ((i // tpb) if nmod > 1 else 0, 0, 0)

    tok = pl.BlockSpec((tm, d), lambda i, f: (i, 0))
    return pl.pallas_call(
        _ffn_kernel,
        grid=(t // tm, dff // tf),
        in_specs=[tok, pl.BlockSpec((d, tf), lambda i, f: (0, f)), pl.BlockSpec((d, tf), lambda i, f: (0, f)),
                  pl.BlockSpec((tf, d), lambda i, f: (f, 0)), tok, pl.BlockSpec((1, 1, d), mod_idx),
                  pl.BlockSpec((1, d), lambda i, f: (0, 0))],
        out_specs=tok,
        out_shape=jax.ShapeDtypeStruct((t, d), F32),
        scratch_shapes=[pltpu.VMEM((tm, d), F32)],
        compiler_params=_cparams("parallel", "arbitrary"),
        name="ffn_dense",
    )(fx, wg, wu, wd, h, gate, g_post.reshape(1, d))


def _route(comb, tile):
    t = comb.shape[0]
    sel = (comb[:, N_EXPERTS:2 * N_EXPERTS] > 0.5).astype(jnp.int32)
    rank = jnp.cumsum(sel, axis=0) - sel
    padded = (jnp.sum(sel, axis=0) + tile - 1) // tile * tile
    ends = jnp.cumsum(padded)
    pos_e = (ends - padded)[None, :] + rank
    e_id = jnp.arange(N_EXPERTS, dtype=jnp.int32)[None, :]
    e_lo = jnp.min(jnp.where(sel > 0, e_id, N_EXPERTS - 1), axis=1)
    e_hi = jnp.max(sel * e_id, axis=1)
    take = lambda a, e: jnp.take_along_axis(a, e[:, None], axis=1)[:, 0]
    pos = jnp.stack([take(pos_e, e_lo), take(pos_e, e_hi)], axis=1).reshape(2 * t).astype(jnp.int32)
    w = comb[:, :N_EXPERTS]
    wts = jnp.pad(jnp.stack([take(w, e_lo), take(w, e_hi)], axis=1), ((0, 0), (0, LANES - 2)))
    ntiles = -(-2 * t // tile) + N_EXPERTS
    nact = (ends[-1] // tile).astype(jnp.int32)
    tile_id = jnp.arange(ntiles, dtype=jnp.int32)
    te = jnp.minimum(jnp.searchsorted(ends, tile_id * tile, side="right"), N_EXPERTS - 1).astype(jnp.int32)
    te = jnp.where(tile_id < nact, te, te[nact - 1])
    return pos, wts, te, nact.reshape(1), ntiles


def _dispatch_kernel(pos_ref, x_ref, xs_in_ref, xs_ref, sem, *, tm):
    del xs_in_ref
    base = pl.program_id(0) * tm

    def row_copy(t, p):
        return pltpu.make_async_copy(x_ref.at[pl.ds(t, 1)], xs_ref.at[pl.ds(p, 1)], sem)

    def issue(t, carry):
        row_copy(t, pos_ref[2 * (base + t)]).start()
        row_copy(t, pos_ref[2 * (base + t) + 1]).start()
        return carry

    def drain(t, carry):
        row_copy(0, 0).wait()
        row_copy(0, 0).wait()
        return carry

    lax.fori_loop(0, tm, issue, 0, unroll=8)
    lax.fori_loop(0, tm, drain, 0, unroll=8)


def _gmm_kernel(te_ref, nact_ref, x_ref, wg_ref, wu_ref, wd_ref, o_ref, xb_ref, acc_ref):
    del te_ref
    f = pl.program_id(1)
    active = pl.program_id(0) < nact_ref[0]

    @pl.when(jnp.logical_not(active) & (f == 0))
    def _():
        o_ref[...] = jnp.zeros_like(o_ref)

    @pl.when(active)
    def _():
        @pl.when(f == 0)
        def _():
            xb_ref[...] = x_ref[...].astype(BF16)
            acc_ref[...] = jnp.zeros_like(acc_ref)

        x = xb_ref[...]
        g = _dot(x, wg_ref[0])
        u = _dot(x, wu_ref[0])
        hid = (g / (1.0 + jnp.exp(-g)) * u).astype(BF16)
        acc_ref[...] += _dot(hid, wd_ref[0])

        @pl.when(f == pl.num_programs(1) - 1)
        def _():
            o_ref[...] = acc_ref[...]


def _combine_kernel(pos_ref, ys_ref, w_ref, h_ref, gate_ref, gpost_ref, o_ref, buf, sem, *, tm):
    base = pl.program_id(0) * tm

    def row_copy(choice, t, p):
        return pltpu.make_async_copy(ys_ref.at[pl.ds(p, 1)], buf.at[choice, pl.ds(t, 1)], sem)

    def issue(t, carry):
        row_copy(0, t, pos_ref[2 * (base + t)]).start()
        row_copy(1, t, pos_ref[2 * (base + t) + 1]).start()
        return carry

    def drain(t, carry):
        row_copy(0, 0, 0).wait()
        row_copy(1, 0, 0).wait()
        return carry

    lax.fori_loop(0, tm, issue, 0, unroll=8)
    lax.fori_loop(0, tm, drain, 0, unroll=8)
    w = w_ref[...]
    y = w[:, 0:1] * buf[0] + w[:, 1:2] * buf[1]
    o_ref[...] = h_ref[...] + gate_ref[0] * _rms(y, gpost_ref[...])


def _ffn_moe(fx, comb, wg, wu, wd, h, gate, g_post, per_batch, tf):
    t, d = h.shape
    dff = wg.shape[2]
    nf = dff // tf
    tile = MOE_TILE
    pos, wts, te, nact, ntiles = _route(comb, tile)
    ts = ntiles * tile
    any_spec = pl.BlockSpec(memory_space=pl.ANY)

    tm_d = min(512, per_batch)
    xs = pl.pallas_call(
        functools.partial(_dispatch_kernel, tm=tm_d),
        grid_spec=pltpu.PrefetchScalarGridSpec(
            num_scalar_prefetch=1, grid=(t // tm_d,),
            in_specs=[pl.BlockSpec((tm_d, d), lambda i, p: (i, 0)), any_spec],
            out_specs=any_spec,
            scratch_shapes=[pltpu.SemaphoreType.DMA(())]),
        out_shape=jax.ShapeDtypeStruct((ts, d), F32),
        input_output_aliases={2: 0},
        compiler_params=_cparams("arbitrary"),
        name="moe_dispatch",
    )(pos, fx, jnp.zeros((ts, d), F32))

    row = lambda i, f, te_r, na_r: (jnp.minimum(i, na_r[0] - 1), 0)
    fidx = lambda i, f, na_r: jnp.where(i < na_r[0], f, nf - 1)
    ys = pl.pallas_call(
        _gmm_kernel,
        grid_spec=pltpu.PrefetchScalarGridSpec(
            num_scalar_prefetch=2, grid=(ntiles, nf),
            in_specs=[pl.BlockSpec((tile, d), row),
                      pl.BlockSpec((1, d, tf), lambda i, f, te_r, na_r: (te_r[i], 0, fidx(i, f, na_r))),
                      pl.BlockSpec((1, d, tf), lambda i, f, te_r, na_r: (te_r[i], 0, fidx(i, f, na_r))),
                      pl.BlockSpec((1, tf, d), lambda i, f, te_r, na_r: (te_r[i], fidx(i, f, na_r), 0))],
            out_specs=pl.BlockSpec((tile, d), lambda i, f, te_r, na_r: (i, 0)),
            scratch_shapes=[pltpu.VMEM((tile, d), BF16), pltpu.VMEM((tile, d), F32)]),
        out_shape=jax.ShapeDtypeStruct((ts, d), F32),
        compiler_params=_cparams("arbitrary", "arbitrary"),
        name="moe_experts",
    )(te, nact, xs, wg, wu, wd)

    tm_c = min(256, per_batch)
    nmod = gate.shape[0]
    tpb = per_batch // tm_c
    tok = lambda w: pl.BlockSpec((tm_c, w), lambda i, p: (i, 0))
    return pl.pallas_call(
        functools.partial(_combine_kernel, tm=tm_c),
        grid_spec=pltpu.PrefetchScalarGridSpec(
            num_scalar_prefetch=1, grid=(t // tm_c,),
            in_specs=[any_spec, tok(LANES), tok(d),
                      pl.BlockSpec((1, 1, d), lambda i, p: ((i // tpb) if nmod > 1 else 0, 0, 0)),
                      pl.BlockSpec((1, d), lambda i, p: (0, 0))],
            out_specs=tok(d),
            scratch_shapes=[pltpu.VMEM((2, tm_c, d), F32), pltpu.SemaphoreType.DMA(())]),
        out_shape=jax.ShapeDtypeStruct((t, d), F32),
        compiler_params=_cparams("arbitrary"),
        name="moe_combine",
    )(pos, ys, wts, h, gate, g_post.reshape(1, d))


def _rope_tables(n, rot_dim):
    n_freq = rot_dim // 4
    inv_freq = ROPE_THETA ** (-jnp.arange(n_freq, dtype=F32) / n_freq)
    rows = n // GRID_W
    row = jnp.repeat(jnp.arange(rows, dtype=F32), GRID_W)
    col = jnp.tile(jnp.arange(GRID_W, dtype=F32), rows)
    ang = jnp.concatenate([row[:, None] * inv_freq, col[:, None] * inv_freq], axis=-1)
    return jnp.repeat(jnp.cos(ang), 2, axis=-1), jnp.repeat(jnp.sin(ang), 2, axis=-1)


def _pair_rot(w):
    k, c = w.shape
    wp = w.reshape(k, c // 2, 2)
    return jnp.stack([-wp[..., 1], wp[..., 0]], axis=-1).reshape(k, c)


def _latent_tables(n):
    cw, sw = _rope_tables(n, WIN_HEAD_DIM)
    cm, sm = _rope_tables(n, MLA_ROPE)
    one = jnp.ones((n, MLA_NOPE), F32)
    zero_n = jnp.zeros((n, MLA_NOPE), F32)
    zero_p = jnp.zeros((n, MLA_HEAD_PAD - MLA_QK), F32)
    return (jnp.tile(cw, (1, WIN_HEADS)), jnp.tile(sw, (1, WIN_HEADS)),
            jnp.concatenate([one, cm, zero_p], axis=1), jnp.concatenate([zero_n, sm, zero_p], axis=1))


def _context_tables(lc):
    one_w = jnp.ones((lc, WIN_HEADS * WIN_HEAD_DIM), F32)
    cm = jnp.concatenate([jnp.ones((lc, MLA_QK), F32), jnp.zeros((lc, MLA_HEAD_PAD - MLA_QK), F32)], axis=1)
    return one_w, jnp.zeros_like(one_w), cm, jnp.zeros_like(cm)


def _prep_layer_weights(w_in, mla_w_uq, mla_w_ukv):
    d = w_in.shape[0]
    o = np.cumsum((0, 256, 256, 128, 128, 256, 128, 32, 512))
    a, q, k, v, cq, ckv, kr, sg = [w_in[:, o[i]:o[i + 1]] for i in range(8)]
    zl = jnp.zeros((d, MLA_NOPE), w_in.dtype)
    zr = jnp.zeros((d, MLA_HEAD_PAD - MLA_QK), w_in.dtype)
    krp = jnp.concatenate([zl, kr, zr], axis=1)
    krpr = jnp.concatenate([zl, _pair_rot(kr), zr], axis=1)
    w_aug = jnp.concatenate([a, q, _pair_rot(q), k, _pair_rot(k), v, cq, ckv, krp, krpr, sg], axis=1).astype(BF16)

    wq = mla_w_uq.reshape(MLA_Q_RANK, MLA_HEADS, MLA_QK)
    zq = jnp.zeros((MLA_Q_RANK, MLA_HEADS, MLA_HEAD_PAD - MLA_QK), wq.dtype)
    wq1 = jnp.concatenate([wq, zq], axis=-1).reshape(MLA_Q_RANK, MLA_HEADS * MLA_HEAD_PAD).astype(BF16)
    rope_rot = _pair_rot(wq[..., MLA_NOPE:].reshape(MLA_Q_RANK, MLA_HEADS * MLA_ROPE)).reshape(MLA_Q_RANK, MLA_HEADS, MLA_ROPE)
    wq2 = jnp.concatenate([jnp.zeros((MLA_Q_RANK, MLA_HEADS, MLA_NOPE), wq.dtype), rope_rot, zq], axis=-1)
    wq2 = wq2.reshape(MLA_Q_RANK, MLA_HEADS * MLA_HEAD_PAD).astype(BF16)

    wkv = mla_w_ukv.reshape(MLA_KV_RANK, MLA_HEADS, MLA_NOPE + MLA_V)
    zk = jnp.zeros((MLA_KV_RANK, MLA_HEADS, MLA_HEAD_PAD - MLA_NOPE), wkv.dtype)
    wk = jnp.concatenate([wkv[..., :MLA_NOPE], zk], axis=-1).reshape(MLA_KV_RANK, MLA_HEADS * MLA_HEAD_PAD).astype(BF16)
    zv = jnp.zeros((MLA_KV_RANK, MLA_HEADS, MLA_HEAD_PAD - MLA_V), wkv.dtype)
    wv = jnp.concatenate([wkv[..., MLA_NOPE:], zv], axis=-1).reshape(MLA_KV_RANK, MLA_HEADS * MLA_HEAD_PAD).astype(BF16)
    return w_aug, wq1, wq2, wk, wv


def _token_tile(per_batch, pref):
    return min(pref, per_batch)


def kernel(x, c, ctx, c_ctx, w_mod, b_mod, g_pre_mix, g_post_mix, g_pre_ffn, g_post_ffn, w_in, w_out, w_fourier, win_sink, mla_g_q, mla_g_kv, mla_w_uq, mla_w_ukv, sgu_w, sgu_b, ffn_w_gate, ffn_w_up, ffn_w_down, moe_w_router, moe_w_gate, moe_w_up, moe_w_down):
    b, n, d = x.shape
    lc = ctx.shape[1]
    depth = w_mod.shape[0]
    tx, tcx = b * n, b * lc
    tm_x = _token_tile(n, 512)
    tm_c = _token_tile(lc, 512)
    tab_x = _latent_tables(n)
    tab_c = _context_tables(lc)
    cond = jnp.concatenate([c, c_ctx[None, :], jnp.zeros((8 - (b + 1) % 8 if (b + 1) % 8 else 0, d), F32)], axis=0)

    hx = x.reshape(tx, d)
    hc = ctx.reshape(tcx, d)
    for layer in range(depth):
        last = layer == depth - 1
        mod = _modulation(cond, w_mod[layer], b_mod[layer])
        mx6 = mod[:b].reshape(b, 1, 6, d)
        mc6 = mod[b:b + 1].reshape(1, 1, 6, d)
        sh_a, sc_a, gt_a, sh_f, sc_f, gt_f = [mx6[:, :, i] for i in range(6)]
        csh_a, csc_a, cgt_a, csh_f, csc_f, cgt_f = [mc6[:, :, i] for i in range(6)]

        w_aug, wq1, wq2, wk, wv = _prep_layer_weights(w_in[layer], mla_w_uq[layer], mla_w_ukv[layer])
        w_f = w_fourier[layer].astype(BF16)
        w_o = w_out[layer].astype(BF16)
        w_s = sgu_w[layer].astype(BF16)
        bias_full = jnp.repeat(sgu_b[layer].T, SGU_CH, axis=1)
        sink = win_sink[layer]
        gq, gkv = mla_g_q[layer], mla_g_kv[layer]

        px = _inproj(hx, g_pre_mix[layer], sh_a, sc_a, w_aug, tab_x, gq, gkv, wq1, wq2, wk, wv, n, tm_x)
        pc = _inproj(hc, g_pre_mix[layer], csh_a, csc_a, w_aug, tab_c, gq, gkv, wq1, wq2, wk, wv, lc, tm_c)
        a_x, qw_x, kw_x, vw_x, qm_x, km_x, vm_x, u_x, vn_x = [t.reshape(b, n, -1) for t in px]
        a_c, qw_c, kw_c, vw_c, qm_c, km_c, vm_c, u_c, vn_c = [t.reshape(b, lc, -1) for t in pc]

        four_x = _fourier_latent(a_x, w_f)
        win_x = _window_latent(qw_x, kw_x, vw_x, kw_c, vw_c, sink)
        mla_x = _mla_attention(qm_x, km_x, vm_x, km_c, vm_c)
        sgu_x = _sgu(u_x, vn_x, w_s, bias_full)
        parts_x = [t.reshape(tx, GROUP_W) for t in (four_x, win_x, mla_x, sgu_x)]

        i = layer // 2
        dense = layer % 2 == 0
        w_router = None
        if not dense:
            w_r = jnp.concatenate([moe_w_router[i], jnp.zeros((d, LANES - N_EXPERTS), F32)], axis=1)
            w_r_hi = w_r.astype(BF16)
            w_router = jnp.stack([w_r_hi, (w_r - w_r_hi.astype(F32)).astype(BF16)])
        res = _outproj(parts_x, w_o, hx, gt_a, g_post_mix[layer], g_pre_ffn[layer], sh_f, sc_f, n, tm_x, w_router)
        if dense:
            wg, wu, wd = ffn_w_gate[i].astype(BF16), ffn_w_up[i].astype(BF16), ffn_w_down[i].astype(BF16)
            hx = _ffn_dense(res[1], wg, wu, wd, res[0], gt_f, g_post_ffn[layer], n, min(1024, n), 512)
        else:
            wg, wu, wd = moe_w_gate[i].astype(BF16), moe_w_up[i].astype(BF16), moe_w_down[i].astype(BF16)
            hx = _ffn_moe(res[1], res[2], wg, wu, wd, res[0], gt_f, g_post_ffn[layer], n, 512)

        if not last:
            four_c = _fourier_context(a_c, w_f)
            win_c = _window_context(qw_c, kw_c, vw_c, sink)
            mla_c = _mla_attention_small(qm_c, km_c, vm_c)
            sgu_c = _sgu(u_c, vn_c, w_s, bias_full)
            parts_c = [t.reshape(tcx, GROUP_W) for t in (four_c, win_c, mla_c, sgu_c)]
            res_c = _outproj(parts_c, w_o, hc, cgt_a, g_post_mix[layer], g_pre_ffn[layer], csh_f, csc_f, lc, tm_c, w_router)
            if dense:
                hc = _ffn_dense(res_c[1], wg, wu, wd, res_c[0], cgt_f, g_post_ffn[layer], lc, tm_c, 512)
            else:
                hc = _ffn_moe(res_c[1], res_c[2], wg, wu, wd, res_c[0], cgt_f, g_post_ffn[layer], lc, 512)
    return hx.reshape(b, n, d)
```

```python
import functools

import numpy as np
import jax
import jax.numpy as jnp
from jax import lax
from jax.experimental import pallas as pl
from jax.experimental.pallas import tpu as pltpu

F32 = jnp.float32
BF16 = jnp.bfloat16
HIGHEST = lax.Precision.HIGHEST

GRID_W = 64
GROUP_W = 256
FNET_GROUPS = 4
FNET_CH = 64
WIN_HEADS = 4
WIN_KV_HEADS = 2
WIN_HEAD_DIM = 64
WINDOW = 128
BLOCK = 128
MLA_HEADS = 4
MLA_NOPE = 64
MLA_ROPE = 32
MLA_QK = MLA_NOPE + MLA_ROPE
MLA_V = 64
MLA_Q_RANK = 256
MLA_KV_RANK = 128
SGU_GROUPS = 4
SGU_CHUNK = 128
SGU_CH = 64
N_EXPERTS = 8
ROPE_THETA = 10000.0
EPS = 1e-6
NEG_INF = -1e30
LOG2E = 1.4426950408889634

LANES = 128
MLA_HEAD_PAD = 128
FFT_N1 = 128
MOE_TILE = 1024
FFN_SUB = 256
VMEM_LIMIT = 56 * 1024 * 1024

_O_A, _O_Q, _O_QR, _O_K, _O_KR, _O_V, _O_CQ, _O_CKV, _O_KRP, _O_KRPR, _O_SG, _O_END = (
    0, 256, 512, 768, 896, 1024, 1152, 1408, 1536, 1664, 1792, 2304)


def _cparams(*sem):
    return pltpu.CompilerParams(dimension_semantics=sem, vmem_limit_bytes=VMEM_LIMIT)


def _full(shape):
    nd = len(shape)
    return pl.BlockSpec(shape, lambda *_: (0,) * nd)


def _dot(a, b):
    return jnp.dot(a, b, preferred_element_type=F32)


def _dot_nt(a, b):
    return lax.dot_general(a, b, (((1,), (1,)), ((), ())), preferred_element_type=F32)


def _dot2(hi, lo, x):
    return _dot(hi, x) + _dot(lo, x)


def _split_hi_lo(m):
    m = np.asarray(m, np.float32)
    hi = jnp.asarray(m, F32).astype(BF16)
    lo = (jnp.asarray(m, F32) - hi.astype(F32)).astype(BF16)
    return hi, lo


def _rms(x, g):
    return x * lax.rsqrt(jnp.mean(x * x, axis=-1, keepdims=True) + EPS) * g


def _mod_kernel(c_ref, w_ref, b_ref, o_ref):
    c = c_ref[...]
    s = c / (1.0 + jnp.exp(-c))
    o_ref[...] = jnp.dot(s, w_ref[...], preferred_element_type=F32, precision=HIGHEST) + b_ref[...]


def _modulation(cond, w_mod, b_mod):
    r, d = cond.shape
    n6 = w_mod.shape[1]
    tn = 512
    return pl.pallas_call(
        _mod_kernel,
        grid=(n6 // tn,),
        in_specs=[_full((r, d)), pl.BlockSpec((d, tn), lambda j: (0, j)), pl.BlockSpec((1, tn), lambda j: (0, j))],
        out_specs=pl.BlockSpec((r, tn), lambda j: (0, j)),
        out_shape=jax.ShapeDtypeStruct((r, n6), F32),
        compiler_params=_cparams("arbitrary"),
        name="adaln_mod",
    )(cond, w_mod, b_mod.reshape(1, n6))


def _inproj_kernel(h_ref, g_ref, sh_ref, sc_ref, w_ref, cw_ref, sw_ref, cm_ref, sm_ref,
                   gq_ref, gkv_ref, wq1_ref, wq2_ref, wk_ref, wv_ref,
                   a_o, qw_o, kw_o, vw_o, qm_o, km_o, vm_o, u_o, vn_o):
    x = h_ref[...]
    ax = (_rms(x, g_ref[...]) * (1.0 + sc_ref[0]) + sh_ref[0]).astype(BF16)

    def proj(lo, hi):
        return _dot(ax, w_ref[:, lo:hi])

    a_o[...] = proj(_O_A, _O_Q).astype(BF16)
    cw = cw_ref[...]
    sw = sw_ref[...]
    qw = (proj(_O_Q, _O_QR) * cw + proj(_O_QR, _O_K) * sw) * (WIN_HEAD_DIM ** -0.5)
    qw_o[...] = qw.astype(BF16)
    nk = WIN_KV_HEADS * WIN_HEAD_DIM
    kw_o[...] = (proj(_O_K, _O_KR) * cw[:, :nk] + proj(_O_KR, _O_V) * sw[:, :nk]).astype(BF16)
    vw_o[...] = proj(_O_V, _O_CQ).astype(BF16)

    cm = cm_ref[...]
    sm = sm_ref[...]
    cqn = _rms(proj(_O_CQ, _O_CKV), gq_ref[...]).astype(BF16)
    y1 = _dot(cqn, wq1_ref[...])
    y2 = _dot(cqn, wq2_ref[...])
    scale = (MLA_QK ** -0.5) * LOG2E
    for h in range(MLA_HEADS):
        sl = slice(h * MLA_HEAD_PAD, (h + 1) * MLA_HEAD_PAD)
        qm_o[:, sl] = ((y1[:, sl] * cm + y2[:, sl] * sm) * scale).astype(BF16)

    ckvn = _rms(proj(_O_CKV, _O_KRP), gkv_ref[...]).astype(BF16)
    krp = proj(_O_KRP, _O_KRPR) * cm + proj(_O_KRPR, _O_SG) * sm
    kk = _dot(ckvn, wk_ref[...])
    for h in range(MLA_HEADS):
        sl = slice(h * MLA_HEAD_PAD, (h + 1) * MLA_HEAD_PAD)
        km_o[:, sl] = (kk[:, sl] + krp).astype(BF16)
    vv = _dot(ckvn, wv_ref[...])
    lane = lax.broadcasted_iota(jnp.int32, vv.shape, 1) % MLA_HEAD_PAD
    vm_o[...] = jnp.where(lane == MLA_V, 1.0, vv).astype(BF16)

    sg = proj(_O_SG, _O_END)
    gl = 0.5 * sg * (1.0 + lax.erf(sg * (2.0 ** -0.5)))
    u_o[...] = gl[:, :GROUP_W].astype(BF16)
    v = gl[:, GROUP_W:]
    mu = jnp.mean(v, axis=-1, keepdims=True)
    vc = v - mu
    var = jnp.mean(vc * vc, axis=-1, keepdims=True)
    vn_o[...] = (vc * lax.rsqrt(var + EPS)).astype(BF16)


def _inproj(h, g, shift, scale, w_aug, tables, gq, gkv, wq1, wq2, wk, wv, per_batch, tm):
    t, d = h.shape
    nmod = shift.shape[0]
    tpb = per_batch // tm
    cw, sw, cm, sm = tables

    def mod_idx(i):
        return ((i // tpb) if nmod > 1 else 0, 0, 0)

    tok = lambda w: pl.BlockSpec((tm, w), lambda i: (i, 0))
    tab = lambda w: pl.BlockSpec((tm, w), lambda i: (i % tpb, 0))
    in_specs = [tok(d), _full((1, d)), pl.BlockSpec((1, 1, d), mod_idx), pl.BlockSpec((1, 1, d), mod_idx),
                _full(w_aug.shape), tab(256), tab(256), tab(LANES), tab(LANES),
                _full((1, MLA_Q_RANK)), _full((1, MLA_KV_RANK)),
                _full(wq1.shape), _full(wq2.shape), _full(wk.shape), _full(wv.shape)]
    widths = (256, 256, 128, 128, 512, 512, 512, 256, 256)
    return pl.pallas_call(
        _inproj_kernel,
        grid=(t // tm,),
        in_specs=in_specs,
        out_specs=[tok(w) for w in widths],
        out_shape=[jax.ShapeDtypeStruct((t, w), BF16) for w in widths],
        compiler_params=_cparams("parallel"),
        name="inproj",
    )(h, g.reshape(1, d), shift, scale, w_aug, cw, sw, cm, sm, gq.reshape(1, -1), gkv.reshape(1, -1), wq1, wq2, wk, wv)


def _fft1_kernel(x_ref, mh_ref, ml_ref, tr_ref, ti_ref, or_ref, oi_ref):
    x = x_ref[0]
    z = _dot2(mh_ref[...], ml_ref[...], x)
    ar, ai = z[:FFT_N1], z[FFT_N1:]
    tr, ti = tr_ref[...], ti_ref[...]
    or_ref[0] = (ar * tr - ai * ti).astype(BF16)
    oi_ref[0] = (ar * ti + ai * tr).astype(BF16)


def _fft2_kernel(r_ref, i_ref, cs_ref, wch_ref, wcl_ref, wf_ref, o_ref, g_scr, *, nj, n2):
    cs = cs_ref[...]
    gw = o_ref.shape[-1]
    for j in range(nj):
        zr = _dot(cs, r_ref[0, j])
        zi = _dot(cs, i_ref[0, j])
        c_r, s_r = zr[:n2] + zr[n2:2 * n2], zr[2 * n2:3 * n2] + zr[3 * n2:]
        c_i, s_i = zi[:n2] + zi[n2:2 * n2], zi[2 * n2:3 * n2] + zi[3 * n2:]
        g_scr[j * n2:(j + 1) * n2, :gw] = (c_r + s_i).astype(BF16)
        g_scr[j * n2:(j + 1) * n2, gw:] = (c_i - s_r).astype(BF16)
    g = g_scr[...]
    y = _dot(g, wch_ref[...]) + _dot(g, wcl_ref[...])
    o = _dot(y.astype(BF16), wf_ref[...]).astype(BF16)
    for j in range(nj):
        o_ref[0, j] = o[j * n2:(j + 1) * n2]


def _dft_small_kernel(a_ref, ch_ref, cl_ref, sh_ref, sl_ref, bch_ref, bcl_ref, bsh_ref, bsl_ref, wf_ref, o_ref):
    a = a_ref[0]
    gr = _dot2(ch_ref[...], cl_ref[...], a).astype(BF16)
    gi = (-_dot2(sh_ref[...], sl_ref[...], a)).astype(BF16)
    y = _dot(gr, bch_ref[...]) + _dot(gr, bcl_ref[...]) + _dot(gi, bsh_ref[...]) + _dot(gi, bsl_ref[...])
    o_ref[0] = _dot(y.astype(BF16), wf_ref[...]).astype(BF16)


def _cos_sin(n):
    k = np.arange(n)
    ang = 2.0 * np.pi * ((k[:, None] * k[None, :]) % n) / n
    return np.cos(ang), np.sin(ang)


def _channel_dft(norm):
    c, s = _cos_sin(FNET_CH)
    bc = np.kron(np.eye(FNET_GROUPS), c) * norm
    bs = np.kron(np.eye(FNET_GROUPS), s) * norm
    return _split_hi_lo(bc) + _split_hi_lo(bs)


def _fourier_latent(a, w_f):
    b, n, gw = a.shape
    n2 = n // FFT_N1
    c1, s1 = _cos_sin(FFT_N1)
    mh, ml = _split_hi_lo(np.concatenate([c1, -s1], axis=0))
    k1 = np.arange(FFT_N1)[:, None]
    t2 = np.arange(n2)[None, :]
    ang = 2.0 * np.pi * ((k1 * t2) % n) / n
    tr = jnp.repeat(jnp.asarray(np.cos(ang), F32), gw, axis=1)
    ti = jnp.repeat(jnp.asarray(-np.sin(ang), F32), gw, axis=1)
    cols = n2 * gw
    tc = min(cols, 4096)
    x2 = a.reshape(b, FFT_N1, cols)
    blk = pl.BlockSpec((1, FFT_N1, tc), lambda bi, ci: (bi, 0, ci))
    tblk = pl.BlockSpec((FFT_N1, tc), lambda bi, ci: (0, ci))
    o_r, o_i = pl.pallas_call(
        _fft1_kernel,
        grid=(b, cols // tc),
        in_specs=[blk, _full(mh.shape), _full(ml.shape), tblk, tblk],
        out_specs=[blk, blk],
        out_shape=[jax.ShapeDtypeStruct((b, FFT_N1, cols), BF16)] * 2,
        compiler_params=_cparams("parallel", "parallel"),
        name="fft_stage1",
    )(x2, mh, ml, tr, ti)

    c2, s2 = _cos_sin(n2)
    cs = jnp.concatenate(_split_hi_lo(c2) + _split_hi_lo(s2), axis=0)
    bch, bcl, bsh, bsl = _channel_dft((n * FNET_CH) ** -0.5)
    wch = jnp.concatenate([bch, bsh], axis=0)
    wcl = jnp.concatenate([bcl, bsl], axis=0)
    nj = 8
    r4 = o_r.reshape(b, FFT_N1, n2, gw)
    i4 = o_i.reshape(b, FFT_N1, n2, gw)
    blk4 = pl.BlockSpec((1, nj, n2, gw), lambda bi, ki: (bi, ki, 0, 0))
    out = pl.pallas_call(
        functools.partial(_fft2_kernel, nj=nj, n2=n2),
        grid=(b, FFT_N1 // nj),
        in_specs=[blk4, blk4, _full(cs.shape), _full(wch.shape), _full(wcl.shape), _full(w_f.shape)],
        out_specs=blk4,
        out_shape=jax.ShapeDtypeStruct((b, FFT_N1, n2, gw), BF16),
        scratch_shapes=[pltpu.VMEM((nj * n2, 2 * gw), BF16)],
        compiler_params=_cparams("parallel", "parallel"),
        name="fft_stage2",
    )(r4, i4, cs, wch, wcl, w_f)
    return jnp.swapaxes(out, 1, 2).reshape(b, n, gw)


def _fourier_context(a, w_f):
    b, n, gw = a.shape
    c, s = _cos_sin(n)
    consts = _split_hi_lo(c) + _split_hi_lo(s) + _channel_dft((n * FNET_CH) ** -0.5)
    blk = pl.BlockSpec((1, n, gw), lambda bi: (bi, 0, 0))
    return pl.pallas_call(
        _dft_small_kernel,
        grid=(b,),
        in_specs=[blk] + [_full(m.shape) for m in consts] + [_full(w_f.shape)],
        out_specs=blk,
        out_shape=jax.ShapeDtypeStruct((b, n, gw), BF16),
        compiler_params=_cparams("parallel"),
        name="dft_context",
    )(a, *consts, w_f)


def _win_heads(q_blk, k_win, v_win, valid, kc, vc, sink_ref):
    rows = lax.broadcasted_iota(jnp.int32, (WIN_KV_HEADS * BLOCK, 1), 0)
    outs = [None] * WIN_HEADS
    d = WIN_HEAD_DIM
    for hk in range(WIN_KV_HEADS):
        h0 = 2 * hk
        q2 = jnp.concatenate([q_blk[:, h0 * d:(h0 + 1) * d], q_blk[:, (h0 + 1) * d:(h0 + 2) * d]], axis=0)
        ksl = slice(hk * d, (hk + 1) * d)
        sink = jnp.where(rows < BLOCK, sink_ref[h0], sink_ref[h0 + 1])
        s_c = _dot_nt(q2, kc[:, ksl])
        m = jnp.maximum(jnp.max(s_c, axis=-1, keepdims=True), sink)
        if k_win is not None:
            s_w = jnp.where(valid, _dot_nt(q2, k_win[:, ksl]), NEG_INF)
            m = jnp.maximum(m, jnp.max(s_w, axis=-1, keepdims=True))
        p_c = jnp.exp(s_c - m)
        den = jnp.sum(p_c, axis=-1, keepdims=True) + jnp.exp(sink - m)
        o = _dot(p_c.astype(BF16), vc[:, ksl])
        if k_win is not None:
            p_w = jnp.exp(s_w - m)
            den = den + jnp.sum(p_w, axis=-1, keepdims=True)
            o = o + _dot(p_w.astype(BF16), v_win[:, ksl])
        o = o / den
        outs[h0] = o[:BLOCK]
        outs[h0 + 1] = o[BLOCK:]
    return jnp.concatenate(outs, axis=1)


def _win_kernel(sink_ref, q_ref, kp_ref, k_ref, kn_ref, vp_ref, v_ref, vn_ref, kc_ref, vc_ref, o_ref, *, nblk, n):
    i = pl.program_id(1)
    kfull = jnp.concatenate([kp_ref[0], k_ref[0], kn_ref[0]], axis=0)
    vfull = jnp.concatenate([vp_ref[0], v_ref[0], vn_ref[0]], axis=0)
    kc = kc_ref[0]
    vc = vc_ref[0]
    shape = (WIN_KV_HEADS * BLOCK, 3 * BLOCK)
    qi = lax.broadcasted_iota(jnp.int32, shape, 0) % BLOCK
    kj = lax.broadcasted_iota(jnp.int32, shape, 1)
    band = jnp.abs(kj - BLOCK - qi) <= WINDOW
    for j in range(nblk):
        kpos = (i * nblk + j - 1) * BLOCK + kj
        valid = band & (kpos >= 0) & (kpos < n)
        o = _win_heads(q_ref[0, j * BLOCK:(j + 1) * BLOCK, :], kfull[j * BLOCK:(j + 3) * BLOCK],
                       vfull[j * BLOCK:(j + 3) * BLOCK], valid, kc, vc, sink_ref)
        o_ref[0, j * BLOCK:(j + 1) * BLOCK, :] = o.astype(BF16)


def _winctx_kernel(sink_ref, q_ref, kc_ref, vc_ref, o_ref, *, nblk):
    kc = kc_ref[0]
    vc = vc_ref[0]
    for j in range(nblk):
        o = _win_heads(q_ref[0, j * BLOCK:(j + 1) * BLOCK, :], None, None, None, kc, vc, sink_ref)
        o_ref[0, j * BLOCK:(j + 1) * BLOCK, :] = o.astype(BF16)


def _window_latent(q, k, v, kc, vc, sink):
    b, n, _ = q.shape
    lc = kc.shape[1]
    nblk = min(8, n // BLOCK)
    tq = nblk * BLOCK
    nb = n // BLOCK
    kvw = WIN_KV_HEADS * WIN_HEAD_DIM
    cur = lambda w: pl.BlockSpec((1, tq, w), lambda bi, i: (bi, i, 0))
    prev = pl.BlockSpec((1, BLOCK, kvw), lambda bi, i: (bi, jnp.maximum(i * nblk - 1, 0), 0))
    nxt = pl.BlockSpec((1, BLOCK, kvw), lambda bi, i: (bi, jnp.minimum((i + 1) * nblk, nb - 1), 0))
    ctx = pl.BlockSpec((1, lc, kvw), lambda bi, i: (bi, 0, 0))
    smem = pl.BlockSpec(memory_space=pltpu.SMEM)
    return pl.pallas_call(
        functools.partial(_win_kernel, nblk=nblk, n=n),
        grid=(b, n // tq),
        in_specs=[smem, cur(256), prev, cur(kvw), nxt, prev, cur(kvw), nxt, ctx, ctx],
        out_specs=cur(256),
        out_shape=jax.ShapeDtypeStruct((b, n, 256), BF16),
        compiler_params=_cparams("parallel", "parallel"),
        name="window_attn",
    )(sink, q, k, k, k, v, v, v, kc, vc)


def _window_context(q, kc, vc, sink):
    b, lc, _ = q.shape
    kvw = WIN_KV_HEADS * WIN_HEAD_DIM
    blk = lambda w: pl.BlockSpec((1, lc, w), lambda bi: (bi, 0, 0))
    smem = pl.BlockSpec(memory_space=pltpu.SMEM)
    return pl.pallas_call(
        functools.partial(_winctx_kernel, nblk=lc // BLOCK),
        grid=(b,),
        in_specs=[smem, blk(256), blk(kvw), blk(kvw)],
        out_specs=blk(256),
        out_shape=jax.ShapeDtypeStruct((b, lc, 256), BF16),
        compiler_params=_cparams("parallel"),
        name="window_attn_ctx",
    )(sink, q, kc, vc)


def _mla_reduce(state, s, v):
    m, acc = state
    m_new = jnp.maximum(m, jnp.max(s, axis=-1, keepdims=True))
    alpha = jnp.exp2(m - m_new)
    p = jnp.exp2(s - m_new).astype(BF16)
    return m_new, alpha * acc + _dot(p, v)


def _mla_finish(acc):
    return (acc[:, :MLA_V] / acc[:, MLA_V:MLA_V + 1]).astype(BF16)


def _mla_kernel(q_ref, k_ref, v_ref, kc_ref, vc_ref, o_ref, s_scr, sc_scr, *, tk, nk):
    tq = q_ref.shape[1]
    hsl = [slice(h * MLA_HEAD_PAD, (h + 1) * MLA_HEAD_PAD) for h in range(2)]
    qs = [q_ref[0, :, sl] for sl in hsl]

    def scores(c, slot):
        off = pl.multiple_of(c * tk, tk)
        for h in range(2):
            s_scr[slot, h] = _dot_nt(qs[h], k_ref[0, pl.ds(off, tk), hsl[h]])

    def reduce(c, slot, states):
        off = pl.multiple_of(c * tk, tk)
        return tuple(_mla_reduce(states[h], s_scr[slot, h], v_ref[0, pl.ds(off, tk), hsl[h]]) for h in range(2))

    scores(0, 0)

    def body(i, states):
        scores(2 * i + 1, 1)
        states = reduce(2 * i, 0, states)
        scores(2 * i + 2, 0)
        return reduce(2 * i + 1, 1, states)

    init = tuple((jnp.full((tq, 1), NEG_INF, F32), jnp.zeros((tq, MLA_HEAD_PAD), F32)) for _ in range(2))
    states = lax.fori_loop(0, nk // 2 - 1, body, init)
    scores(nk - 1, 1)
    states = reduce(nk - 2, 0, states)
    for h in range(2):
        sc_scr[h] = _dot_nt(qs[h], kc_ref[0, :, hsl[h]])
    states = reduce(nk - 1, 1, states)
    for h in range(2):
        _, acc = _mla_reduce(states[h], sc_scr[h], vc_ref[0, :, hsl[h]])
        o_ref[0, :, h * MLA_V:(h + 1) * MLA_V] = _mla_finish(acc)


def _mla_small_kernel(q_ref, k_ref, v_ref, o_ref):
    tq = q_ref.shape[1]
    for h in range(2):
        sl = slice(h * MLA_HEAD_PAD, (h + 1) * MLA_HEAD_PAD)
        init = (jnp.full((tq, 1), NEG_INF, F32), jnp.zeros((tq, MLA_HEAD_PAD), F32))
        _, acc = _mla_reduce(init, _dot_nt(q_ref[0, :, sl], k_ref[0, :, sl]), v_ref[0, :, sl])
        o_ref[0, :, h * MLA_V:(h + 1) * MLA_V] = _mla_finish(acc)


def _mla_attention(q, k, v, kc, vc):
    b, nq, _ = q.shape
    nkeys = k.shape[1]
    lc = kc.shape[1]
    tq = min(512, nq)
    tk = min(1024, nkeys // 2)
    assert nkeys % (2 * tk) == 0 and nq % tq == 0
    w2 = 2 * MLA_HEAD_PAD
    res = lambda n: pl.BlockSpec((1, n, w2), lambda bi, h, i: (bi, 0, h))
    return pl.pallas_call(
        functools.partial(_mla_kernel, tk=tk, nk=nkeys // tk),
        grid=(b, MLA_HEADS // 2, nq // tq),
        in_specs=[pl.BlockSpec((1, tq, w2), lambda bi, h, i: (bi, i, h)), res(nkeys), res(nkeys), res(lc), res(lc)],
        out_specs=pl.BlockSpec((1, tq, 2 * MLA_V), lambda bi, h, i: (bi, i, h)),
        out_shape=jax.ShapeDtypeStruct((b, nq, MLA_HEADS * MLA_V), BF16),
        scratch_shapes=[pltpu.VMEM((2, 2, tq, tk), F32), pltpu.VMEM((2, tq, lc), F32)],
        compiler_params=_cparams("parallel", "parallel", "arbitrary"),
        name="mla_attn",
    )(q, k, v, kc, vc)


def _mla_attention_small(q, k, v):
    b, n, _ = q.shape
    w2 = 2 * MLA_HEAD_PAD
    blk = pl.BlockSpec((1, n, w2), lambda bi, h: (bi, 0, h))
    return pl.pallas_call(
        _mla_small_kernel,
        grid=(b, MLA_HEADS // 2),
        in_specs=[blk, blk, blk],
        out_specs=pl.BlockSpec((1, n, 2 * MLA_V), lambda bi, h: (bi, 0, h)),
        out_shape=jax.ShapeDtypeStruct((b, n, MLA_HEADS * MLA_V), BF16),
        compiler_params=_cparams("parallel", "parallel"),
        name="mla_attn_ctx",
    )(q, k, v)


def _sgu_kernel(u_ref, v_ref, w_ref, b_ref, o_ref, *, nchunk):
    lane = lax.broadcasted_iota(jnp.int32, (SGU_CHUNK, GROUP_W), 1) // SGU_CH
    bias = b_ref[...]
    for c in range(nchunk):
        sl = slice(c * SGU_CHUNK, (c + 1) * SGU_CHUNK)
        v = v_ref[0, sl, :]
        s = bias
        for g in range(SGU_GROUPS):
            s = s + _dot(w_ref[g], jnp.where(lane == g, v, jnp.zeros_like(v)))
        o_ref[0, sl, :] = (u_ref[0, sl, :].astype(F32) * s).astype(BF16)


def _sgu(u, vn, w_s, bias_full):
    b, n, gw = u.shape
    nchunk = min(8, n // SGU_CHUNK)
    tn = nchunk * SGU_CHUNK
    blk = pl.BlockSpec((1, tn, gw), lambda bi, i: (bi, i, 0))
    return pl.pallas_call(
        functools.partial(_sgu_kernel, nchunk=nchunk),
        grid=(b, n // tn),
        in_specs=[blk, blk, _full(w_s.shape), _full(bias_full.shape)],
        out_specs=blk,
        out_shape=jax.ShapeDtypeStruct((b, n, gw), BF16),
        compiler_params=_cparams("parallel", "parallel"),
        name="sgu",
    )(u, vn, w_s, bias_full)


def _outproj_kernel(*refs, with_router):
    if with_router:
        (p0, p1, p2, p3, w_ref, h_ref, gate_ref, gpost_ref, gpre_ref, sh_ref, sc_ref, wr_ref,
         hn_o, fx_o, comb_o) = refs
    else:
        p0, p1, p2, p3, w_ref, h_ref, gate_ref, gpost_ref, gpre_ref, sh_ref, sc_ref, hn_o, fx_o = refs
    mx = None
    for idx, p in enumerate((p0, p1, p2, p3)):
        t = _dot(p[...], w_ref[idx * GROUP_W:(idx + 1) * GROUP_W, :])
        mx = t if mx is None else mx + t
    hn = h_ref[...] + gate_ref[0] * _rms(mx, gpost_ref[...])
    hn_o[...] = hn
    fx = _rms(hn, gpre_ref[...]) * (1.0 + sc_ref[0]) + sh_ref[0]
    fx_o[...] = fx.astype(fx_o.dtype)
    if with_router:
        fx_hi = fx.astype(BF16)
        fx_lo = (fx - fx_hi.astype(F32)).astype(BF16)
        logits = _dot(fx_hi, wr_ref[0]) + (_dot(fx_lo, wr_ref[0]) + _dot(fx_hi, wr_ref[1]))
        lane = lax.broadcasted_iota(jnp.int32, logits.shape, 1).astype(F32)
        lg = jnp.where(lane < N_EXPERTS, logits, -jnp.inf)
        m1 = jnp.max(lg, axis=-1, keepdims=True)
        i1 = jnp.min(jnp.where(lg == m1, lane, float(LANES)), axis=-1, keepdims=True)
        lg2 = jnp.where(lane == i1, -jnp.inf, lg)
        m2 = jnp.max(lg2, axis=-1, keepdims=True)
        i2 = jnp.min(jnp.where(lg2 == m2, lane, float(LANES)), axis=-1, keepdims=True)
        e2 = jnp.exp(m2 - m1)
        den = 1.0 + e2
        flag = (lane == i1 + N_EXPERTS) | (lane == i2 + N_EXPERTS)
        comb_o[...] = (jnp.where(lane == i1, 1.0 / den, 0.0) + jnp.where(lane == i2, e2 / den, 0.0)
                       + jnp.where(flag, 1.0, 0.0))


def _outproj(parts, w_out, h, gate, g_post, g_pre, shift, scale, per_batch, tm, w_router=None):
    t, d = h.shape
    nmod = gate.shape[0]
    tpb = per_batch // tm
    with_router = w_router is not None

    def mod_idx(i):
        return ((i // tpb) if nmod > 1 else 0, 0, 0)

    tok = lambda w: pl.BlockSpec((tm, w), lambda i: (i, 0))
    mod = pl.BlockSpec((1, 1, d), mod_idx)
    in_specs = [tok(GROUP_W)] * 4 + [_full(w_out.shape), tok(d), mod, _full((1, d)), _full((1, d)), mod, mod]
    args = list(parts) + [w_out, h, gate, g_post.reshape(1, d), g_pre.reshape(1, d), shift, scale]
    out_specs = [tok(d), tok(d)]
    out_shape = [jax.ShapeDtypeStruct((t, d), F32), jax.ShapeDtypeStruct((t, d), F32 if with_router else BF16)]
    if with_router:
        in_specs.append(_full(w_router.shape))
        args.append(w_router)
        out_specs.append(tok(LANES))
        out_shape.append(jax.ShapeDtypeStruct((t, LANES), F32))
    return pl.pallas_call(
        functools.partial(_outproj_kernel, with_router=with_router),
        grid=(t // tm,),
        in_specs=in_specs,
        out_specs=out_specs,
        out_shape=out_shape,
        compiler_params=_cparams("parallel"),
        name="outproj",
    )(*args)


def _swiglu_partial(x, wg, wu, wd, tf):
    y = None
    for lo in range(0, tf, FFN_SUB):
        hi = lo + FFN_SUB
        g = _dot(x, wg(lo, hi))
        u = _dot(x, wu(lo, hi))
        hid = (g / (1.0 + jnp.exp(-g)) * u).astype(BF16)
        t = _dot(hid, wd(lo, hi))
        y = t if y is None else y + t
    return y


def _ffn_kernel(x_ref, wg_ref, wu_ref, wd_ref, h_ref, gate_ref, gpost_ref, o_ref):
    f = pl.program_id(1)
    last = pl.num_programs(1) - 1
    y = _swiglu_partial(x_ref[...], lambda lo, hi: wg_ref[:, lo:hi], lambda lo, hi: wu_ref[:, lo:hi],
                        lambda lo, hi: wd_ref[lo:hi, :], wd_ref.shape[0])

    @pl.when(f == 0)
    def _():
        o_ref[...] = y

    @pl.when((f > 0) & (f < last))
    def _():
        o_ref[...] += y

    @pl.when(f == last)
    def _():
        o_ref[...] = h_ref[...] + gate_ref[0] * _rms(o_ref[...] + y, gpost_ref[...])


def _ffn_dense(fx, wg, wu, wd, h, gate, g_post, per_batch, tm, tf):
    t, d = h.shape
    dff = wg.shape[1]
    nmod = gate.shape[0]
    tpb = per_batch // tm
    assert dff // tf >= 2 and tf % FFN_SUB == 0

    def mod_idx(i, f):
        return ((i // tpb) if nmod > 1 else 0, 0, 0)

    tok = pl.BlockSpec((tm, d), lambda i, f: (i, 0))
    return pl.pallas_call(
        _ffn_kernel,
        grid=(t // tm, dff // tf),
        in_specs=[tok, pl.BlockSpec((d, tf), lambda i, f: (0, f)), pl.BlockSpec((d, tf), lambda i, f: (0, f)),
                  pl.BlockSpec((tf, d), lambda i, f: (f, 0)), tok, pl.BlockSpec((1, 1, d), mod_idx),
                  pl.BlockSpec((1, d), lambda i, f: (0, 0))],
        out_specs=tok,
        out_shape=jax.ShapeDtypeStruct((t, d), F32),
        compiler_params=_cparams("parallel", "arbitrary"),
        name="ffn_dense",
    )(fx, wg, wu, wd, h, gate, g_post.reshape(1, d))


def _route(comb, tile):
    t = comb.shape[0]
    sel = (comb[:, N_EXPERTS:2 * N_EXPERTS] > 0.5).astype(jnp.int32)
    rank = jnp.cumsum(sel, axis=0) - sel
    padded = (jnp.sum(sel, axis=0) + tile - 1) // tile * tile
    ends = jnp.cumsum(padded)
    pos_e = (ends - padded)[None, :] + rank
    e_id = jnp.arange(N_EXPERTS, dtype=jnp.int32)[None, :]
    e_lo = jnp.min(jnp.where(sel > 0, e_id, N_EXPERTS - 1), axis=1)
    e_hi = jnp.max(sel * e_id, axis=1)
    take = lambda a, e: jnp.take_along_axis(a, e[:, None], axis=1)[:, 0]
    pos = jnp.stack([take(pos_e, e_lo), take(pos_e, e_hi)], axis=1).reshape(2 * t).astype(jnp.int32)
    w = comb[:, :N_EXPERTS]
    wts = jnp.pad(jnp.stack([take(w, e_lo), take(w, e_hi)], axis=1), ((0, 0), (0, LANES - 2)))
    ntiles = -(-2 * t // tile) + N_EXPERTS
    nact = (ends[-1] // tile).astype(jnp.int32)
    tile_id = jnp.arange(ntiles, dtype=jnp.int32)
    te = jnp.minimum(jnp.searchsorted(ends, tile_id * tile, side="right"), N_EXPERTS - 1).astype(jnp.int32)
    te = jnp.where(tile_id < nact, te, te[nact - 1])
    return pos, wts, te, nact.reshape(1), ntiles


def _dispatch_kernel(pos_ref, x_ref, xs_in_ref, xs_ref, sem, *, tm):
    del xs_in_ref
    base = pl.program_id(0) * tm

    def row_copy(t, p):
        return pltpu.make_async_copy(x_ref.at[pl.ds(t, 1)], xs_ref.at[pl.ds(p, 1)], sem)

    def issue(t, carry):
        row_copy(t, pos_ref[2 * (base + t)]).start()
        row_copy(t, pos_ref[2 * (base + t) + 1]).start()
        return carry

    def drain(t, carry):
        row_copy(0, 0).wait()
        row_copy(0, 0).wait()
        return carry

    lax.fori_loop(0, tm, issue, 0, unroll=8)
    lax.fori_loop(0, tm, drain, 0, unroll=8)


def _gmm_kernel(te_ref, nact_ref, x_ref, wg_ref, wu_ref, wd_ref, o_ref):
    del te_ref
    f = pl.program_id(1)
    active = pl.program_id(0) < nact_ref[0]

    @pl.when(jnp.logical_not(active) & (f == 0))
    def _():
        o_ref[...] = jnp.zeros_like(o_ref)

    @pl.when(active)
    def _():
        y = _swiglu_partial(x_ref[...].astype(BF16), lambda lo, hi: wg_ref[0, :, lo:hi],
                            lambda lo, hi: wu_ref[0, :, lo:hi], lambda lo, hi: wd_ref[0, lo:hi, :], wd_ref.shape[1])

        @pl.when(f == 0)
        def _():
            o_ref[...] = y

        @pl.when(f > 0)
        def _():
            o_ref[...] += y


def _combine_kernel(pos_ref, ys_ref, w_ref, h_ref, gate_ref, gpost_ref, o_ref, buf, sem, *, tm):
    base = pl.program_id(0) * tm

    def row_copy(choice, t, p):
        return pltpu.make_async_copy(ys_ref.at[pl.ds(p, 1)], buf.at[choice, pl.ds(t, 1)], sem)

    def issue(t, carry):
        row_copy(0, t, pos_ref[2 * (base + t)]).start()
        row_copy(1, t, pos_ref[2 * (base + t) + 1]).start()
        return carry

    def drain(t, carry):
        row_copy(0, 0, 0).wait()
        row_copy(1, 0, 0).wait()
        return carry

    lax.fori_loop(0, tm, issue, 0, unroll=8)
    lax.fori_loop(0, tm, drain, 0, unroll=8)
    w = w_ref[...]
    y = w[:, 0:1] * buf[0] + w[:, 1:2] * buf[1]
    o_ref[...] = h_ref[...] + gate_ref[0] * _rms(y, gpost_ref[...])


def _ffn_moe(fx, comb, wg, wu, wd, h, gate, g_post, per_batch, tf):
    t, d = h.shape
    dff = wg.shape[2]
    nf = dff // tf
    tile = MOE_TILE
    pos, wts, te, nact, ntiles = _route(comb, tile)
    ts = ntiles * tile
    any_spec = pl.BlockSpec(memory_space=pl.ANY)

    tm_d = min(512, per_batch)
    xs = pl.pallas_call(
        functools.partial(_dispatch_kernel, tm=tm_d),
        grid_spec=pltpu.PrefetchScalarGridSpec(
            num_scalar_prefetch=1, grid=(t // tm_d,),
            in_specs=[pl.BlockSpec((tm_d, d), lambda i, p: (i, 0)), any_spec],
            out_specs=any_spec,
            scratch_shapes=[pltpu.SemaphoreType.DMA(())]),
        out_shape=jax.ShapeDtypeStruct((ts, d), F32),
        input_output_aliases={2: 0},
        compiler_params=_cparams("arbitrary"),
        name="moe_dispatch",
    )(pos, fx, jnp.zeros((ts, d), F32))

    row = lambda i, f, te_r, na_r: (jnp.minimum(i, na_r[0] - 1), 0)
    fidx = lambda i, f, na_r: jnp.where(i < na_r[0], f, nf - 1)
    ys = pl.pallas_call(
        _gmm_kernel,
        grid_spec=pltpu.PrefetchScalarGridSpec(
            num_scalar_prefetch=2, grid=(ntiles, nf),
            in_specs=[pl.BlockSpec((tile, d), row),
                      pl.BlockSpec((1, d, tf), lambda i, f, te_r, na_r: (te_r[i], 0, fidx(i, f, na_r))),
                      pl.BlockSpec((1, d, tf), lambda i, f, te_r, na_r: (te_r[i], 0, fidx(i, f, na_r))),
                      pl.BlockSpec((1, tf, d), lambda i, f, te_r, na_r: (te_r[i], fidx(i, f, na_r), 0))],
            out_specs=pl.BlockSpec((tile, d), lambda i, f, te_r, na_r: (i, 0))),
        out_shape=jax.ShapeDtypeStruct((ts, d), F32),
        compiler_params=_cparams("arbitrary", "arbitrary"),
        name="moe_experts",
    )(te, nact, xs, wg, wu, wd)

    tm_c = min(256, per_batch)
    nmod = gate.shape[0]
    tpb = per_batch // tm_c
    tok = lambda w: pl.BlockSpec((tm_c, w), lambda i, p: (i, 0))
    return pl.pallas_call(
        functools.partial(_combine_kernel, tm=tm_c),
        grid_spec=pltpu.PrefetchScalarGridSpec(
            num_scalar_prefetch=1, grid=(t // tm_c,),
            in_specs=[any_spec, tok(LANES), tok(d),
                      pl.BlockSpec((1, 1, d), lambda i, p: ((i // tpb) if nmod > 1 else 0, 0, 0)),
                      pl.BlockSpec((1, d), lambda i, p: (0, 0))],
            out_specs=tok(d),
            scratch_shapes=[pltpu.VMEM((2, tm_c, d), F32), pltpu.SemaphoreType.DMA(())]),
        out_shape=jax.ShapeDtypeStruct((t, d), F32),
        compiler_params=_cparams("arbitrary"),
        name="moe_combine",
    )(pos, ys, wts, h, gate, g_post.reshape(1, d))


def _rope_tables(n, rot_dim):
    n_freq = rot_dim // 4
    inv_freq = ROPE_THETA ** (-jnp.arange(n_freq, dtype=F32) / n_freq)
    rows = n // GRID_W
    row = jnp.repeat(jnp.arange(rows, dtype=F32), GRID_W)
    col = jnp.tile(jnp.arange(GRID_W, dtype=F32), rows)
    ang = jnp.concatenate([row[:, None] * inv_freq, col[:, None] * inv_freq], axis=-1)
    return jnp.repeat(jnp.cos(ang), 2, axis=-1), jnp.repeat(jnp.sin(ang), 2, axis=-1)


def _pair_rot(w):
    k, c = w.shape
    wp = w.reshape(k, c // 2, 2)
    return jnp.stack([-wp[..., 1], wp[..., 0]], axis=-1).reshape(k, c)


def _latent_tables(n):
    cw, sw = _rope_tables(n, WIN_HEAD_DIM)
    cm, sm = _rope_tables(n, MLA_ROPE)
    one = jnp.ones((n, MLA_NOPE), F32)
    zero_n = jnp.zeros((n, MLA_NOPE), F32)
    zero_p = jnp.zeros((n, MLA_HEAD_PAD - MLA_QK), F32)
    return (jnp.tile(cw, (1, WIN_HEADS)), jnp.tile(sw, (1, WIN_HEADS)),
            jnp.concatenate([one, cm, zero_p], axis=1), jnp.concatenate([zero_n, sm, zero_p], axis=1))


def _context_tables(lc):
    one_w = jnp.ones((lc, WIN_HEADS * WIN_HEAD_DIM), F32)
    cm = jnp.concatenate([jnp.ones((lc, MLA_QK), F32), jnp.zeros((lc, MLA_HEAD_PAD - MLA_QK), F32)], axis=1)
    return one_w, jnp.zeros_like(one_w), cm, jnp.zeros_like(cm)


def _prep_layer_weights(w_in, mla_w_uq, mla_w_ukv):
    d = w_in.shape[0]
    o = np.cumsum((0, 256, 256, 128, 128, 256, 128, 32, 512))
    a, q, k, v, cq, ckv, kr, sg = [w_in[:, o[i]:o[i + 1]] for i in range(8)]
    zl = jnp.zeros((d, MLA_NOPE), w_in.dtype)
    zr = jnp.zeros((d, MLA_HEAD_PAD - MLA_QK), w_in.dtype)
    krp = jnp.concatenate([zl, kr, zr], axis=1)
    krpr = jnp.concatenate([zl, _pair_rot(kr), zr], axis=1)
    w_aug = jnp.concatenate([a, q, _pair_rot(q), k, _pair_rot(k), v, cq, ckv, krp, krpr, sg], axis=1).astype(BF16)

    wq = mla_w_uq.reshape(MLA_Q_RANK, MLA_HEADS, MLA_QK)
    zq = jnp.zeros((MLA_Q_RANK, MLA_HEADS, MLA_HEAD_PAD - MLA_QK), wq.dtype)
    wq1 = jnp.concatenate([wq, zq], axis=-1).reshape(MLA_Q_RANK, MLA_HEADS * MLA_HEAD_PAD).astype(BF16)
    rope_rot = _pair_rot(wq[..., MLA_NOPE:].reshape(MLA_Q_RANK, MLA_HEADS * MLA_ROPE)).reshape(MLA_Q_RANK, MLA_HEADS, MLA_ROPE)
    wq2 = jnp.concatenate([jnp.zeros((MLA_Q_RANK, MLA_HEADS, MLA_NOPE), wq.dtype), rope_rot, zq], axis=-1)
    wq2 = wq2.reshape(MLA_Q_RANK, MLA_HEADS * MLA_HEAD_PAD).astype(BF16)

    wkv = mla_w_ukv.reshape(MLA_KV_RANK, MLA_HEADS, MLA_NOPE + MLA_V)
    zk = jnp.zeros((MLA_KV_RANK, MLA_HEADS, MLA_HEAD_PAD - MLA_NOPE), wkv.dtype)
    wk = jnp.concatenate([wkv[..., :MLA_NOPE], zk], axis=-1).reshape(MLA_KV_RANK, MLA_HEADS * MLA_HEAD_PAD).astype(BF16)
    zv = jnp.zeros((MLA_KV_RANK, MLA_HEADS, MLA_HEAD_PAD - MLA_V), wkv.dtype)
    wv = jnp.concatenate([wkv[..., MLA_NOPE:], zv], axis=-1).reshape(MLA_KV_RANK, MLA_HEADS * MLA_HEAD_PAD).astype(BF16)
    return w_aug, wq1, wq2, wk, wv


def _token_tile(per_batch, pref):
    return min(pref, per_batch)


def kernel(x, c, ctx, c_ctx, w_mod, b_mod, g_pre_mix, g_post_mix, g_pre_ffn, g_post_ffn, w_in, w_out, w_fourier, win_sink, mla_g_q, mla_g_kv, mla_w_uq, mla_w_ukv, sgu_w, sgu_b, ffn_w_gate, ffn_w_up, ffn_w_down, moe_w_router, moe_w_gate, moe_w_up, moe_w_down):
    b, n, d = x.shape
    lc = ctx.shape[1]
    depth = w_mod.shape[0]
    tx, tcx = b * n, b * lc
    tm_x = _token_tile(n, 512)
    tm_c = _token_tile(lc, 512)
    tab_x = _latent_tables(n)
    tab_c = _context_tables(lc)
    cond = jnp.concatenate([c, c_ctx[None, :], jnp.zeros((8 - (b + 1) % 8 if (b + 1) % 8 else 0, d), F32)], axis=0)

    hx = x.reshape(tx, d)
    hc = ctx.reshape(tcx, d)
    for layer in range(depth):
        last = layer == depth - 1
        mod = _modulation(cond, w_mod[layer], b_mod[layer])
        mx6 = mod[:b].reshape(b, 1, 6, d)
        mc6 = mod[b:b + 1].reshape(1, 1, 6, d)
        sh_a, sc_a, gt_a, sh_f, sc_f, gt_f = [mx6[:, :, i] for i in range(6)]
        csh_a, csc_a, cgt_a, csh_f, csc_f, cgt_f = [mc6[:, :, i] for i in range(6)]

        w_aug, wq1, wq2, wk, wv = _prep_layer_weights(w_in[layer], mla_w_uq[layer], mla_w_ukv[layer])
        w_f = w_fourier[layer].astype(BF16)
        w_o = w_out[layer].astype(BF16)
        w_s = sgu_w[layer].astype(BF16)
        bias_full = jnp.repeat(sgu_b[layer].T, SGU_CH, axis=1)
        sink = win_sink[layer]
        gq, gkv = mla_g_q[layer], mla_g_kv[layer]

        px = _inproj(hx, g_pre_mix[layer], sh_a, sc_a, w_aug, tab_x, gq, gkv, wq1, wq2, wk, wv, n, tm_x)
        pc = _inproj(hc, g_pre_mix[layer], csh_a, csc_a, w_aug, tab_c, gq, gkv, wq1, wq2, wk, wv, lc, tm_c)
        a_x, qw_x, kw_x, vw_x, qm_x, km_x, vm_x, u_x, vn_x = [t.reshape(b, n, -1) for t in px]
        a_c, qw_c, kw_c, vw_c, qm_c, km_c, vm_c, u_c, vn_c = [t.reshape(b, lc, -1) for t in pc]

        four_x = _fourier_latent(a_x, w_f)
        win_x = _window_latent(qw_x, kw_x, vw_x, kw_c, vw_c, sink)
        mla_x = _mla_attention(qm_x, km_x, vm_x, km_c, vm_c)
        sgu_x = _sgu(u_x, vn_x, w_s, bias_full)
        parts_x = [t.reshape(tx, GROUP_W) for t in (four_x, win_x, mla_x, sgu_x)]

        i = layer // 2
        dense = layer % 2 == 0
        dff = ffn_w_gate.shape[-1]
        tf = dff // 2 if (dff // 2) % FFN_SUB == 0 else FFN_SUB
        w_router = None
        if not dense:
            w_r = jnp.concatenate([moe_w_router[i], jnp.zeros((d, LANES - N_EXPERTS), F32)], axis=1)
            w_r_hi = w_r.astype(BF16)
            w_router = jnp.stack([w_r_hi, (w_r - w_r_hi.astype(F32)).astype(BF16)])
        res = _outproj(parts_x, w_o, hx, gt_a, g_post_mix[layer], g_pre_ffn[layer], sh_f, sc_f, n, tm_x, w_router)
        if dense:
            wg, wu, wd = ffn_w_gate[i].astype(BF16), ffn_w_up[i].astype(BF16), ffn_w_down[i].astype(BF16)
            hx = _ffn_dense(res[1], wg, wu, wd, res[0], gt_f, g_post_ffn[layer], n, min(1024, n), tf)
        else:
            wg, wu, wd = moe_w_gate[i].astype(BF16), moe_w_up[i].astype(BF16), moe_w_down[i].astype(BF16)
            hx = _ffn_moe(res[1], res[2], wg, wu, wd, res[0], gt_f, g_post_ffn[layer], n, tf)

        if not last:
            four_c = _fourier_context(a_c, w_f)
            win_c = _window_context(qw_c, kw_c, vw_c, sink)
            mla_c = _mla_attention_small(qm_c, km_c, vm_c)
            sgu_c = _sgu(u_c, vn_c, w_s, bias_full)
            parts_c = [t.reshape(tcx, GROUP_W) for t in (four_c, win_c, mla_c, sgu_c)]
            res_c = _outproj(parts_c, w_o, hc, cgt_a, g_post_mix[layer], g_pre_ffn[layer], csh_f, csc_f, lc, tm_c, w_router)
            if dense:
                hc = _ffn_dense(res_c[1], wg, wu, wd, res_c[0], cgt_f, g_post_ffn[layer], lc, tm_c, tf)
            else:
                hc = _ffn_moe(res_c[1], res_c[2], wg, wu, wd, res_c[0], cgt_f, g_post_ffn[layer], lc, tf)
    return hx.reshape(b, n, d)
```

```python
import functools

import numpy as np
import jax
import jax.numpy as jnp
from jax import lax
from jax.experimental import pallas as pl
from jax.experimental.pallas import tpu as pltpu

F32 = jnp.float32
BF16 = jnp.bfloat16
HIGHEST = lax.Precision.HIGHEST

GRID_W = 64
GROUP_W = 256
FNET_GROUPS = 4
FNET_CH = 64
WIN_HEADS = 4
WIN_KV_HEADS = 2
WIN_HEAD_DIM = 64
WINDOW = 128
BLOCK = 128
MLA_HEADS = 4
MLA_NOPE = 64
MLA_ROPE = 32
MLA_QK = MLA_NOPE + MLA_ROPE
MLA_V = 64
MLA_Q_RANK = 256
MLA_KV_RANK = 128
SGU_GROUPS = 4
SGU_CHUNK = 128
SGU_CH = 64
N_EXPERTS = 8
ROPE_THETA = 10000.0
EPS = 1e-6
NEG_INF = -1e30
LOG2E = 1.4426950408889634

LANES = 128
MLA_HEAD_PAD = 128
FFT_N1 = 128
MOE_TILE = 1024
FFN_SUB = 256
VMEM_LIMIT = 56 * 1024 * 1024

_O_A, _O_Q, _O_K, _O_V, _O_CQ, _O_CKV, _O_KRP, _O_SG, _O_END = (0, 256, 512, 640, 768, 1024, 1152, 1280, 1792)


def _cparams(*sem):
    return pltpu.CompilerParams(dimension_semantics=sem, vmem_limit_bytes=VMEM_LIMIT)


def _full(shape):
    nd = len(shape)
    return pl.BlockSpec(shape, lambda *_: (0,) * nd)


def _dot(a, b):
    return jnp.dot(a, b, preferred_element_type=F32)


def _dot_nt(a, b):
    return lax.dot_general(a, b, (((1,), (1,)), ((), ())), preferred_element_type=F32)


def _dot2(hi, lo, x):
    return _dot(hi, x) + _dot(lo, x)


def _split_hi_lo(m):
    m = np.asarray(m, np.float32)
    hi = jnp.asarray(m, F32).astype(BF16)
    lo = (jnp.asarray(m, F32) - hi.astype(F32)).astype(BF16)
    return hi, lo


def _rms(x, g):
    return x * lax.rsqrt(jnp.mean(x * x, axis=-1, keepdims=True) + EPS) * g


def _mod_kernel(c_ref, w_ref, b_ref, o_ref):
    c = c_ref[...]
    s = c / (1.0 + jnp.exp(-c))
    o_ref[...] = jnp.dot(s, w_ref[...], preferred_element_type=F32, precision=HIGHEST) + b_ref[...]


def _modulation(cond, w_mod, b_mod):
    r, d = cond.shape
    n6 = w_mod.shape[1]
    tn = 512
    return pl.pallas_call(
        _mod_kernel,
        grid=(n6 // tn,),
        in_specs=[_full((r, d)), pl.BlockSpec((d, tn), lambda j: (0, j)), pl.BlockSpec((1, tn), lambda j: (0, j))],
        out_specs=pl.BlockSpec((r, tn), lambda j: (0, j)),
        out_shape=jax.ShapeDtypeStruct((r, n6), F32),
        compiler_params=_cparams("arbitrary"),
        name="adaln_mod",
    )(cond, w_mod, b_mod.reshape(1, n6))


def _inproj_kernel(h_ref, g_ref, sh_ref, sc_ref, w_ref, cw_ref, sw_ref, cm_ref, sm_ref,
                   gq_ref, gkv_ref, wq_ref, wk_ref, wv_ref,
                   a_o, qw_o, kw_o, vw_o, qm_o, km_o, vm_o, u_o, vn_o):
    x = h_ref[...]
    ax = (_rms(x, g_ref[...]) * (1.0 + sc_ref[0]) + sh_ref[0]).astype(BF16)

    def proj(lo, hi):
        return _dot(ax, w_ref[:, lo:hi])

    def rope(z, cos, sin_signed):
        w = z.shape[1]
        even = lax.broadcasted_iota(jnp.int32, z.shape, 1) % 2 == 0
        partner = jnp.where(even, pltpu.roll(z, w - 1, 1), pltpu.roll(z, 1, 1))
        return z * cos + partner * sin_signed

    a_o[...] = proj(_O_A, _O_Q).astype(BF16)
    cw = cw_ref[...]
    sw = sw_ref[...]
    qw_o[...] = (rope(proj(_O_Q, _O_K), cw, sw) * (WIN_HEAD_DIM ** -0.5)).astype(BF16)
    nk = WIN_KV_HEADS * WIN_HEAD_DIM
    kw_o[...] = rope(proj(_O_K, _O_V), cw[:, :nk], sw[:, :nk]).astype(BF16)
    vw_o[...] = proj(_O_V, _O_CQ).astype(BF16)

    cm = cm_ref[...]
    sm = sm_ref[...]
    cqn = _rms(proj(_O_CQ, _O_CKV), gq_ref[...]).astype(BF16)
    y1 = _dot(cqn, wq_ref[...])
    scale = (MLA_QK ** -0.5) * LOG2E
    for h in range(MLA_HEADS):
        sl = slice(h * MLA_HEAD_PAD, (h + 1) * MLA_HEAD_PAD)
        qm_o[:, sl] = (rope(y1[:, sl], cm, sm) * scale).astype(BF16)

    ckvn = _rms(proj(_O_CKV, _O_KRP), gkv_ref[...]).astype(BF16)
    krp = rope(proj(_O_KRP, _O_SG), cm, sm)
    kk = _dot(ckvn, wk_ref[...])
    for h in range(MLA_HEADS):
        sl = slice(h * MLA_HEAD_PAD, (h + 1) * MLA_HEAD_PAD)
        km_o[:, sl] = (kk[:, sl] + krp).astype(BF16)
    vv = _dot(ckvn, wv_ref[...])
    lane = lax.broadcasted_iota(jnp.int32, vv.shape, 1) % MLA_HEAD_PAD
    vm_o[...] = jnp.where(lane == MLA_V, 1.0, vv).astype(BF16)

    sg = proj(_O_SG, _O_END)
    gl = 0.5 * sg * (1.0 + lax.erf(sg * (2.0 ** -0.5)))
    u_o[...] = gl[:, :GROUP_W].astype(BF16)
    v = gl[:, GROUP_W:]
    mu = jnp.mean(v, axis=-1, keepdims=True)
    vc = v - mu
    var = jnp.mean(vc * vc, axis=-1, keepdims=True)
    vn_o[...] = (vc * lax.rsqrt(var + EPS)).astype(BF16)


def _inproj(h, g, shift, scale, w_aug, tables, gq, gkv, wq, wk, wv, per_batch, tm):
    t, d = h.shape
    nmod = shift.shape[0]
    tpb = per_batch // tm
    cw, sw, cm, sm = tables

    def mod_idx(i):
        return ((i // tpb) if nmod > 1 else 0, 0, 0)

    tok = lambda w: pl.BlockSpec((tm, w), lambda i: (i, 0))
    tab = lambda w: pl.BlockSpec((tm, w), lambda i: (i % tpb, 0))
    in_specs = [tok(d), _full((1, d)), pl.BlockSpec((1, 1, d), mod_idx), pl.BlockSpec((1, 1, d), mod_idx),
                _full(w_aug.shape), tab(256), tab(256), tab(LANES), tab(LANES),
                _full((1, MLA_Q_RANK)), _full((1, MLA_KV_RANK)),
                _full(wq.shape), _full(wk.shape), _full(wv.shape)]
    widths = (256, 256, 128, 128, 512, 512, 512, 256, 256)
    return pl.pallas_call(
        _inproj_kernel,
        grid=(t // tm,),
        in_specs=in_specs,
        out_specs=[tok(w) for w in widths],
        out_shape=[jax.ShapeDtypeStruct((t, w), BF16) for w in widths],
        compiler_params=_cparams("parallel"),
        name="inproj",
    )(h, g.reshape(1, d), shift, scale, w_aug, cw, sw, cm, sm, gq.reshape(1, -1), gkv.reshape(1, -1), wq, wk, wv)


def _fft1_kernel(x_ref, mh_ref, ml_ref, tr_ref, ti_ref, or_ref, oi_ref):
    x = x_ref[0]
    z = _dot2(mh_ref[...], ml_ref[...], x)
    ar, ai = z[:FFT_N1], z[FFT_N1:]
    tr, ti = tr_ref[...], ti_ref[...]
    or_ref[0] = (ar * tr - ai * ti).astype(BF16)
    oi_ref[0] = (ar * ti + ai * tr).astype(BF16)


def _fft2_kernel(r_ref, i_ref, cs_ref, wch_ref, wcl_ref, wf_ref, o_ref, g_scr, *, nj, n2):
    cs = cs_ref[...]
    gw = o_ref.shape[-1]
    for j in range(nj):
        zr = _dot(cs, r_ref[0, j])
        zi = _dot(cs, i_ref[0, j])
        c_r, s_r = zr[:n2] + zr[n2:2 * n2], zr[2 * n2:3 * n2] + zr[3 * n2:]
        c_i, s_i = zi[:n2] + zi[n2:2 * n2], zi[2 * n2:3 * n2] + zi[3 * n2:]
        g_scr[j * n2:(j + 1) * n2, :gw] = (c_r + s_i).astype(BF16)
        g_scr[j * n2:(j + 1) * n2, gw:] = (c_i - s_r).astype(BF16)
    g = g_scr[...]
    y = _dot(g, wch_ref[...]) + _dot(g, wcl_ref[...])
    o = _dot(y.astype(BF16), wf_ref[...]).astype(BF16)
    for j in range(nj):
        o_ref[0, j] = o[j * n2:(j + 1) * n2]


def _dft_small_kernel(a_ref, ch_ref, cl_ref, sh_ref, sl_ref, bch_ref, bcl_ref, bsh_ref, bsl_ref, wf_ref, o_ref):
    a = a_ref[0]
    gr = _dot2(ch_ref[...], cl_ref[...], a).astype(BF16)
    gi = (-_dot2(sh_ref[...], sl_ref[...], a)).astype(BF16)
    y = _dot(gr, bch_ref[...]) + _dot(gr, bcl_ref[...]) + _dot(gi, bsh_ref[...]) + _dot(gi, bsl_ref[...])
    o_ref[0] = _dot(y.astype(BF16), wf_ref[...]).astype(BF16)


def _cos_sin(n):
    k = np.arange(n)
    ang = 2.0 * np.pi * ((k[:, None] * k[None, :]) % n) / n
    return np.cos(ang), np.sin(ang)


def _channel_dft(norm):
    c, s = _cos_sin(FNET_CH)
    bc = np.kron(np.eye(FNET_GROUPS), c) * norm
    bs = np.kron(np.eye(FNET_GROUPS), s) * norm
    return _split_hi_lo(bc) + _split_hi_lo(bs)


def _fourier_latent(a, w_f):
    b, n, gw = a.shape
    n2 = n // FFT_N1
    c1, s1 = _cos_sin(FFT_N1)
    mh, ml = _split_hi_lo(np.concatenate([c1, -s1], axis=0))
    k1 = np.arange(FFT_N1)[:, None]
    t2 = np.arange(n2)[None, :]
    ang = 2.0 * np.pi * ((k1 * t2) % n) / n
    tr = jnp.repeat(jnp.asarray(np.cos(ang), F32), gw, axis=1)
    ti = jnp.repeat(jnp.asarray(-np.sin(ang), F32), gw, axis=1)
    cols = n2 * gw
    tc = min(cols, 4096)
    x2 = a.reshape(b, FFT_N1, cols)
    blk = pl.BlockSpec((1, FFT_N1, tc), lambda bi, ci: (bi, 0, ci))
    tblk = pl.BlockSpec((FFT_N1, tc), lambda bi, ci: (0, ci))
    o_r, o_i = pl.pallas_call(
        _fft1_kernel,
        grid=(b, cols // tc),
        in_specs=[blk, _full(mh.shape), _full(ml.shape), tblk, tblk],
        out_specs=[blk, blk],
        out_shape=[jax.ShapeDtypeStruct((b, FFT_N1, cols), BF16)] * 2,
        compiler_params=_cparams("parallel", "parallel"),
        name="fft_stage1",
    )(x2, mh, ml, tr, ti)

    c2, s2 = _cos_sin(n2)
    cs = jnp.concatenate(_split_hi_lo(c2) + _split_hi_lo(s2), axis=0)
    bch, bcl, bsh, bsl = _channel_dft((n * FNET_CH) ** -0.5)
    wch = jnp.concatenate([bch, bsh], axis=0)
    wcl = jnp.concatenate([bcl, bsl], axis=0)
    nj = 8
    r4 = o_r.reshape(b, FFT_N1, n2, gw)
    i4 = o_i.reshape(b, FFT_N1, n2, gw)
    blk4 = pl.BlockSpec((1, nj, n2, gw), lambda bi, ki: (bi, ki, 0, 0))
    out = pl.pallas_call(
        functools.partial(_fft2_kernel, nj=nj, n2=n2),
        grid=(b, FFT_N1 // nj),
        in_specs=[blk4, blk4, _full(cs.shape), _full(wch.shape), _full(wcl.shape), _full(w_f.shape)],
        out_specs=blk4,
        out_shape=jax.ShapeDtypeStruct((b, FFT_N1, n2, gw), BF16),
        scratch_shapes=[pltpu.VMEM((nj * n2, 2 * gw), BF16)],
        compiler_params=_cparams("parallel", "parallel"),
        name="fft_stage2",
    )(r4, i4, cs, wch, wcl, w_f)
    return jnp.swapaxes(out, 1, 2).reshape(b, n, gw)


def _fourier_context(a, w_f):
    b, n, gw = a.shape
    c, s = _cos_sin(n)
    consts = _split_hi_lo(c) + _split_hi_lo(s) + _channel_dft((n * FNET_CH) ** -0.5)
    blk = pl.BlockSpec((1, n, gw), lambda bi: (bi, 0, 0))
    return pl.pallas_call(
        _dft_small_kernel,
        grid=(b,),
        in_specs=[blk] + [_full(m.shape) for m in consts] + [_full(w_f.shape)],
        out_specs=blk,
        out_shape=jax.ShapeDtypeStruct((b, n, gw), BF16),
        compiler_params=_cparams("parallel"),
        name="dft_context",
    )(a, *consts, w_f)


def _win_heads(q_blk, k_win, v_win, valid, kc, vc, sink_ref):
    rows = lax.broadcasted_iota(jnp.int32, (WIN_KV_HEADS * BLOCK, 1), 0)
    outs = [None] * WIN_HEADS
    d = WIN_HEAD_DIM
    for hk in range(WIN_KV_HEADS):
        h0 = 2 * hk
        q2 = jnp.concatenate([q_blk[:, h0 * d:(h0 + 1) * d], q_blk[:, (h0 + 1) * d:(h0 + 2) * d]], axis=0)
        ksl = slice(hk * d, (hk + 1) * d)
        sink = jnp.where(rows < BLOCK, sink_ref[h0], sink_ref[h0 + 1])
        s_c = _dot_nt(q2, kc[:, ksl])
        m = jnp.maximum(jnp.max(s_c, axis=-1, keepdims=True), sink)
        if k_win is not None:
            s_w = jnp.where(valid, _dot_nt(q2, k_win[:, ksl]), NEG_INF)
            m = jnp.maximum(m, jnp.max(s_w, axis=-1, keepdims=True))
        p_c = jnp.exp(s_c - m)
        den = jnp.sum(p_c, axis=-1, keepdims=True) + jnp.exp(sink - m)
        o = _dot(p_c.astype(BF16), vc[:, ksl])
        if k_win is not None:
            p_w = jnp.exp(s_w - m)
            den = den + jnp.sum(p_w, axis=-1, keepdims=True)
            o = o + _dot(p_w.astype(BF16), v_win[:, ksl])
        o = o / den
        outs[h0] = o[:BLOCK]
        outs[h0 + 1] = o[BLOCK:]
    return jnp.concatenate(outs, axis=1)


def _win_kernel(sink_ref, q_ref, kp_ref, k_ref, kn_ref, vp_ref, v_ref, vn_ref, kc_ref, vc_ref, o_ref, *, nblk, n):
    i = pl.program_id(1)
    kfull = jnp.concatenate([kp_ref[0], k_ref[0], kn_ref[0]], axis=0)
    vfull = jnp.concatenate([vp_ref[0], v_ref[0], vn_ref[0]], axis=0)
    kc = kc_ref[0]
    vc = vc_ref[0]
    shape = (WIN_KV_HEADS * BLOCK, 3 * BLOCK)
    qi = lax.broadcasted_iota(jnp.int32, shape, 0) % BLOCK
    kj = lax.broadcasted_iota(jnp.int32, shape, 1)
    band = jnp.abs(kj - BLOCK - qi) <= WINDOW
    for j in range(nblk):
        kpos = (i * nblk + j - 1) * BLOCK + kj
        valid = band & (kpos >= 0) & (kpos < n)
        o = _win_heads(q_ref[0, j * BLOCK:(j + 1) * BLOCK, :], kfull[j * BLOCK:(j + 3) * BLOCK],
                       vfull[j * BLOCK:(j + 3) * BLOCK], valid, kc, vc, sink_ref)
        o_ref[0, j * BLOCK:(j + 1) * BLOCK, :] = o.astype(BF16)


def _winctx_kernel(sink_ref, q_ref, kc_ref, vc_ref, o_ref, *, nblk):
    kc = kc_ref[0]
    vc = vc_ref[0]
    for j in range(nblk):
        o = _win_heads(q_ref[0, j * BLOCK:(j + 1) * BLOCK, :], None, None, None, kc, vc, sink_ref)
        o_ref[0, j * BLOCK:(j + 1) * BLOCK, :] = o.astype(BF16)


def _window_latent(q, k, v, kc, vc, sink):
    b, n, _ = q.shape
    lc = kc.shape[1]
    nblk = min(8, n // BLOCK)
    tq = nblk * BLOCK
    nb = n // BLOCK
    kvw = WIN_KV_HEADS * WIN_HEAD_DIM
    cur = lambda w: pl.BlockSpec((1, tq, w), lambda bi, i: (bi, i, 0))
    prev = pl.BlockSpec((1, BLOCK, kvw), lambda bi, i: (bi, jnp.maximum(i * nblk - 1, 0), 0))
    nxt = pl.BlockSpec((1, BLOCK, kvw), lambda bi, i: (bi, jnp.minimum((i + 1) * nblk, nb - 1), 0))
    ctx = pl.BlockSpec((1, lc, kvw), lambda bi, i: (bi, 0, 0))
    smem = pl.BlockSpec(memory_space=pltpu.SMEM)
    return pl.pallas_call(
        functools.partial(_win_kernel, nblk=nblk, n=n),
        grid=(b, n // tq),
        in_specs=[smem, cur(256), prev, cur(kvw), nxt, prev, cur(kvw), nxt, ctx, ctx],
        out_specs=cur(256),
        out_shape=jax.ShapeDtypeStruct((b, n, 256), BF16),
        compiler_params=_cparams("parallel", "parallel"),
        name="window_attn",
    )(sink, q, k, k, k, v, v, v, kc, vc)


def _window_context(q, kc, vc, sink):
    b, lc, _ = q.shape
    kvw = WIN_KV_HEADS * WIN_HEAD_DIM
    blk = lambda w: pl.BlockSpec((1, lc, w), lambda bi: (bi, 0, 0))
    smem = pl.BlockSpec(memory_space=pltpu.SMEM)
    return pl.pallas_call(
        functools.partial(_winctx_kernel, nblk=lc // BLOCK),
        grid=(b,),
        in_specs=[smem, blk(256), blk(kvw), blk(kvw)],
        out_specs=blk(256),
        out_shape=jax.ShapeDtypeStruct((b, lc, 256), BF16),
        compiler_params=_cparams("parallel"),
        name="window_attn_ctx",
    )(sink, q, kc, vc)


def _mla_reduce(state, s, v):
    m, acc = state
    m_new = jnp.maximum(m, jnp.max(s, axis=-1, keepdims=True))
    alpha = jnp.exp2(m - m_new)
    p = jnp.exp2(s - m_new).astype(BF16)
    return m_new, alpha * acc + _dot(p, v)


def _mla_finish(acc):
    return (acc[:, :MLA_V] / acc[:, MLA_V:MLA_V + 1]).astype(BF16)


def _mla_kernel(q_ref, k_ref, v_ref, kc_ref, vc_ref, o_ref, s_scr, sc_scr, *, tk, nk):
    tq = q_ref.shape[1]
    hsl = [slice(h * MLA_HEAD_PAD, (h + 1) * MLA_HEAD_PAD) for h in range(2)]
    qs = [q_ref[0, :, sl] for sl in hsl]

    def scores(c, slot):
        off = pl.multiple_of(c * tk, tk)
        for h in range(2):
            s_scr[slot, h] = _dot_nt(qs[h], k_ref[0, pl.ds(off, tk), hsl[h]])

    def reduce(c, slot, states):
        off = pl.multiple_of(c * tk, tk)
        return tuple(_mla_reduce(states[h], s_scr[slot, h], v_ref[0, pl.ds(off, tk), hsl[h]]) for h in range(2))

    scores(0, 0)

    def body(i, states):
        scores(2 * i + 1, 1)
        states = reduce(2 * i, 0, states)
        scores(2 * i + 2, 0)
        return reduce(2 * i + 1, 1, states)

    init = tuple((jnp.full((tq, 1), NEG_INF, F32), jnp.zeros((tq, MLA_HEAD_PAD), F32)) for _ in range(2))
    states = lax.fori_loop(0, nk // 2 - 1, body, init)
    scores(nk - 1, 1)
    states = reduce(nk - 2, 0, states)
    for h in range(2):
        sc_scr[h] = _dot_nt(qs[h], kc_ref[0, :, hsl[h]])
    states = reduce(nk - 1, 1, states)
    for h in range(2):
        _, acc = _mla_reduce(states[h], sc_scr[h], vc_ref[0, :, hsl[h]])
        o_ref[0, :, h * MLA_V:(h + 1) * MLA_V] = _mla_finish(acc)


def _mla_small_kernel(q_ref, k_ref, v_ref, o_ref):
    tq = q_ref.shape[1]
    for h in range(2):
        sl = slice(h * MLA_HEAD_PAD, (h + 1) * MLA_HEAD_PAD)
        init = (jnp.full((tq, 1), NEG_INF, F32), jnp.zeros((tq, MLA_HEAD_PAD), F32))
        _, acc = _mla_reduce(init, _dot_nt(q_ref[0, :, sl], k_ref[0, :, sl]), v_ref[0, :, sl])
        o_ref[0, :, h * MLA_V:(h + 1) * MLA_V] = _mla_finish(acc)


def _mla_attention(q, k, v, kc, vc):
    b, nq, _ = q.shape
    nkeys = k.shape[1]
    lc = kc.shape[1]
    tq = min(512, nq)
    tk = min(1024, nkeys // 2)
    assert nkeys % (2 * tk) == 0 and nq % tq == 0
    w2 = 2 * MLA_HEAD_PAD
    res = lambda n: pl.BlockSpec((1, n, w2), lambda bi, h, i: (bi, 0, h))
    return pl.pallas_call(
        functools.partial(_mla_kernel, tk=tk, nk=nkeys // tk),
        grid=(b, MLA_HEADS // 2, nq // tq),
        in_specs=[pl.BlockSpec((1, tq, w2), lambda bi, h, i: (bi, i, h)), res(nkeys), res(nkeys), res(lc), res(lc)],
        out_specs=pl.BlockSpec((1, tq, 2 * MLA_V), lambda bi, h, i: (bi, i, h)),
        out_shape=jax.ShapeDtypeStruct((b, nq, MLA_HEADS * MLA_V), BF16),
        scratch_shapes=[pltpu.VMEM((2, 2, tq, tk), F32), pltpu.VMEM((2, tq, lc), F32)],
        compiler_params=_cparams("parallel", "parallel", "arbitrary"),
        name="mla_attn",
    )(q, k, v, kc, vc)


def _mla_attention_small(q, k, v):
    b, n, _ = q.shape
    w2 = 2 * MLA_HEAD_PAD
    blk = pl.BlockSpec((1, n, w2), lambda bi, h: (bi, 0, h))
    return pl.pallas_call(
        _mla_small_kernel,
        grid=(b, MLA_HEADS // 2),
        in_specs=[blk, blk, blk],
        out_specs=pl.BlockSpec((1, n, 2 * MLA_V), lambda bi, h: (bi, 0, h)),
        out_shape=jax.ShapeDtypeStruct((b, n, MLA_HEADS * MLA_V), BF16),
        compiler_params=_cparams("parallel", "parallel"),
        name="mla_attn_ctx",
    )(q, k, v)


def _sgu_kernel(u_ref, v_ref, w_ref, b_ref, o_ref, *, nchunk):
    lane = lax.broadcasted_iota(jnp.int32, (SGU_CHUNK, GROUP_W), 1) // SGU_CH
    bias = b_ref[...]
    for c in range(nchunk):
        sl = slice(c * SGU_CHUNK, (c + 1) * SGU_CHUNK)
        v = v_ref[0, sl, :]
        s = bias
        for g in range(SGU_GROUPS):
            s = s + _dot(w_ref[g], jnp.where(lane == g, v, jnp.zeros_like(v)))
        o_ref[0, sl, :] = (u_ref[0, sl, :].astype(F32) * s).astype(BF16)


def _sgu(u, vn, w_s, bias_full):
    b, n, gw = u.shape
    nchunk = min(8, n // SGU_CHUNK)
    tn = nchunk * SGU_CHUNK
    blk = pl.BlockSpec((1, tn, gw), lambda bi, i: (bi, i, 0))
    return pl.pallas_call(
        functools.partial(_sgu_kernel, nchunk=nchunk),
        grid=(b, n // tn),
        in_specs=[blk, blk, _full(w_s.shape), _full(bias_full.shape)],
        out_specs=blk,
        out_shape=jax.ShapeDtypeStruct((b, n, gw), BF16),
        compiler_params=_cparams("parallel", "parallel"),
        name="sgu",
    )(u, vn, w_s, bias_full)


def _outproj_kernel(*refs, with_router):
    if with_router:
        (p0, p1, p2, p3, w_ref, h_ref, gate_ref, gpost_ref, gpre_ref, sh_ref, sc_ref, wr_ref,
         hn_o, fx_o, comb_o) = refs
    else:
        p0, p1, p2, p3, w_ref, h_ref, gate_ref, gpost_ref, gpre_ref, sh_ref, sc_ref, hn_o, fx_o = refs
    mx = None
    for idx, p in enumerate((p0, p1, p2, p3)):
        t = _dot(p[...], w_ref[idx * GROUP_W:(idx + 1) * GROUP_W, :])
        mx = t if mx is None else mx + t
    hn = h_ref[...] + gate_ref[0] * _rms(mx, gpost_ref[...])
    hn_o[...] = hn
    fx = _rms(hn, gpre_ref[...]) * (1.0 + sc_ref[0]) + sh_ref[0]
    fx_o[...] = fx.astype(fx_o.dtype)
    if with_router:
        fx_hi = fx.astype(BF16)
        fx_lo = (fx - fx_hi.astype(F32)).astype(BF16)
        logits = _dot(fx_hi, wr_ref[0]) + (_dot(fx_lo, wr_ref[0]) + _dot(fx_hi, wr_ref[1]))
        lane = lax.broadcasted_iota(jnp.int32, logits.shape, 1).astype(F32)
        lg = jnp.where(lane < N_EXPERTS, logits, -jnp.inf)
        m1 = jnp.max(lg, axis=-1, keepdims=True)
        i1 = jnp.min(jnp.where(lg == m1, lane, float(LANES)), axis=-1, keepdims=True)
        lg2 = jnp.where(lane == i1, -jnp.inf, lg)
        m2 = jnp.max(lg2, axis=-1, keepdims=True)
        i2 = jnp.min(jnp.where(lg2 == m2, lane, float(LANES)), axis=-1, keepdims=True)
        e2 = jnp.exp(m2 - m1)
        den = 1.0 + e2
        flag = (lane == i1 + N_EXPERTS) | (lane == i2 + N_EXPERTS)
        comb_o[...] = (jnp.where(lane == i1, 1.0 / den, 0.0) + jnp.where(lane == i2, e2 / den, 0.0)
                       + jnp.where(flag, 1.0, 0.0))


def _outproj(parts, w_out, h, gate, g_post, g_pre, shift, scale, per_batch, tm, w_router=None):
    t, d = h.shape
    nmod = gate.shape[0]
    tpb = per_batch // tm
    with_router = w_router is not None

    def mod_idx(i):
        return ((i // tpb) if nmod > 1 else 0, 0, 0)

    tok = lambda w: pl.BlockSpec((tm, w), lambda i: (i, 0))
    mod = pl.BlockSpec((1, 1, d), mod_idx)
    in_specs = [tok(GROUP_W)] * 4 + [_full(w_out.shape), tok(d), mod, _full((1, d)), _full((1, d)), mod, mod]
    args = list(parts) + [w_out, h, gate, g_post.reshape(1, d), g_pre.reshape(1, d), shift, scale]
    out_specs = [tok(d), tok(d)]
    out_shape = [jax.ShapeDtypeStruct((t, d), F32), jax.ShapeDtypeStruct((t, d), F32 if with_router else BF16)]
    if with_router:
        in_specs.append(_full(w_router.shape))
        args.append(w_router)
        out_specs.append(tok(LANES))
        out_shape.append(jax.ShapeDtypeStruct((t, LANES), F32))
    return pl.pallas_call(
        functools.partial(_outproj_kernel, with_router=with_router),
        grid=(t // tm,),
        in_specs=in_specs,
        out_specs=out_specs,
        out_shape=out_shape,
        compiler_params=_cparams("parallel"),
        name="outproj",
    )(*args)


def _swiglu_partial(x, wg, wu, wd, tf):
    y = None
    for lo in range(0, tf, FFN_SUB):
        hi = lo + FFN_SUB
        g = _dot(x, wg(lo, hi))
        u = _dot(x, wu(lo, hi))
        hid = (g / (1.0 + jnp.exp(-g)) * u).astype(BF16)
        t = _dot(hid, wd(lo, hi))
        y = t if y is None else y + t
    return y


def _ffn_kernel(x_ref, wg_ref, wu_ref, wd_ref, h_ref, gate_ref, gpost_ref, o_ref):
    f = pl.program_id(1)
    last = pl.num_programs(1) - 1
    y = _swiglu_partial(x_ref[...], lambda lo, hi: wg_ref[:, lo:hi], lambda lo, hi: wu_ref[:, lo:hi],
                        lambda lo, hi: wd_ref[lo:hi, :], wd_ref.shape[0])

    @pl.when(f == 0)
    def _():
        o_ref[...] = y

    @pl.when((f > 0) & (f < last))
    def _():
        o_ref[...] += y

    @pl.when(f == last)
    def _():
        o_ref[...] = h_ref[...] + gate_ref[0] * _rms(o_ref[...] + y, gpost_ref[...])


def _ffn_dense(fx, wg, wu, wd, h, gate, g_post, per_batch, tm, tf):
    t, d = h.shape
    dff = wg.shape[1]
    nmod = gate.shape[0]
    tpb = per_batch // tm
    assert dff // tf >= 2 and tf % FFN_SUB == 0

    def mod_idx(i, f):
        return ((i // tpb) if nmod > 1 else 0, 0, 0)

    tok = pl.BlockSpec((tm, d), lambda i, f: (i, 0))
    return pl.pallas_call(
        _ffn_kernel,
        grid=(t // tm, dff // tf),
        in_specs=[tok, pl.BlockSpec((d, tf), lambda i, f: (0, f)), pl.BlockSpec((d, tf), lambda i, f: (0, f)),
                  pl.BlockSpec((tf, d), lambda i, f: (f, 0)), tok, pl.BlockSpec((1, 1, d), mod_idx),
                  pl.BlockSpec((1, d), lambda i, f: (0, 0))],
        out_specs=tok,
        out_shape=jax.ShapeDtypeStruct((t, d), F32),
        compiler_params=_cparams("parallel", "arbitrary"),
        name="ffn_dense",
    )(fx, wg, wu, wd, h, gate, g_post.reshape(1, d))


def _route(comb, tile):
    t = comb.shape[0]
    sel = (comb[:, N_EXPERTS:2 * N_EXPERTS] > 0.5).astype(jnp.int32)
    rank = jnp.cumsum(sel, axis=0) - sel
    padded = (jnp.sum(sel, axis=0) + tile - 1) // tile * tile
    ends = jnp.cumsum(padded)
    pos_e = (ends - padded)[None, :] + rank
    e_id = jnp.arange(N_EXPERTS, dtype=jnp.int32)[None, :]
    e_lo = jnp.min(jnp.where(sel > 0, e_id, N_EXPERTS - 1), axis=1)
    e_hi = jnp.max(sel * e_id, axis=1)
    take = lambda a, e: jnp.take_along_axis(a, e[:, None], axis=1)[:, 0]
    pos = jnp.stack([take(pos_e, e_lo), take(pos_e, e_hi)], axis=1).reshape(2 * t).astype(jnp.int32)
    w = comb[:, :N_EXPERTS]
    wts = jnp.pad(jnp.stack([take(w, e_lo), take(w, e_hi)], axis=1), ((0, 0), (0, LANES - 2)))
    ntiles = -(-2 * t // tile) + N_EXPERTS
    nact = (ends[-1] // tile).astype(jnp.int32)
    tile_id = jnp.arange(ntiles, dtype=jnp.int32)
    te = jnp.minimum(jnp.searchsorted(ends, tile_id * tile, side="right"), N_EXPERTS - 1).astype(jnp.int32)
    te = jnp.where(tile_id < nact, te, te[nact - 1])
    return pos, wts, te, nact.reshape(1), ntiles


def _dispatch_kernel(pos_ref, x_ref, xs_in_ref, xs_ref, sem, *, tm):
    del xs_in_ref
    base = pl.program_id(0) * tm

    def row_copy(t, p):
        return pltpu.make_async_copy(x_ref.at[pl.ds(t, 1)], xs_ref.at[pl.ds(p, 1)], sem)

    def issue(t, carry):
        row_copy(t, pos_ref[2 * (base + t)]).start(priority=0)
        row_copy(t, pos_ref[2 * (base + t) + 1]).start(priority=1)
        return carry

    def drain(t, carry):
        row_copy(0, 0).wait()
        row_copy(0, 0).wait()
        return carry

    lax.fori_loop(0, tm, issue, 0, unroll=8)
    lax.fori_loop(0, tm, drain, 0, unroll=8)


def _gmm_kernel(te_ref, nact_ref, x_ref, wg_ref, wu_ref, wd_ref, o_ref):
    del te_ref
    f = pl.program_id(1)
    active = pl.program_id(0) < nact_ref[0]

    @pl.when(jnp.logical_not(active) & (f == 0))
    def _():
        o_ref[...] = jnp.zeros_like(o_ref)

    @pl.when(active)
    def _():
        y = _swiglu_partial(x_ref[...].astype(BF16), lambda lo, hi: wg_ref[0, :, lo:hi],
                            lambda lo, hi: wu_ref[0, :, lo:hi], lambda lo, hi: wd_ref[0, lo:hi, :], wd_ref.shape[1])

        @pl.when(f == 0)
        def _():
            o_ref[...] = y

        @pl.when(f > 0)
        def _():
            o_ref[...] += y


def _combine_kernel(pos_ref, ys_ref, w_ref, h_ref, gate_ref, gpost_ref, o_ref, buf, sem, *, tm):
    base = pl.program_id(0) * tm

    def row_copy(choice, t, p):
        return pltpu.make_async_copy(ys_ref.at[pl.ds(p, 1)], buf.at[choice, pl.ds(t, 1)], sem)

    def issue(t, carry):
        row_copy(0, t, pos_ref[2 * (base + t)]).start(priority=0)
        row_copy(1, t, pos_ref[2 * (base + t) + 1]).start(priority=1)
        return carry

    def drain(t, carry):
        row_copy(0, 0, 0).wait()
        row_copy(1, 0, 0).wait()
        return carry

    lax.fori_loop(0, tm, issue, 0, unroll=8)
    lax.fori_loop(0, tm, drain, 0, unroll=8)
    w = w_ref[...]
    y = w[:, 0:1] * buf[0] + w[:, 1:2] * buf[1]
    o_ref[...] = h_ref[...] + gate_ref[0] * _rms(y, gpost_ref[...])


def _ffn_moe(fx, comb, wg, wu, wd, h, gate, g_post, per_batch, tf):
    t, d = h.shape
    dff = wg.shape[2]
    nf = dff // tf
    tile = MOE_TILE
    pos, wts, te, nact, ntiles = _route(comb, tile)
    ts = ntiles * tile
    any_spec = pl.BlockSpec(memory_space=pl.ANY)

    tm_d = min(512, per_batch)
    xs = pl.pallas_call(
        functools.partial(_dispatch_kernel, tm=tm_d),
        grid_spec=pltpu.PrefetchScalarGridSpec(
            num_scalar_prefetch=1, grid=(t // tm_d,),
            in_specs=[pl.BlockSpec((tm_d, d), lambda i, p: (i, 0)), any_spec],
            out_specs=any_spec,
            scratch_shapes=[pltpu.SemaphoreType.DMA(())]),
        out_shape=jax.ShapeDtypeStruct((ts, d), F32),
        input_output_aliases={2: 0},
        compiler_params=_cparams("arbitrary"),
        name="moe_dispatch",
    )(pos, fx, jnp.zeros((ts, d), F32))

    row = lambda i, f, te_r, na_r: (jnp.minimum(i, na_r[0] - 1), 0)
    fidx = lambda i, f, na_r: jnp.where(i < na_r[0], f, nf - 1)
    ys = pl.pallas_call(
        _gmm_kernel,
        grid_spec=pltpu.PrefetchScalarGridSpec(
            num_scalar_prefetch=2, grid=(ntiles, nf),
            in_specs=[pl.BlockSpec((tile, d), row),
                      pl.BlockSpec((1, d, tf), lambda i, f, te_r, na_r: (te_r[i], 0, fidx(i, f, na_r))),
                      pl.BlockSpec((1, d, tf), lambda i, f, te_r, na_r: (te_r[i], 0, fidx(i, f, na_r))),
                      pl.BlockSpec((1, tf, d), lambda i, f, te_r, na_r: (te_r[i], fidx(i, f, na_r), 0))],
            out_specs=pl.BlockSpec((tile, d), lambda i, f, te_r, na_r: (i, 0))),
        out_shape=jax.ShapeDtypeStruct((ts, d), F32),
        compiler_params=_cparams("arbitrary", "arbitrary"),
        name="moe_experts",
    )(te, nact, xs, wg, wu, wd)

    tm_c = min(256, per_batch)
    nmod = gate.shape[0]
    tpb = per_batch // tm_c
    tok = lambda w: pl.BlockSpec((tm_c, w), lambda i, p: (i, 0))
    return pl.pallas_call(
        functools.partial(_combine_kernel, tm=tm_c),
        grid_spec=pltpu.PrefetchScalarGridSpec(
            num_scalar_prefetch=1, grid=(t // tm_c,),
            in_specs=[any_spec, tok(LANES), tok(d),
                      pl.BlockSpec((1, 1, d), lambda i, p: ((i // tpb) if nmod > 1 else 0, 0, 0)),
                      pl.BlockSpec((1, d), lambda i, p: (0, 0))],
            out_specs=tok(d),
            scratch_shapes=[pltpu.VMEM((2, tm_c, d), F32), pltpu.SemaphoreType.DMA(())]),
        out_shape=jax.ShapeDtypeStruct((t, d), F32),
        compiler_params=_cparams("arbitrary"),
        name="moe_combine",
    )(pos, ys, wts, h, gate, g_post.reshape(1, d))


def _rope_tables(n, rot_dim):
    n_freq = rot_dim // 4
    inv_freq = ROPE_THETA ** (-jnp.arange(n_freq, dtype=F32) / n_freq)
    rows = n // GRID_W
    row = jnp.repeat(jnp.arange(rows, dtype=F32), GRID_W)
    col = jnp.tile(jnp.arange(GRID_W, dtype=F32), rows)
    ang = jnp.concatenate([row[:, None] * inv_freq, col[:, None] * inv_freq], axis=-1)
    sin = jnp.sin(ang)
    return jnp.repeat(jnp.cos(ang), 2, axis=-1), jnp.stack([-sin, sin], axis=-1).reshape(n, rot_dim)


def _latent_tables(n):
    cw, sw = _rope_tables(n, WIN_HEAD_DIM)
    cm, sm = _rope_tables(n, MLA_ROPE)
    one = jnp.ones((n, MLA_NOPE), F32)
    zero_n = jnp.zeros((n, MLA_NOPE), F32)
    zero_p = jnp.zeros((n, MLA_HEAD_PAD - MLA_QK), F32)
    return (jnp.tile(cw, (1, WIN_HEADS)), jnp.tile(sw, (1, WIN_HEADS)),
            jnp.concatenate([one, cm, zero_p], axis=1), jnp.concatenate([zero_n, sm, zero_p], axis=1))


def _context_tables(lc):
    one_w = jnp.ones((lc, WIN_HEADS * WIN_HEAD_DIM), F32)
    cm = jnp.concatenate([jnp.ones((lc, MLA_QK), F32), jnp.zeros((lc, MLA_HEAD_PAD - MLA_QK), F32)], axis=1)
    return one_w, jnp.zeros_like(one_w), cm, jnp.zeros_like(cm)


def _prep_layer_weights(w_in, mla_w_uq, mla_w_ukv):
    d = w_in.shape[0]
    o = np.cumsum((0, 256, 256, 128, 128, 256, 128, 32, 512))
    a, q, k, v, cq, ckv, kr, sg = [w_in[:, o[i]:o[i + 1]] for i in range(8)]
    zl = jnp.zeros((d, MLA_NOPE), w_in.dtype)
    zr = jnp.zeros((d, MLA_HEAD_PAD - MLA_QK), w_in.dtype)
    krp = jnp.concatenate([zl, kr, zr], axis=1)
    w_aug = jnp.concatenate([a, q, k, v, cq, ckv, krp, sg], axis=1).astype(BF16)

    wq = mla_w_uq.reshape(MLA_Q_RANK, MLA_HEADS, MLA_QK)
    zq = jnp.zeros((MLA_Q_RANK, MLA_HEADS, MLA_HEAD_PAD - MLA_QK), wq.dtype)
    wq = jnp.concatenate([wq, zq], axis=-1).reshape(MLA_Q_RANK, MLA_HEADS * MLA_HEAD_PAD).astype(BF16)

    wkv = mla_w_ukv.reshape(MLA_KV_RANK, MLA_HEADS, MLA_NOPE + MLA_V)
    zk = jnp.zeros((MLA_KV_RANK, MLA_HEADS, MLA_HEAD_PAD - MLA_NOPE), wkv.dtype)
    wk = jnp.concatenate([wkv[..., :MLA_NOPE], zk], axis=-1).reshape(MLA_KV_RANK, MLA_HEADS * MLA_HEAD_PAD).astype(BF16)
    zv = jnp.zeros((MLA_KV_RANK, MLA_HEADS, MLA_HEAD_PAD - MLA_V), wkv.dtype)
    wv = jnp.concatenate([wkv[..., MLA_NOPE:], zv], axis=-1).reshape(MLA_KV_RANK, MLA_HEADS * MLA_HEAD_PAD).astype(BF16)
    return w_aug, wq, wk, wv


def _token_tile(per_batch, pref):
    return min(pref, per_batch)


def kernel(x, c, ctx, c_ctx, w_mod, b_mod, g_pre_mix, g_post_mix, g_pre_ffn, g_post_ffn, w_in, w_out, w_fourier, win_sink, mla_g_q, mla_g_kv, mla_w_uq, mla_w_ukv, sgu_w, sgu_b, ffn_w_gate, ffn_w_up, ffn_w_down, moe_w_router, moe_w_gate, moe_w_up, moe_w_down):
    b, n, d = x.shape
    lc = ctx.shape[1]
    depth = w_mod.shape[0]
    tx, tcx = b * n, b * lc
    tm_x = _token_tile(n, 512)
    tm_c = _token_tile(lc, 512)
    tab_x = _latent_tables(n)
    tab_c = _context_tables(lc)
    cond = jnp.concatenate([c, c_ctx[None, :], jnp.zeros((8 - (b + 1) % 8 if (b + 1) % 8 else 0, d), F32)], axis=0)

    hx = x.reshape(tx, d)
    hc = ctx.reshape(tcx, d)
    for layer in range(depth):
        last = layer == depth - 1
        mod = _modulation(cond, w_mod[layer], b_mod[layer])
        mx6 = mod[:b].reshape(b, 1, 6, d)
        mc6 = mod[b:b + 1].reshape(1, 1, 6, d)
        sh_a, sc_a, gt_a, sh_f, sc_f, gt_f = [mx6[:, :, i] for i in range(6)]
        csh_a, csc_a, cgt_a, csh_f, csc_f, cgt_f = [mc6[:, :, i] for i in range(6)]

        w_aug, wq, wk, wv = _prep_layer_weights(w_in[layer], mla_w_uq[layer], mla_w_ukv[layer])
        w_f = w_fourier[layer].astype(BF16)
        w_o = w_out[layer].astype(BF16)
        w_s = sgu_w[layer].astype(BF16)
        bias_full = jnp.repeat(sgu_b[layer].T, SGU_CH, axis=1)
        sink = win_sink[layer]
        gq, gkv = mla_g_q[layer], mla_g_kv[layer]

        px = _inproj(hx, g_pre_mix[layer], sh_a, sc_a, w_aug, tab_x, gq, gkv, wq, wk, wv, n, tm_x)
        pc = _inproj(hc, g_pre_mix[layer], csh_a, csc_a, w_aug, tab_c, gq, gkv, wq, wk, wv, lc, tm_c)
        a_x, qw_x, kw_x, vw_x, qm_x, km_x, vm_x, u_x, vn_x = [t.reshape(b, n, -1) for t in px]
        a_c, qw_c, kw_c, vw_c, qm_c, km_c, vm_c, u_c, vn_c = [t.reshape(b, lc, -1) for t in pc]

        four_x = _fourier_latent(a_x, w_f)
        win_x = _window_latent(qw_x, kw_x, vw_x, kw_c, vw_c, sink)
        mla_x = _mla_attention(qm_x, km_x, vm_x, km_c, vm_c)
        sgu_x = _sgu(u_x, vn_x, w_s, bias_full)
        parts_x = [t.reshape(tx, GROUP_W) for t in (four_x, win_x, mla_x, sgu_x)]

        i = layer // 2
        dense = layer % 2 == 0
        dff = ffn_w_gate.shape[-1]
        tf = dff // 2 if (dff // 2) % FFN_SUB == 0 else FFN_SUB
        w_router = None
        if not dense:
            w_r = jnp.concatenate([moe_w_router[i], jnp.zeros((d, LANES - N_EXPERTS), F32)], axis=1)
            w_r_hi = w_r.astype(BF16)
            w_router = jnp.stack([w_r_hi, (w_r - w_r_hi.astype(F32)).astype(BF16)])
        res = _outproj(parts_x, w_o, hx, gt_a, g_post_mix[layer], g_pre_ffn[layer], sh_f, sc_f, n, tm_x, w_router)
        if dense:
            wg, wu, wd = ffn_w_gate[i].astype(BF16), ffn_w_up[i].astype(BF16), ffn_w_down[i].astype(BF16)
            hx = _ffn_dense(res[1], wg, wu, wd, res[0], gt_f, g_post_ffn[layer], n, min(1024, n), tf)
        else:
            wg, wu, wd = moe_w_gate[i].astype(BF16), moe_w_up[i].astype(BF16), moe_w_down[i].astype(BF16)
            hx = _ffn_moe(res[1], res[2], wg, wu, wd, res[0], gt_f, g_post_ffn[layer], n, tf)

        if not last:
            four_c = _fourier_context(a_c, w_f)
            win_c = _window_context(qw_c, kw_c, vw_c, sink)
            mla_c = _mla_attention_small(qm_c, km_c, vm_c)
            sgu_c = _sgu(u_c, vn_c, w_s, bias_full)
            parts_c = [t.reshape(tcx, GROUP_W) for t in (four_c, win_c, mla_c, sgu_c)]
            res_c = _outproj(parts_c, w_o, hc, cgt_a, g_post_mix[layer], g_pre_ffn[layer], csh_f, csc_f, lc, tm_c, w_router)
            if dense:
                hc = _ffn_dense(res_c[1], wg, wu, wd, res_c[0], cgt_f, g_post_ffn[layer], lc, tm_c, tf)
            else:
                hc = _ffn_moe(res_c[1], res_c[2], wg, wu, wd, res_c[0], cgt_f, g_post_ffn[layer], lc, tf)
    return hx.reshape(b, n, d)
```

```python
import functools

import numpy as np
import jax
import jax.numpy as jnp
from jax import lax
from jax.experimental import pallas as pl
from jax.experimental.pallas import tpu as pltpu

F32 = jnp.float32
BF16 = jnp.bfloat16
HIGHEST = lax.Precision.HIGHEST

GRID_W = 64
GROUP_W = 256
FNET_GROUPS = 4
FNET_CH = 64
WIN_HEADS = 4
WIN_KV_HEADS = 2
WIN_HEAD_DIM = 64
WINDOW = 128
BLOCK = 128
MLA_HEADS = 4
MLA_NOPE = 64
MLA_ROPE = 32
MLA_QK = MLA_NOPE + MLA_ROPE
MLA_V = 64
MLA_Q_RANK = 256
MLA_KV_RANK = 128
SGU_GROUPS = 4
SGU_CHUNK = 128
SGU_CH = 64
N_EXPERTS = 8
ROPE_THETA = 10000.0
EPS = 1e-6
NEG_INF = -1e30
LOG2E = 1.4426950408889634

LANES = 128
MLA_HEAD_PAD = 128
FFT_N1 = 128
MOE_TILE = 1024
FFN_SUB = 256
VMEM_LIMIT = 56 * 1024 * 1024

_O_A, _O_Q, _O_K, _O_V, _O_CQ, _O_CKV, _O_KRP, _O_SG, _O_END = (0, 256, 512, 640, 896, 1152, 1280, 1408, 1920)


def _cparams(*sem):
    return pltpu.CompilerParams(dimension_semantics=sem, vmem_limit_bytes=VMEM_LIMIT)


def _full(shape):
    nd = len(shape)
    return pl.BlockSpec(shape, lambda *_: (0,) * nd)


def _dot(a, b):
    return jnp.dot(a, b, preferred_element_type=F32)


def _dot_nt(a, b):
    return lax.dot_general(a, b, (((1,), (1,)), ((), ())), preferred_element_type=F32)


def _dot2(hi, lo, x):
    return _dot(hi, x) + _dot(lo, x)


def _split_hi_lo(m):
    m = np.asarray(m, np.float32)
    hi = jnp.asarray(m, F32).astype(BF16)
    lo = (jnp.asarray(m, F32) - hi.astype(F32)).astype(BF16)
    return hi, lo


def _rms(x, g):
    return x * lax.rsqrt(jnp.mean(x * x, axis=-1, keepdims=True) + EPS) * g


def _mod_kernel(c_ref, w_ref, b_ref, o_ref):
    c = c_ref[...]
    s = c / (1.0 + jnp.exp(-c))
    o_ref[...] = jnp.dot(s, w_ref[...], preferred_element_type=F32, precision=HIGHEST) + b_ref[...]


def _modulation(cond, w_mod, b_mod):
    r, d = cond.shape
    n6 = w_mod.shape[1]
    tn = 512
    return pl.pallas_call(
        _mod_kernel,
        grid=(n6 // tn,),
        in_specs=[_full((r, d)), pl.BlockSpec((d, tn), lambda j: (0, j)), pl.BlockSpec((1, tn), lambda j: (0, j))],
        out_specs=pl.BlockSpec((r, tn), lambda j: (0, j)),
        out_shape=jax.ShapeDtypeStruct((r, n6), F32),
        compiler_params=_cparams("arbitrary"),
        name="adaln_mod",
    )(cond, w_mod, b_mod.reshape(1, n6))


def _inproj_kernel(h_ref, g_ref, sh_ref, sc_ref, w_ref, cw_ref, sw_ref, cm_ref, sm_ref,
                   gq_ref, gkv_ref, wq_ref, wk_ref, wv_ref,
                   a_o, qw_o, kw_o, vw_o, qm_o, km_o, vm_o, u_o, vn_o):
    x = h_ref[...]
    ax = (_rms(x, g_ref[...]) * (1.0 + sc_ref[0]) + sh_ref[0]).astype(BF16)

    def proj(lo, hi):
        return _dot(ax, w_ref[:, lo:hi])

    def rope(z, cos, sin_signed):
        w = z.shape[1]
        even = lax.broadcasted_iota(jnp.int32, z.shape, 1) % 2 == 0
        partner = jnp.where(even, pltpu.roll(z, w - 1, 1), pltpu.roll(z, 1, 1))
        return z * cos + partner * sin_signed

    a_o[...] = proj(_O_A, _O_Q).astype(BF16)
    cw = cw_ref[...]
    sw = sw_ref[...]
    qw_o[...] = (rope(proj(_O_Q, _O_K), cw, sw) * ((WIN_HEAD_DIM ** -0.5) * LOG2E)).astype(BF16)
    nk = WIN_KV_HEADS * WIN_HEAD_DIM
    kw_o[...] = rope(proj(_O_K, _O_V), cw[:, :nk], sw[:, :nk]).astype(BF16)
    vw = proj(_O_V, _O_CQ)
    vw_lane = lax.broadcasted_iota(jnp.int32, vw.shape, 1) % LANES
    vw_o[...] = jnp.where(vw_lane == WIN_HEAD_DIM, 1.0, vw).astype(BF16)

    cm = cm_ref[...]
    sm = sm_ref[...]
    cqn = _rms(proj(_O_CQ, _O_CKV), gq_ref[...]).astype(BF16)
    y1 = _dot(cqn, wq_ref[...])
    scale = (MLA_QK ** -0.5) * LOG2E
    for h in range(MLA_HEADS):
        sl = slice(h * MLA_HEAD_PAD, (h + 1) * MLA_HEAD_PAD)
        qm_o[:, sl] = (rope(y1[:, sl], cm, sm) * scale).astype(BF16)

    ckvn = _rms(proj(_O_CKV, _O_KRP), gkv_ref[...]).astype(BF16)
    krp = rope(proj(_O_KRP, _O_SG), cm, sm)
    kk = _dot(ckvn, wk_ref[...])
    for h in range(MLA_HEADS):
        sl = slice(h * MLA_HEAD_PAD, (h + 1) * MLA_HEAD_PAD)
        km_o[:, sl] = (kk[:, sl] + krp).astype(BF16)
    vv = _dot(ckvn, wv_ref[...])
    lane = lax.broadcasted_iota(jnp.int32, vv.shape, 1) % MLA_HEAD_PAD
    vm_o[...] = jnp.where(lane == MLA_V, 1.0, vv).astype(BF16)

    sg = proj(_O_SG, _O_END)
    gl = 0.5 * sg * (1.0 + lax.erf(sg * (2.0 ** -0.5)))
    u_o[...] = gl[:, :GROUP_W].astype(BF16)
    v = gl[:, GROUP_W:]
    mu = jnp.mean(v, axis=-1, keepdims=True)
    vc = v - mu
    var = jnp.mean(vc * vc, axis=-1, keepdims=True)
    vn_o[...] = (vc * lax.rsqrt(var + EPS)).astype(BF16)


def _inproj(h, g, shift, scale, w_aug, tables, gq, gkv, wq, wk, wv, per_batch, tm):
    t, d = h.shape
    nmod = shift.shape[0]
    tpb = per_batch // tm
    cw, sw, cm, sm = tables

    def mod_idx(i):
        return ((i // tpb) if nmod > 1 else 0, 0, 0)

    tok = lambda w: pl.BlockSpec((tm, w), lambda i: (i, 0))
    tab = lambda w: pl.BlockSpec((tm, w), lambda i: (i % tpb, 0))
    in_specs = [tok(d), _full((1, d)), pl.BlockSpec((1, 1, d), mod_idx), pl.BlockSpec((1, 1, d), mod_idx),
                _full(w_aug.shape), tab(256), tab(256), tab(LANES), tab(LANES),
                _full((1, MLA_Q_RANK)), _full((1, MLA_KV_RANK)),
                _full(wq.shape), _full(wk.shape), _full(wv.shape)]
    widths = (256, 256, 128, 256, 512, 512, 512, 256, 256)
    return pl.pallas_call(
        _inproj_kernel,
        grid=(t // tm,),
        in_specs=in_specs,
        out_specs=[tok(w) for w in widths],
        out_shape=[jax.ShapeDtypeStruct((t, w), BF16) for w in widths],
        compiler_params=_cparams("parallel"),
        name="inproj",
    )(h, g.reshape(1, d), shift, scale, w_aug, cw, sw, cm, sm, gq.reshape(1, -1), gkv.reshape(1, -1), wq, wk, wv)


def _fft1_kernel(x_ref, mh_ref, ml_ref, tr_ref, ti_ref, or_ref, oi_ref):
    x = x_ref[0]
    z = _dot2(mh_ref[...], ml_ref[...], x)
    ar, ai = z[:FFT_N1], z[FFT_N1:]
    tr, ti = tr_ref[...], ti_ref[...]
    or_ref[0] = (ar * tr - ai * ti).astype(BF16)
    oi_ref[0] = (ar * ti + ai * tr).astype(BF16)


def _fft2_kernel(r_ref, i_ref, cs_ref, wch_ref, wcl_ref, wf_ref, o_ref, g_scr, *, nj, n2):
    cs = cs_ref[...]
    gw = o_ref.shape[-1]
    for j in range(nj):
        zr = _dot(cs, r_ref[0, j])
        zi = _dot(cs, i_ref[0, j])
        c_r, s_r = zr[:n2] + zr[n2:2 * n2], zr[2 * n2:3 * n2] + zr[3 * n2:]
        c_i, s_i = zi[:n2] + zi[n2:2 * n2], zi[2 * n2:3 * n2] + zi[3 * n2:]
        g_scr[j * n2:(j + 1) * n2, :gw] = (c_r + s_i).astype(BF16)
        g_scr[j * n2:(j + 1) * n2, gw:] = (c_i - s_r).astype(BF16)
    g = g_scr[...]
    y = _dot(g, wch_ref[...]) + _dot(g, wcl_ref[...])
    o = _dot(y.astype(BF16), wf_ref[...]).astype(BF16)
    for j in range(nj):
        o_ref[0, j] = o[j * n2:(j + 1) * n2]


def _dft_small_kernel(a_ref, ch_ref, cl_ref, sh_ref, sl_ref, bch_ref, bcl_ref, bsh_ref, bsl_ref, wf_ref, o_ref):
    a = a_ref[0]
    gr = _dot2(ch_ref[...], cl_ref[...], a).astype(BF16)
    gi = (-_dot2(sh_ref[...], sl_ref[...], a)).astype(BF16)
    y = _dot(gr, bch_ref[...]) + _dot(gr, bcl_ref[...]) + _dot(gi, bsh_ref[...]) + _dot(gi, bsl_ref[...])
    o_ref[0] = _dot(y.astype(BF16), wf_ref[...]).astype(BF16)


def _cos_sin(n):
    k = np.arange(n)
    ang = 2.0 * np.pi * ((k[:, None] * k[None, :]) % n) / n
    return np.cos(ang), np.sin(ang)


def _channel_dft(norm):
    c, s = _cos_sin(FNET_CH)
    bc = np.kron(np.eye(FNET_GROUPS), c) * norm
    bs = np.kron(np.eye(FNET_GROUPS), s) * norm
    return _split_hi_lo(bc) + _split_hi_lo(bs)


def _fourier_latent(a, w_f):
    b, n, gw = a.shape
    n2 = n // FFT_N1
    c1, s1 = _cos_sin(FFT_N1)
    mh, ml = _split_hi_lo(np.concatenate([c1, -s1], axis=0))
    k1 = np.arange(FFT_N1)[:, None]
    t2 = np.arange(n2)[None, :]
    ang = 2.0 * np.pi * ((k1 * t2) % n) / n
    tr = jnp.repeat(jnp.asarray(np.cos(ang), F32), gw, axis=1)
    ti = jnp.repeat(jnp.asarray(-np.sin(ang), F32), gw, axis=1)
    cols = n2 * gw
    tc = min(cols, 4096)
    x2 = a.reshape(b, FFT_N1, cols)
    blk = pl.BlockSpec((1, FFT_N1, tc), lambda bi, ci: (bi, 0, ci))
    tblk = pl.BlockSpec((FFT_N1, tc), lambda bi, ci: (0, ci))
    o_r, o_i = pl.pallas_call(
        _fft1_kernel,
        grid=(b, cols // tc),
        in_specs=[blk, _full(mh.shape), _full(ml.shape), tblk, tblk],
        out_specs=[blk, blk],
        out_shape=[jax.ShapeDtypeStruct((b, FFT_N1, cols), BF16)] * 2,
        compiler_params=_cparams("parallel", "parallel"),
        name="fft_stage1",
    )(x2, mh, ml, tr, ti)

    c2, s2 = _cos_sin(n2)
    cs = jnp.concatenate(_split_hi_lo(c2) + _split_hi_lo(s2), axis=0)
    bch, bcl, bsh, bsl = _channel_dft((n * FNET_CH) ** -0.5)
    wch = jnp.concatenate([bch, bsh], axis=0)
    wcl = jnp.concatenate([bcl, bsl], axis=0)
    nj = 8
    r4 = o_r.reshape(b, FFT_N1, n2, gw)
    i4 = o_i.reshape(b, FFT_N1, n2, gw)
    blk4 = pl.BlockSpec((1, nj, n2, gw), lambda bi, ki: (bi, ki, 0, 0))
    out = pl.pallas_call(
        functools.partial(_fft2_kernel, nj=nj, n2=n2),
        grid=(b, FFT_N1 // nj),
        in_specs=[blk4, blk4, _full(cs.shape), _full(wch.shape), _full(wcl.shape), _full(w_f.shape)],
        out_specs=blk4,
        out_shape=jax.ShapeDtypeStruct((b, FFT_N1, n2, gw), BF16),
        scratch_shapes=[pltpu.VMEM((nj * n2, 2 * gw), BF16)],
        compiler_params=_cparams("parallel", "parallel"),
        name="fft_stage2",
    )(r4, i4, cs, wch, wcl, w_f)
    return jnp.swapaxes(out, 1, 2).reshape(b, n, gw)


def _fourier_context(a, w_f):
    b, n, gw = a.shape
    c, s = _cos_sin(n)
    consts = _split_hi_lo(c) + _split_hi_lo(s) + _channel_dft((n * FNET_CH) ** -0.5)
    blk = pl.BlockSpec((1, n, gw), lambda bi: (bi, 0, 0))
    return pl.pallas_call(
        _dft_small_kernel,
        grid=(b,),
        in_specs=[blk] + [_full(m.shape) for m in consts] + [_full(w_f.shape)],
        out_specs=blk,
        out_shape=jax.ShapeDtypeStruct((b, n, gw), BF16),
        compiler_params=_cparams("parallel"),
        name="dft_context",
    )(a, *consts, w_f)


def _win_heads(q_blk, k_win, v_win, bias, kc, vc, sink_ref):
    rows = lax.broadcasted_iota(jnp.int32, (WIN_KV_HEADS * BLOCK, 1), 0)
    outs = [None] * WIN_HEADS
    d = WIN_HEAD_DIM
    for hk in range(WIN_KV_HEADS):
        h0 = 2 * hk
        q2 = jnp.concatenate([q_blk[:, h0 * d:(h0 + 1) * d], q_blk[:, (h0 + 1) * d:(h0 + 2) * d]], axis=0)
        ksl = slice(hk * d, (hk + 1) * d)
        vsl = slice(hk * LANES, (hk + 1) * LANES)
        sink = jnp.where(rows < BLOCK, sink_ref[h0] * LOG2E, sink_ref[h0 + 1] * LOG2E)
        s_c = _dot_nt(q2, kc[:, ksl])
        m = jnp.maximum(jnp.max(s_c, axis=-1, keepdims=True), sink)
        if k_win is not None:
            s_w = _dot_nt(q2, k_win[:, ksl]) + bias
            m = jnp.maximum(m, jnp.max(s_w, axis=-1, keepdims=True))
        r = _dot(jnp.exp2(s_c - m).astype(BF16), vc[:, vsl])
        if k_win is not None:
            r = r + _dot(jnp.exp2(s_w - m).astype(BF16), v_win[:, vsl])
        o = r[:, :d] / (r[:, d:d + 1] + jnp.exp2(sink - m))
        outs[h0] = o[:BLOCK]
        outs[h0 + 1] = o[BLOCK:]
    return jnp.concatenate(outs, axis=1)


def _win_kernel(sink_ref, q_ref, kp_ref, k_ref, kn_ref, vp_ref, v_ref, vn_ref, kc_ref, vc_ref, o_ref, *, nblk):
    i = pl.program_id(1)
    kfull = jnp.concatenate([kp_ref[0], k_ref[0], kn_ref[0]], axis=0)
    vfull = jnp.concatenate([vp_ref[0], v_ref[0], vn_ref[0]], axis=0)
    kc = kc_ref[0]
    vc = vc_ref[0]
    shape = (WIN_KV_HEADS * BLOCK, 3 * BLOCK)
    qi = lax.broadcasted_iota(jnp.int32, shape, 0) % BLOCK
    kj = lax.broadcasted_iota(jnp.int32, shape, 1)
    band = jnp.abs(kj - BLOCK - qi) <= WINDOW
    has_prev = jnp.logical_or(kj >= BLOCK, i > 0)
    has_next = jnp.logical_or(kj < 2 * BLOCK, i < pl.num_programs(1) - 1)
    bias_mid = jnp.where(band, 0.0, NEG_INF)
    for j in range(nblk):
        ok = band
        if j == 0:
            ok = ok & has_prev
        if j == nblk - 1:
            ok = ok & has_next
        bias = bias_mid if ok is band else jnp.where(ok, 0.0, NEG_INF)
        o = _win_heads(q_ref[0, j * BLOCK:(j + 1) * BLOCK, :], kfull[j * BLOCK:(j + 3) * BLOCK],
                       vfull[j * BLOCK:(j + 3) * BLOCK], bias, kc, vc, sink_ref)
        o_ref[0, j * BLOCK:(j + 1) * BLOCK, :] = o.astype(BF16)


def _winctx_kernel(sink_ref, q_ref, kc_ref, vc_ref, o_ref, *, nblk):
    kc = kc_ref[0]
    vc = vc_ref[0]
    for j in range(nblk):
        o = _win_heads(q_ref[0, j * BLOCK:(j + 1) * BLOCK, :], None, None, None, kc, vc, sink_ref)
        o_ref[0, j * BLOCK:(j + 1) * BLOCK, :] = o.astype(BF16)


def _window_latent(q, k, v, kc, vc, sink):
    b, n, _ = q.shape
    lc = kc.shape[1]
    nblk = min(8, n // BLOCK)
    tq = nblk * BLOCK
    nb = n // BLOCK
    kw = WIN_KV_HEADS * WIN_HEAD_DIM
    vw = WIN_KV_HEADS * LANES
    cur = lambda w: pl.BlockSpec((1, tq, w), lambda bi, i: (bi, i, 0))
    prev = lambda w: pl.BlockSpec((1, BLOCK, w), lambda bi, i: (bi, jnp.maximum(i * nblk - 1, 0), 0))
    nxt = lambda w: pl.BlockSpec((1, BLOCK, w), lambda bi, i: (bi, jnp.minimum((i + 1) * nblk, nb - 1), 0))
    ctx = lambda w: pl.BlockSpec((1, lc, w), lambda bi, i: (bi, 0, 0))
    smem = pl.BlockSpec(memory_space=pltpu.SMEM)
    return pl.pallas_call(
        functools.partial(_win_kernel, nblk=nblk),
        grid=(b, n // tq),
        in_specs=[smem, cur(256), prev(kw), cur(kw), nxt(kw), prev(vw), cur(vw), nxt(vw), ctx(kw), ctx(vw)],
        out_specs=cur(256),
        out_shape=jax.ShapeDtypeStruct((b, n, 256), BF16),
        compiler_params=_cparams("parallel", "arbitrary"),
        name="window_attn",
    )(sink, q, k, k, k, v, v, v, kc, vc)


def _window_context(q, kc, vc, sink):
    b, lc, _ = q.shape
    kvw = WIN_KV_HEADS * WIN_HEAD_DIM
    blk = lambda w: pl.BlockSpec((1, lc, w), lambda bi: (bi, 0, 0))
    smem = pl.BlockSpec(memory_space=pltpu.SMEM)
    return pl.pallas_call(
        functools.partial(_winctx_kernel, nblk=lc // BLOCK),
        grid=(b,),
        in_specs=[smem, blk(256), blk(kvw), blk(WIN_KV_HEADS * LANES)],
        out_specs=blk(256),
        out_shape=jax.ShapeDtypeStruct((b, lc, 256), BF16),
        compiler_params=_cparams("parallel"),
        name="window_attn_ctx",
    )(sink, q, kc, vc)


def _mla_reduce(state, s, v):
    m, acc = state
    m_new = jnp.maximum(m, jnp.max(s, axis=-1, keepdims=True))
    alpha = jnp.exp2(m - m_new)
    p = jnp.exp2(s - m_new).astype(BF16)
    return m_new, alpha * acc + _dot(p, v)


def _mla_finish(acc):
    return (acc[:, :MLA_V] / acc[:, MLA_V:MLA_V + 1]).astype(BF16)


def _mla_kernel(q_ref, k_ref, v_ref, kc_ref, vc_ref, o_ref, s_scr, sc_scr, *, tk, nk):
    tq = q_ref.shape[1]
    hsl = [slice(h * MLA_HEAD_PAD, (h + 1) * MLA_HEAD_PAD) for h in range(2)]
    qs = [q_ref[0, :, sl] for sl in hsl]

    def scores(c, slot):
        off = pl.multiple_of(c * tk, tk)
        for h in range(2):
            s_scr[slot, h] = _dot_nt(qs[h], k_ref[0, pl.ds(off, tk), hsl[h]])

    def reduce(c, slot, states):
        off = pl.multiple_of(c * tk, tk)
        return tuple(_mla_reduce(states[h], s_scr[slot, h], v_ref[0, pl.ds(off, tk), hsl[h]]) for h in range(2))

    scores(0, 0)

    def body(i, states):
        scores(2 * i + 1, 1)
        states = reduce(2 * i, 0, states)
        scores(2 * i + 2, 0)
        return reduce(2 * i + 1, 1, states)

    init = tuple((jnp.full((tq, 1), NEG_INF, F32), jnp.zeros((tq, MLA_HEAD_PAD), F32)) for _ in range(2))
    states = lax.fori_loop(0, nk // 2 - 1, body, init)
    scores(nk - 1, 1)
    states = reduce(nk - 2, 0, states)
    for h in range(2):
        sc_scr[h] = _dot_nt(qs[h], kc_ref[0, :, hsl[h]])
    states = reduce(nk - 1, 1, states)
    for h in range(2):
        _, acc = _mla_reduce(states[h], sc_scr[h], vc_ref[0, :, hsl[h]])
        o_ref[0, :, h * MLA_V:(h + 1) * MLA_V] = _mla_finish(acc)


def _mla_small_kernel(q_ref, k_ref, v_ref, o_ref):
    tq = q_ref.shape[1]
    for h in range(2):
        sl = slice(h * MLA_HEAD_PAD, (h + 1) * MLA_HEAD_PAD)
        init = (jnp.full((tq, 1), NEG_INF, F32), jnp.zeros((tq, MLA_HEAD_PAD), F32))
        _, acc = _mla_reduce(init, _dot_nt(q_ref[0, :, sl], k_ref[0, :, sl]), v_ref[0, :, sl])
        o_ref[0, :, h * MLA_V:(h + 1) * MLA_V] = _mla_finish(acc)


def _mla_attention(q, k, v, kc, vc):
    b, nq, _ = q.shape
    nkeys = k.shape[1]
    lc = kc.shape[1]
    tq = min(512, nq)
    tk = min(1024, nkeys // 2)
    assert nkeys % (2 * tk) == 0 and nq % tq == 0
    w2 = 2 * MLA_HEAD_PAD
    res = lambda n: pl.BlockSpec((1, n, w2), lambda bi, h, i: (bi, 0, h))
    return pl.pallas_call(
        functools.partial(_mla_kernel, tk=tk, nk=nkeys // tk),
        grid=(b, MLA_HEADS // 2, nq // tq),
        in_specs=[pl.BlockSpec((1, tq, w2), lambda bi, h, i: (bi, i, h)), res(nkeys), res(nkeys), res(lc), res(lc)],
        out_specs=pl.BlockSpec((1, tq, 2 * MLA_V), lambda bi, h, i: (bi, i, h)),
        out_shape=jax.ShapeDtypeStruct((b, nq, MLA_HEADS * MLA_V), BF16),
        scratch_shapes=[pltpu.VMEM((2, 2, tq, tk), F32), pltpu.VMEM((2, tq, lc), F32)],
        compiler_params=_cparams("parallel", "parallel", "arbitrary"),
        name="mla_attn",
    )(q, k, v, kc, vc)


def _mla_attention_small(q, k, v):
    b, n, _ = q.shape
    w2 = 2 * MLA_HEAD_PAD
    blk = pl.BlockSpec((1, n, w2), lambda bi, h: (bi, 0, h))
    return pl.pallas_call(
        _mla_small_kernel,
        grid=(b, MLA_HEADS // 2),
        in_specs=[blk, blk, blk],
        out_specs=pl.BlockSpec((1, n, 2 * MLA_V), lambda bi, h: (bi, 0, h)),
        out_shape=jax.ShapeDtypeStruct((b, n, MLA_HEADS * MLA_V), BF16),
        compiler_params=_cparams("parallel", "parallel"),
        name="mla_attn_ctx",
    )(q, k, v)


def _sgu_kernel(u_ref, v_ref, w_ref, b_ref, o_ref, *, nchunk):
    lane = lax.broadcasted_iota(jnp.int32, (SGU_CHUNK, GROUP_W), 1) // SGU_CH
    bias = b_ref[...]
    for c in range(nchunk):
        sl = slice(c * SGU_CHUNK, (c + 1) * SGU_CHUNK)
        v = v_ref[0, sl, :]
        s = bias
        for g in range(SGU_GROUPS):
            s = s + _dot(w_ref[g], jnp.where(lane == g, v, jnp.zeros_like(v)))
        o_ref[0, sl, :] = (u_ref[0, sl, :].astype(F32) * s).astype(BF16)


def _sgu(u, vn, w_s, bias_full):
    b, n, gw = u.shape
    nchunk = min(8, n // SGU_CHUNK)
    tn = nchunk * SGU_CHUNK
    blk = pl.BlockSpec((1, tn, gw), lambda bi, i: (bi, i, 0))
    return pl.pallas_call(
        functools.partial(_sgu_kernel, nchunk=nchunk),
        grid=(b, n // tn),
        in_specs=[blk, blk, _full(w_s.shape), _full(bias_full.shape)],
        out_specs=blk,
        out_shape=jax.ShapeDtypeStruct((b, n, gw), BF16),
        compiler_params=_cparams("parallel", "parallel"),
        name="sgu",
    )(u, vn, w_s, bias_full)


def _outproj_kernel(*refs, with_router):
    if with_router:
        (p0, p1, p2, p3, w_ref, h_ref, gate_ref, gpost_ref, gpre_ref, sh_ref, sc_ref, wr_ref,
         hn_o, fx_o, comb_o) = refs
    else:
        p0, p1, p2, p3, w_ref, h_ref, gate_ref, gpost_ref, gpre_ref, sh_ref, sc_ref, hn_o, fx_o = refs
    mx = None
    for idx, p in enumerate((p0, p1, p2, p3)):
        t = _dot(p[...], w_ref[idx * GROUP_W:(idx + 1) * GROUP_W, :])
        mx = t if mx is None else mx + t
    hn = h_ref[...] + gate_ref[0] * _rms(mx, gpost_ref[...])
    hn_o[...] = hn
    fx = _rms(hn, gpre_ref[...]) * (1.0 + sc_ref[0]) + sh_ref[0]
    fx_o[...] = fx.astype(fx_o.dtype)
    if with_router:
        fx_hi = fx.astype(BF16)
        fx_lo = (fx - fx_hi.astype(F32)).astype(BF16)
        logits = _dot(fx_hi, wr_ref[0]) + (_dot(fx_lo, wr_ref[0]) + _dot(fx_hi, wr_ref[1]))
        lane = lax.broadcasted_iota(jnp.int32, logits.shape, 1).astype(F32)
        lg = jnp.where(lane < N_EXPERTS, logits, -jnp.inf)
        m1 = jnp.max(lg, axis=-1, keepdims=True)
        i1 = jnp.min(jnp.where(lg == m1, lane, float(LANES)), axis=-1, keepdims=True)
        lg2 = jnp.where(lane == i1, -jnp.inf, lg)
        m2 = jnp.max(lg2, axis=-1, keepdims=True)
        i2 = jnp.min(jnp.where(lg2 == m2, lane, float(LANES)), axis=-1, keepdims=True)
        e2 = jnp.exp(m2 - m1)
        den = 1.0 + e2
        flag = (lane == i1 + N_EXPERTS) | (lane == i2 + N_EXPERTS)
        comb_o[...] = (jnp.where(lane == i1, 1.0 / den, 0.0) + jnp.where(lane == i2, e2 / den, 0.0)
                       + jnp.where(flag, 1.0, 0.0))


def _outproj(parts, w_out, h, gate, g_post, g_pre, shift, scale, per_batch, tm, w_router=None):
    t, d = h.shape
    nmod = gate.shape[0]
    tpb = per_batch // tm
    with_router = w_router is not None

    def mod_idx(i):
        return ((i // tpb) if nmod > 1 else 0, 0, 0)

    tok = lambda w: pl.BlockSpec((tm, w), lambda i: (i, 0))
    mod = pl.BlockSpec((1, 1, d), mod_idx)
    in_specs = [tok(GROUP_W)] * 4 + [_full(w_out.shape), tok(d), mod, _full((1, d)), _full((1, d)), mod, mod]
    args = list(parts) + [w_out, h, gate, g_post.reshape(1, d), g_pre.reshape(1, d), shift, scale]
    out_specs = [tok(d), tok(d)]
    out_shape = [jax.ShapeDtypeStruct((t, d), F32), jax.ShapeDtypeStruct((t, d), F32 if with_router else BF16)]
    if with_router:
        in_specs.append(_full(w_router.shape))
        args.append(w_router)
        out_specs.append(tok(LANES))
        out_shape.append(jax.ShapeDtypeStruct((t, LANES), F32))
    return pl.pallas_call(
        functools.partial(_outproj_kernel, with_router=with_router),
        grid=(t // tm,),
        in_specs=in_specs,
        out_specs=out_specs,
        out_shape=out_shape,
        compiler_params=_cparams("parallel"),
        name="outproj",
    )(*args)


def _swiglu_partial(x, wg, wu, wd, tf):
    y = None
    for lo in range(0, tf, FFN_SUB):
        hi = lo + FFN_SUB
        g = _dot(x, wg(lo, hi))
        u = _dot(x, wu(lo, hi))
        hid = (g / (1.0 + jnp.exp(-g)) * u).astype(BF16)
        t = _dot(hid, wd(lo, hi))
        y = t if y is None else y + t
    return y


def _ffn_kernel(x_ref, wg_ref, wu_ref, wd_ref, h_ref, gate_ref, gpost_ref, o_ref):
    f = pl.program_id(1)
    last = pl.num_programs(1) - 1
    y = _swiglu_partial(x_ref[...], lambda lo, hi: wg_ref[:, lo:hi], lambda lo, hi: wu_ref[:, lo:hi],
                        lambda lo, hi: wd_ref[lo:hi, :], wd_ref.shape[0])

    @pl.when(f == 0)
    def _():
        o_ref[...] = y

    @pl.when((f > 0) & (f < last))
    def _():
        o_ref[...] += y

    @pl.when(f == last)
    def _():
        o_ref[...] = h_ref[...] + gate_ref[0] * _rms(o_ref[...] + y, gpost_ref[...])


def _ffn_dense(fx, wg, wu, wd, h, gate, g_post, per_batch, tm, tf):
    t, d = h.shape
    dff = wg.shape[1]
    nmod = gate.shape[0]
    tpb = per_batch // tm
    assert dff // tf >= 2 and tf % FFN_SUB == 0

    def mod_idx(i, f):
        return ((i // tpb) if nmod > 1 else 0, 0, 0)

    tok = pl.BlockSpec((tm, d), lambda i, f: (i, 0))
    return pl.pallas_call(
        _ffn_kernel,
        grid=(t // tm, dff // tf),
        in_specs=[tok, pl.BlockSpec((d, tf), lambda i, f: (0, f)), pl.BlockSpec((d, tf), lambda i, f: (0, f)),
                  pl.BlockSpec((tf, d), lambda i, f: (f, 0)), tok, pl.BlockSpec((1, 1, d), mod_idx),
                  pl.BlockSpec((1, d), lambda i, f: (0, 0))],
        out_specs=tok,
        out_shape=jax.ShapeDtypeStruct((t, d), F32),
        compiler_params=_cparams("parallel", "arbitrary"),
        name="ffn_dense",
    )(fx, wg, wu, wd, h, gate, g_post.reshape(1, d))


def _route(comb, tile):
    t = comb.shape[0]
    sel = (comb[:, N_EXPERTS:2 * N_EXPERTS] > 0.5).astype(jnp.int32)
    rank = jnp.cumsum(sel, axis=0) - sel
    padded = (jnp.sum(sel, axis=0) + tile - 1) // tile * tile
    ends = jnp.cumsum(padded)
    pos_e = (ends - padded)[None, :] + rank
    e_id = jnp.arange(N_EXPERTS, dtype=jnp.int32)[None, :]
    e_lo = jnp.min(jnp.where(sel > 0, e_id, N_EXPERTS - 1), axis=1)
    e_hi = jnp.max(sel * e_id, axis=1)
    take = lambda a, e: jnp.take_along_axis(a, e[:, None], axis=1)[:, 0]
    pos = jnp.stack([take(pos_e, e_lo), take(pos_e, e_hi)], axis=1).reshape(2 * t).astype(jnp.int32)
    w = comb[:, :N_EXPERTS]
    wts = jnp.pad(jnp.stack([take(w, e_lo), take(w, e_hi)], axis=1), ((0, 0), (0, LANES - 2)))
    ntiles = -(-2 * t // tile) + N_EXPERTS
    nact = (ends[-1] // tile).astype(jnp.int32)
    tile_id = jnp.arange(ntiles, dtype=jnp.int32)
    te = jnp.minimum(jnp.searchsorted(ends, tile_id * tile, side="right"), N_EXPERTS - 1).astype(jnp.int32)
    te = jnp.where(tile_id < nact, te, te[nact - 1])
    return pos, wts, te, nact.reshape(1), ntiles


def _dispatch_kernel(pos_ref, x_ref, xs_in_ref, xs_ref, sem, *, tm):
    del xs_in_ref
    base = pl.program_id(0) * tm

    def row_copy(t, p):
        return pltpu.make_async_copy(x_ref.at[pl.ds(t, 1)], xs_ref.at[pl.ds(p, 1)], sem)

    def issue(t, carry):
        row_copy(t, pos_ref[2 * (base + t)]).start(priority=0)
        row_copy(t, pos_ref[2 * (base + t) + 1]).start(priority=1)
        return carry

    def drain(t, carry):
        row_copy(0, 0).wait()
        row_copy(0, 0).wait()
        return carry

    lax.fori_loop(0, tm, issue, 0, unroll=8)
    lax.fori_loop(0, tm, drain, 0, unroll=8)


def _gmm_kernel(te_ref, nact_ref, x_ref, wg_ref, wu_ref, wd_ref, o_ref):
    del te_ref
    f = pl.program_id(1)
    active = pl.program_id(0) < nact_ref[0]

    @pl.when(jnp.logical_not(active) & (f == 0))
    def _():
        o_ref[...] = jnp.zeros_like(o_ref)

    @pl.when(active)
    def _():
        y = _swiglu_partial(x_ref[...].astype(BF16), lambda lo, hi: wg_ref[0, :, lo:hi],
                            lambda lo, hi: wu_ref[0, :, lo:hi], lambda lo, hi: wd_ref[0, lo:hi, :], wd_ref.shape[1])

        @pl.when(f == 0)
        def _():
            o_ref[...] = y

        @pl.when(f > 0)
        def _():
            o_ref[...] += y


def _combine_kernel(pos_ref, ys_ref, w_ref, h_ref, gate_ref, gpost_ref, o_ref, buf, sem, *, tm):
    base = pl.program_id(0) * tm

    def row_copy(choice, t, p):
        return pltpu.make_async_copy(ys_ref.at[pl.ds(p, 1)], buf.at[choice, pl.ds(t, 1)], sem)

    def issue(t, carry):
        row_copy(0, t, pos_ref[2 * (base + t)]).start(priority=0)
        row_copy(1, t, pos_ref[2 * (base + t) + 1]).start(priority=1)
        return carry

    def drain(t, carry):
        row_copy(0, 0, 0).wait()
        row_copy(1, 0, 0).wait()
        return carry

    lax.fori_loop(0, tm, issue, 0, unroll=8)
    lax.fori_loop(0, tm, drain, 0, unroll=8)
    w = w_ref[...]
    y = w[:, 0:1] * buf[0] + w[:, 1:2] * buf[1]
    o_ref[...] = h_ref[...] + gate_ref[0] * _rms(y, gpost_ref[...])


def _ffn_moe(fx, comb, wg, wu, wd, h, gate, g_post, per_batch, tf):
    t, d = h.shape
    dff = wg.shape[2]
    nf = dff // tf
    tile = MOE_TILE
    pos, wts, te, nact, ntiles = _route(comb, tile)
    ts = ntiles * tile
    any_spec = pl.BlockSpec(memory_space=pl.ANY)

    tm_d = min(512, per_batch)
    xs = pl.pallas_call(
        functools.partial(_dispatch_kernel, tm=tm_d),
        grid_spec=pltpu.PrefetchScalarGridSpec(
            num_scalar_prefetch=1, grid=(t // tm_d,),
            in_specs=[pl.BlockSpec((tm_d, d), lambda i, p: (i, 0)), any_spec],
            out_specs=any_spec,
            scratch_shapes=[pltpu.SemaphoreType.DMA(())]),
        out_shape=jax.ShapeDtypeStruct((ts, d), F32),
        input_output_aliases={2: 0},
        compiler_params=_cparams("arbitrary"),
        name="moe_dispatch",
    )(pos, fx, jnp.zeros((ts, d), F32))

    row = lambda i, f, te_r, na_r: (jnp.minimum(i, na_r[0] - 1), 0)
    fidx = lambda i, f, na_r: jnp.where(i < na_r[0], f, nf - 1)
    ys = pl.pallas_call(
        _gmm_kernel,
        grid_spec=pltpu.PrefetchScalarGridSpec(
            num_scalar_prefetch=2, grid=(ntiles, nf),
            in_specs=[pl.BlockSpec((tile, d), row),
                      pl.BlockSpec((1, d, tf), lambda i, f, te_r, na_r: (te_r[i], 0, fidx(i, f, na_r))),
                      pl.BlockSpec((1, d, tf), lambda i, f, te_r, na_r: (te_r[i], 0, fidx(i, f, na_r))),
                      pl.BlockSpec((1, tf, d), lambda i, f, te_r, na_r: (te_r[i], fidx(i, f, na_r), 0))],
            out_specs=pl.BlockSpec((tile, d), lambda i, f, te_r, na_r: (i, 0))),
        out_shape=jax.ShapeDtypeStruct((ts, d), F32),
        compiler_params=_cparams("arbitrary", "arbitrary"),
        name="moe_experts",
    )(te, nact, xs, wg, wu, wd)

    tm_c = min(512, per_batch)
    nmod = gate.shape[0]
    tpb = per_batch // tm_c
    tok = lambda w: pl.BlockSpec((tm_c, w), lambda i, p: (i, 0))
    return pl.pallas_call(
        functools.partial(_combine_kernel, tm=tm_c),
        grid_spec=pltpu.PrefetchScalarGridSpec(
            num_scalar_prefetch=1, grid=(t // tm_c,),
            in_specs=[any_spec, tok(LANES), tok(d),
                      pl.BlockSpec((1, 1, d), lambda i, p: ((i // tpb) if nmod > 1 else 0, 0, 0)),
                      pl.BlockSpec((1, d), lambda i, p: (0, 0))],
            out_specs=tok(d),
            scratch_shapes=[pltpu.VMEM((2, tm_c, d), F32), pltpu.SemaphoreType.DMA(())]),
        out_shape=jax.ShapeDtypeStruct((t, d), F32),
        compiler_params=_cparams("arbitrary"),
        name="moe_combine",
    )(pos, ys, wts, h, gate, g_post.reshape(1, d))


def _rope_tables(n, rot_dim):
    n_freq = rot_dim // 4
    inv_freq = ROPE_THETA ** (-jnp.arange(n_freq, dtype=F32) / n_freq)
    rows = n // GRID_W
    row = jnp.repeat(jnp.arange(rows, dtype=F32), GRID_W)
    col = jnp.tile(jnp.arange(GRID_W, dtype=F32), rows)
    ang = jnp.concatenate([row[:, None] * inv_freq, col[:, None] * inv_freq], axis=-1)
    sin = jnp.sin(ang)
    return jnp.repeat(jnp.cos(ang), 2, axis=-1), jnp.stack([-sin, sin], axis=-1).reshape(n, rot_dim)


def _latent_tables(n):
    cw, sw = _rope_tables(n, WIN_HEAD_DIM)
    cm, sm = _rope_tables(n, MLA_ROPE)
    one = jnp.ones((n, MLA_NOPE), F32)
    zero_n = jnp.zeros((n, MLA_NOPE), F32)
    zero_p = jnp.zeros((n, MLA_HEAD_PAD - MLA_QK), F32)
    return (jnp.tile(cw, (1, WIN_HEADS)), jnp.tile(sw, (1, WIN_HEADS)),
            jnp.concatenate([one, cm, zero_p], axis=1), jnp.concatenate([zero_n, sm, zero_p], axis=1))


def _context_tables(lc):
    one_w = jnp.ones((lc, WIN_HEADS * WIN_HEAD_DIM), F32)
    cm = jnp.concatenate([jnp.ones((lc, MLA_QK), F32), jnp.zeros((lc, MLA_HEAD_PAD - MLA_QK), F32)], axis=1)
    return one_w, jnp.zeros_like(one_w), cm, jnp.zeros_like(cm)


def _prep_layer_weights(w_in, mla_w_uq, mla_w_ukv):
    d = w_in.shape[0]
    o = np.cumsum((0, 256, 256, 128, 128, 256, 128, 32, 512))
    a, q, k, v, cq, ckv, kr, sg = [w_in[:, o[i]:o[i + 1]] for i in range(8)]
    zl = jnp.zeros((d, MLA_NOPE), w_in.dtype)
    zr = jnp.zeros((d, MLA_HEAD_PAD - MLA_QK), w_in.dtype)
    krp = jnp.concatenate([zl, kr, zr], axis=1)
    zh = jnp.zeros((d, LANES - WIN_HEAD_DIM), w_in.dtype)
    vp = jnp.concatenate([v[:, :WIN_HEAD_DIM], zh, v[:, WIN_HEAD_DIM:], zh], axis=1)
    w_aug = jnp.concatenate([a, q, k, vp, cq, ckv, krp, sg], axis=1).astype(BF16)

    wq = mla_w_uq.reshape(MLA_Q_RANK, MLA_HEADS, MLA_QK)
    zq = jnp.zeros((MLA_Q_RANK, MLA_HEADS, MLA_HEAD_PAD - MLA_QK), wq.dtype)
    wq = jnp.concatenate([wq, zq], axis=-1).reshape(MLA_Q_RANK, MLA_HEADS * MLA_HEAD_PAD).astype(BF16)

    wkv = mla_w_ukv.reshape(MLA_KV_RANK, MLA_HEADS, MLA_NOPE + MLA_V)
    zk = jnp.zeros((MLA_KV_RANK, MLA_HEADS, MLA_HEAD_PAD - MLA_NOPE), wkv.dtype)
    wk = jnp.concatenate([wkv[..., :MLA_NOPE], zk], axis=-1).reshape(MLA_KV_RANK, MLA_HEADS * MLA_HEAD_PAD).astype(BF16)
    zv = jnp.zeros((MLA_KV_RANK, MLA_HEADS, MLA_HEAD_PAD - MLA_V), wkv.dtype)
    wv = jnp.concatenate([wkv[..., MLA_NOPE:], zv], axis=-1).reshape(MLA_KV_RANK, MLA_HEADS * MLA_HEAD_PAD).astype(BF16)
    return w_aug, wq, wk, wv


def _token_tile(per_batch, pref):
    return min(pref, per_batch)


def kernel(x, c, ctx, c_ctx, w_mod, b_mod, g_pre_mix, g_post_mix, g_pre_ffn, g_post_ffn, w_in, w_out, w_fourier, win_sink, mla_g_q, mla_g_kv, mla_w_uq, mla_w_ukv, sgu_w, sgu_b, ffn_w_gate, ffn_w_up, ffn_w_down, moe_w_router, moe_w_gate, moe_w_up, moe_w_down):
    b, n, d = x.shape
    lc = ctx.shape[1]
    depth = w_mod.shape[0]
    tx, tcx = b * n, b * lc
    tm_x = _token_tile(n, 512)
    tm_c = _token_tile(lc, 512)
    tab_x = _latent_tables(n)
    tab_c = _context_tables(lc)
    cond = jnp.concatenate([c, c_ctx[None, :], jnp.zeros((8 - (b + 1) % 8 if (b + 1) % 8 else 0, d), F32)], axis=0)

    hx = x.reshape(tx, d)
    hc = ctx.reshape(tcx, d)
    for layer in range(depth):
        last = layer == depth - 1
        mod = _modulation(cond, w_mod[layer], b_mod[layer])
        mx6 = mod[:b].reshape(b, 1, 6, d)
        mc6 = mod[b:b + 1].reshape(1, 1, 6, d)
        sh_a, sc_a, gt_a, sh_f, sc_f, gt_f = [mx6[:, :, i] for i in range(6)]
        csh_a, csc_a, cgt_a, csh_f, csc_f, cgt_f = [mc6[:, :, i] for i in range(6)]

        w_aug, wq, wk, wv = _prep_layer_weights(w_in[layer], mla_w_uq[layer], mla_w_ukv[layer])
        w_f = w_fourier[layer].astype(BF16)
        w_o = w_out[layer].astype(BF16)
        w_s = sgu_w[layer].astype(BF16)
        bias_full = jnp.repeat(sgu_b[layer].T, SGU_CH, axis=1)
        sink = win_sink[layer]
        gq, gkv = mla_g_q[layer], mla_g_kv[layer]

        px = _inproj(hx, g_pre_mix[layer], sh_a, sc_a, w_aug, tab_x, gq, gkv, wq, wk, wv, n, tm_x)
        pc = _inproj(hc, g_pre_mix[layer], csh_a, csc_a, w_aug, tab_c, gq, gkv, wq, wk, wv, lc, tm_c)
        a_x, qw_x, kw_x, vw_x, qm_x, km_x, vm_x, u_x, vn_x = [t.reshape(b, n, -1) for t in px]
        a_c, qw_c, kw_c, vw_c, qm_c, km_c, vm_c, u_c, vn_c = [t.reshape(b, lc, -1) for t in pc]

        four_x = _fourier_latent(a_x, w_f)
        win_x = _window_latent(qw_x, kw_x, vw_x, kw_c, vw_c, sink)
        mla_x = _mla_attention(qm_x, km_x, vm_x, km_c, vm_c)
        sgu_x = _sgu(u_x, vn_x, w_s, bias_full)
        parts_x = [t.reshape(tx, GROUP_W) for t in (four_x, win_x, mla_x, sgu_x)]

        i = layer // 2
        dense = layer % 2 == 0
        dff = ffn_w_gate.shape[-1]
        tf = dff // 2 if (dff // 2) % FFN_SUB == 0 else FFN_SUB
        w_router = None
        if not dense:
            w_r = jnp.concatenate([moe_w_router[i], jnp.zeros((d, LANES - N_EXPERTS), F32)], axis=1)
            w_r_hi = w_r.astype(BF16)
            w_router = jnp.stack([w_r_hi, (w_r - w_r_hi.astype(F32)).astype(BF16)])
        res = _outproj(parts_x, w_o, hx, gt_a, g_post_mix[layer], g_pre_ffn[layer], sh_f, sc_f, n, tm_x, w_router)
        if dense:
            wg, wu, wd = ffn_w_gate[i].astype(BF16), ffn_w_up[i].astype(BF16), ffn_w_down[i].astype(BF16)
            hx = _ffn_dense(res[1], wg, wu, wd, res[0], gt_f, g_post_ffn[layer], n, min(1024, n), tf)
        else:
            wg, wu, wd = moe_w_gate[i].astype(BF16), moe_w_up[i].astype(BF16), moe_w_down[i].astype(BF16)
            hx = _ffn_moe(res[1], res[2], wg, wu, wd, res[0], gt_f, g_post_ffn[layer], n, tf)

        if not last:
            four_c = _fourier_context(a_c, w_f)
            win_c = _window_context(qw_c, kw_c, vw_c, sink)
            mla_c = _mla_attention_small(qm_c, km_c, vm_c)
            sgu_c = _sgu(u_c, vn_c, w_s, bias_full)
            parts_c = [t.reshape(tcx, GROUP_W) for t in (four_c, win_c, mla_c, sgu_c)]
            res_c = _outproj(parts_c, w_o, hc, cgt_a, g_post_mix[layer], g_pre_ffn[layer], csh_f, csc_f, lc, tm_c, w_router)
            if dense:
                hc = _ffn_dense(res_c[1], wg, wu, wd, res_c[0], cgt_f, g_post_ffn[layer], lc, tm_c, tf)
            else:
                hc = _ffn_moe(res_c[1], res_c[2], wg, wu, wd, res_c[0], cgt_f, g_post_ffn[layer], lc, tf)
    return hx.reshape(b, n, d)
```

```python
import functools

import numpy as np
import jax
import jax.numpy as jnp
from jax import lax
from jax.experimental import pallas as pl
from jax.experimental.pallas import tpu as pltpu

F32 = jnp.float32
BF16 = jnp.bfloat16
HIGHEST = lax.Precision.HIGHEST

GRID_W = 64
GROUP_W = 256
FNET_GROUPS = 4
FNET_CH = 64
WIN_HEADS = 4
WIN_KV_HEADS = 2
WIN_HEAD_DIM = 64
WINDOW = 128
BLOCK = 128
MLA_HEADS = 4
MLA_NOPE = 64
MLA_ROPE = 32
MLA_QK = MLA_NOPE + MLA_ROPE
MLA_V = 64
MLA_Q_RANK = 256
MLA_KV_RANK = 128
SGU_GROUPS = 4
SGU_CHUNK = 128
SGU_CH = 64
N_EXPERTS = 8
ROPE_THETA = 10000.0
EPS = 1e-6
NEG_INF = -1e30
LOG2E = 1.4426950408889634

LANES = 128
MLA_HEAD_PAD = 128
FFT_N1 = 128
MOE_TILE = 1024
FFN_SUB = 256
VMEM_LIMIT = 56 * 1024 * 1024

_O_A, _O_Q, _O_K, _O_V, _O_CQ, _O_CKV, _O_KRP, _O_SG, _O_END = (0, 256, 512, 640, 896, 1152, 1280, 1408, 1920)


def _cparams(*sem):
    return pltpu.CompilerParams(dimension_semantics=sem, vmem_limit_bytes=VMEM_LIMIT)


def _full(shape):
    nd = len(shape)
    return pl.BlockSpec(shape, lambda *_: (0,) * nd)


def _dot(a, b):
    return jnp.dot(a, b, preferred_element_type=F32)


def _dot_nt(a, b):
    return lax.dot_general(a, b, (((1,), (1,)), ((), ())), preferred_element_type=F32)


def _dot2(hi, lo, x):
    return _dot(hi, x) + _dot(lo, x)


def _split_hi_lo(m):
    m = np.asarray(m, np.float32)
    hi = jnp.asarray(m, F32).astype(BF16)
    lo = (jnp.asarray(m, F32) - hi.astype(F32)).astype(BF16)
    return hi, lo


def _rms(x, g):
    return x * lax.rsqrt(jnp.mean(x * x, axis=-1, keepdims=True) + EPS) * g


def _mod_kernel(c_ref, w_ref, b_ref, o_ref):
    c = c_ref[...]
    s = c / (1.0 + jnp.exp(-c))
    o_ref[...] = jnp.dot(s, w_ref[...], preferred_element_type=F32, precision=HIGHEST) + b_ref[...]


def _modulation(cond, w_mod, b_mod):
    r, d = cond.shape
    n6 = w_mod.shape[1]
    tn = 512
    return pl.pallas_call(
        _mod_kernel,
        grid=(n6 // tn,),
        in_specs=[_full((r, d)), pl.BlockSpec((d, tn), lambda j: (0, j)), pl.BlockSpec((1, tn), lambda j: (0, j))],
        out_specs=pl.BlockSpec((r, tn), lambda j: (0, j)),
        out_shape=jax.ShapeDtypeStruct((r, n6), F32),
        compiler_params=_cparams("arbitrary"),
        name="adaln_mod",
    )(cond, w_mod, b_mod.reshape(1, n6))


def _inproj_kernel(h_ref, g_ref, sh_ref, sc_ref, w_ref, cw_ref, sw_ref, cm_ref, sm_ref,
                   gq_ref, gkv_ref, wq_ref, wk_ref, wv_ref,
                   a_o, qw_o, kw_o, vw_o, qm_o, km_o, vm_o, u_o, vn_o):
    x = h_ref[...]
    ax = (_rms(x, g_ref[...]) * (1.0 + sc_ref[0]) + sh_ref[0]).astype(BF16)

    def proj(lo, hi):
        return _dot(ax, w_ref[:, lo:hi])

    def rope(z, cos, sin_signed):
        w = z.shape[1]
        even = lax.broadcasted_iota(jnp.int32, z.shape, 1) % 2 == 0
        partner = jnp.where(even, pltpu.roll(z, w - 1, 1), pltpu.roll(z, 1, 1))
        return z * cos + partner * sin_signed

    a_o[...] = proj(_O_A, _O_Q).astype(BF16)
    cw = cw_ref[...]
    sw = sw_ref[...]
    qw_o[...] = (rope(proj(_O_Q, _O_K), cw, sw) * ((WIN_HEAD_DIM ** -0.5) * LOG2E)).astype(BF16)
    nk = WIN_KV_HEADS * WIN_HEAD_DIM
    kw_o[...] = rope(proj(_O_K, _O_V), cw[:, :nk], sw[:, :nk]).astype(BF16)
    vw = proj(_O_V, _O_CQ)
    vw_lane = lax.broadcasted_iota(jnp.int32, vw.shape, 1) % LANES
    vw_o[...] = jnp.where(vw_lane == WIN_HEAD_DIM, 1.0, vw).astype(BF16)

    cm = cm_ref[...]
    sm = sm_ref[...]
    cqn = _rms(proj(_O_CQ, _O_CKV), gq_ref[...]).astype(BF16)
    y1 = _dot(cqn, wq_ref[...])
    scale = (MLA_QK ** -0.5) * LOG2E
    for h in range(MLA_HEADS):
        sl = slice(h * MLA_HEAD_PAD, (h + 1) * MLA_HEAD_PAD)
        qm_o[:, sl] = (rope(y1[:, sl], cm, sm) * scale).astype(BF16)

    ckvn = _rms(proj(_O_CKV, _O_KRP), gkv_ref[...]).astype(BF16)
    krp = rope(proj(_O_KRP, _O_SG), cm, sm)
    kk = _dot(ckvn, wk_ref[...])
    for h in range(MLA_HEADS):
        sl = slice(h * MLA_HEAD_PAD, (h + 1) * MLA_HEAD_PAD)
        km_o[:, sl] = (kk[:, sl] + krp).astype(BF16)
    vv = _dot(ckvn, wv_ref[...])
    lane = lax.broadcasted_iota(jnp.int32, vv.shape, 1) % MLA_HEAD_PAD
    vm_o[...] = jnp.where(lane == MLA_V, 1.0, vv).astype(BF16)

    sg = proj(_O_SG, _O_END)
    gl = 0.5 * sg * (1.0 + lax.erf(sg * (2.0 ** -0.5)))
    u_o[...] = gl[:, :GROUP_W].astype(BF16)
    v = gl[:, GROUP_W:]
    mu = jnp.mean(v, axis=-1, keepdims=True)
    vc = v - mu
    var = jnp.mean(vc * vc, axis=-1, keepdims=True)
    vn_o[...] = (vc * lax.rsqrt(var + EPS)).astype(BF16)


def _inproj(h, g, shift, scale, w_aug, tables, gq, gkv, wq, wk, wv, per_batch, tm):
    t, d = h.shape
    nmod = shift.shape[0]
    tpb = per_batch // tm
    cw, sw, cm, sm = tables

    def mod_idx(i):
        return ((i // tpb) if nmod > 1 else 0, 0, 0)

    tok = lambda w: pl.BlockSpec((tm, w), lambda i: (i, 0))
    tab = lambda w: pl.BlockSpec((tm, w), lambda i: (i % tpb, 0))
    in_specs = [tok(d), _full((1, d)), pl.BlockSpec((1, 1, d), mod_idx), pl.BlockSpec((1, 1, d), mod_idx),
                _full(w_aug.shape), tab(256), tab(256), tab(LANES), tab(LANES),
                _full((1, MLA_Q_RANK)), _full((1, MLA_KV_RANK)),
                _full(wq.shape), _full(wk.shape), _full(wv.shape)]
    widths = (256, 256, 128, 256, 512, 512, 512, 256, 256)
    return pl.pallas_call(
        _inproj_kernel,
        grid=(t // tm,),
        in_specs=in_specs,
        out_specs=[tok(w) for w in widths],
        out_shape=[jax.ShapeDtypeStruct((t, w), BF16) for w in widths],
        compiler_params=_cparams("parallel"),
        name="inproj",
    )(h, g.reshape(1, d), shift, scale, w_aug, cw, sw, cm, sm, gq.reshape(1, -1), gkv.reshape(1, -1), wq, wk, wv)


def _fft1_kernel(x_ref, mh_ref, ml_ref, tr_ref, ti_ref, or_ref, oi_ref):
    x = x_ref[0]
    z = _dot2(mh_ref[...], ml_ref[...], x)
    ar, ai = z[:FFT_N1], z[FFT_N1:]
    tr, ti = tr_ref[...], ti_ref[...]
    or_ref[0] = (ar * tr - ai * ti).astype(BF16)
    oi_ref[0] = (ar * ti + ai * tr).astype(BF16)


def _fft2_kernel(r_ref, i_ref, cs_ref, wch_ref, wcl_ref, wf_ref, o_ref, g_scr, *, nj, n2):
    cs = cs_ref[...]
    gw = o_ref.shape[-1]
    for j in range(nj):
        zr = _dot(cs, r_ref[0, j])
        zi = _dot(cs, i_ref[0, j])
        c_r, s_r = zr[:n2] + zr[n2:2 * n2], zr[2 * n2:3 * n2] + zr[3 * n2:]
        c_i, s_i = zi[:n2] + zi[n2:2 * n2], zi[2 * n2:3 * n2] + zi[3 * n2:]
        g_scr[j * n2:(j + 1) * n2, :gw] = (c_r + s_i).astype(BF16)
        g_scr[j * n2:(j + 1) * n2, gw:] = (c_i - s_r).astype(BF16)
    g = g_scr[...]
    y = _dot(g, wch_ref[...]) + _dot(g, wcl_ref[...])
    o = _dot(y.astype(BF16), wf_ref[...]).astype(BF16)
    for j in range(nj):
        o_ref[0, j] = o[j * n2:(j + 1) * n2]


def _dft_small_kernel(a_ref, ch_ref, cl_ref, sh_ref, sl_ref, bch_ref, bcl_ref, bsh_ref, bsl_ref, wf_ref, o_ref):
    a = a_ref[0]
    gr = _dot2(ch_ref[...], cl_ref[...], a).astype(BF16)
    gi = (-_dot2(sh_ref[...], sl_ref[...], a)).astype(BF16)
    y = _dot(gr, bch_ref[...]) + _dot(gr, bcl_ref[...]) + _dot(gi, bsh_ref[...]) + _dot(gi, bsl_ref[...])
    o_ref[0] = _dot(y.astype(BF16), wf_ref[...]).astype(BF16)


def _cos_sin(n):
    k = np.arange(n)
    ang = 2.0 * np.pi * ((k[:, None] * k[None, :]) % n) / n
    return np.cos(ang), np.sin(ang)


def _channel_dft(norm):
    c, s = _cos_sin(FNET_CH)
    bc = np.kron(np.eye(FNET_GROUPS), c) * norm
    bs = np.kron(np.eye(FNET_GROUPS), s) * norm
    return _split_hi_lo(bc) + _split_hi_lo(bs)


def _fourier_latent(a, w_f):
    b, n, gw = a.shape
    n2 = n // FFT_N1
    c1, s1 = _cos_sin(FFT_N1)
    mh, ml = _split_hi_lo(np.concatenate([c1, -s1], axis=0))
    k1 = np.arange(FFT_N1)[:, None]
    t2 = np.arange(n2)[None, :]
    ang = 2.0 * np.pi * ((k1 * t2) % n) / n
    tr = jnp.repeat(jnp.asarray(np.cos(ang), F32), gw, axis=1)
    ti = jnp.repeat(jnp.asarray(-np.sin(ang), F32), gw, axis=1)
    cols = n2 * gw
    tc = min(cols, 4096)
    x2 = a.reshape(b, FFT_N1, cols)
    blk = pl.BlockSpec((1, FFT_N1, tc), lambda bi, ci: (bi, 0, ci))
    tblk = pl.BlockSpec((FFT_N1, tc), lambda bi, ci: (0, ci))
    o_r, o_i = pl.pallas_call(
        _fft1_kernel,
        grid=(b, cols // tc),
        in_specs=[blk, _full(mh.shape), _full(ml.shape), tblk, tblk],
        out_specs=[blk, blk],
        out_shape=[jax.ShapeDtypeStruct((b, FFT_N1, cols), BF16)] * 2,
        compiler_params=_cparams("parallel", "parallel"),
        name="fft_stage1",
    )(x2, mh, ml, tr, ti)

    c2, s2 = _cos_sin(n2)
    cs = jnp.concatenate(_split_hi_lo(c2) + _split_hi_lo(s2), axis=0)
    bch, bcl, bsh, bsl = _channel_dft((n * FNET_CH) ** -0.5)
    wch = jnp.concatenate([bch, bsh], axis=0)
    wcl = jnp.concatenate([bcl, bsl], axis=0)
    nj = 8
    r4 = o_r.reshape(b, FFT_N1, n2, gw)
    i4 = o_i.reshape(b, FFT_N1, n2, gw)
    blk4 = pl.BlockSpec((1, nj, n2, gw), lambda bi, ki: (bi, ki, 0, 0))
    out = pl.pallas_call(
        functools.partial(_fft2_kernel, nj=nj, n2=n2),
        grid=(b, FFT_N1 // nj),
        in_specs=[blk4, blk4, _full(cs.shape), _full(wch.shape), _full(wcl.shape), _full(w_f.shape)],
        out_specs=blk4,
        out_shape=jax.ShapeDtypeStruct((b, FFT_N1, n2, gw), BF16),
        scratch_shapes=[pltpu.VMEM((nj * n2, 2 * gw), BF16)],
        compiler_params=_cparams("parallel", "parallel"),
        name="fft_stage2",
    )(r4, i4, cs, wch, wcl, w_f)
    return jnp.swapaxes(out, 1, 2).reshape(b, n, gw)


def _fourier_context(a, w_f):
    b, n, gw = a.shape
    c, s = _cos_sin(n)
    consts = _split_hi_lo(c) + _split_hi_lo(s) + _channel_dft((n * FNET_CH) ** -0.5)
    blk = pl.BlockSpec((1, n, gw), lambda bi: (bi, 0, 0))
    return pl.pallas_call(
        _dft_small_kernel,
        grid=(b,),
        in_specs=[blk] + [_full(m.shape) for m in consts] + [_full(w_f.shape)],
        out_specs=blk,
        out_shape=jax.ShapeDtypeStruct((b, n, gw), BF16),
        compiler_params=_cparams("parallel"),
        name="dft_context",
    )(a, *consts, w_f)


def _win_heads(q_blk, k_win, v_win, bias, kc, vc, sink_ref):
    rows = lax.broadcasted_iota(jnp.int32, (WIN_KV_HEADS * BLOCK, 1), 0)
    outs = [None] * WIN_HEADS
    d = WIN_HEAD_DIM
    for hk in range(WIN_KV_HEADS):
        h0 = 2 * hk
        q2 = jnp.concatenate([q_blk[:, h0 * d:(h0 + 1) * d], q_blk[:, (h0 + 1) * d:(h0 + 2) * d]], axis=0)
        ksl = slice(hk * d, (hk + 1) * d)
        vsl = slice(hk * LANES, (hk + 1) * LANES)
        sink = jnp.where(rows < BLOCK, sink_ref[h0] * LOG2E, sink_ref[h0 + 1] * LOG2E)
        s_c = _dot_nt(q2, kc[:, ksl])
        m = jnp.maximum(jnp.max(s_c, axis=-1, keepdims=True), sink)
        if k_win is not None:
            s_w = _dot_nt(q2, k_win[:, ksl]) + bias
            m = jnp.maximum(m, jnp.max(s_w, axis=-1, keepdims=True))
        r = _dot(jnp.exp2(s_c - m).astype(BF16), vc[:, vsl])
        if k_win is not None:
            r = r + _dot(jnp.exp2(s_w - m).astype(BF16), v_win[:, vsl])
        o = r[:, :d] / (r[:, d:d + 1] + jnp.exp2(sink - m))
        outs[h0] = o[:BLOCK]
        outs[h0 + 1] = o[BLOCK:]
    return jnp.concatenate(outs, axis=1)


def _win_kernel(sink_ref, q_ref, kp_ref, k_ref, kn_ref, vp_ref, v_ref, vn_ref, kc_ref, vc_ref, o_ref, *, nblk):
    i = pl.program_id(1)
    kfull = jnp.concatenate([kp_ref[0], k_ref[0], kn_ref[0]], axis=0)
    vfull = jnp.concatenate([vp_ref[0], v_ref[0], vn_ref[0]], axis=0)
    kc = kc_ref[0]
    vc = vc_ref[0]
    shape = (WIN_KV_HEADS * BLOCK, 3 * BLOCK)
    qi = lax.broadcasted_iota(jnp.int32, shape, 0) % BLOCK
    kj = lax.broadcasted_iota(jnp.int32, shape, 1)
    band = jnp.abs(kj - BLOCK - qi) <= WINDOW
    has_prev = jnp.logical_or(kj >= BLOCK, i > 0)
    has_next = jnp.logical_or(kj < 2 * BLOCK, i < pl.num_programs(1) - 1)
    bias_mid = jnp.where(band, 0.0, NEG_INF)
    for j in range(nblk):
        ok = band
        if j == 0:
            ok = ok & has_prev
        if j == nblk - 1:
            ok = ok & has_next
        bias = bias_mid if ok is band else jnp.where(ok, 0.0, NEG_INF)
        o = _win_heads(q_ref[0, j * BLOCK:(j + 1) * BLOCK, :], kfull[j * BLOCK:(j + 3) * BLOCK],
                       vfull[j * BLOCK:(j + 3) * BLOCK], bias, kc, vc, sink_ref)
        o_ref[0, j * BLOCK:(j + 1) * BLOCK, :] = o.astype(BF16)


def _winctx_kernel(sink_ref, q_ref, kc_ref, vc_ref, o_ref, *, nblk):
    kc = kc_ref[0]
    vc = vc_ref[0]
    for j in range(nblk):
        o = _win_heads(q_ref[0, j * BLOCK:(j + 1) * BLOCK, :], None, None, None, kc, vc, sink_ref)
        o_ref[0, j * BLOCK:(j + 1) * BLOCK, :] = o.astype(BF16)


def _window_latent(q, k, v, kc, vc, sink):
    b, n, _ = q.shape
    lc = kc.shape[1]
    nblk = min(8, n // BLOCK)
    tq = nblk * BLOCK
    nb = n // BLOCK
    kw = WIN_KV_HEADS * WIN_HEAD_DIM
    vw = WIN_KV_HEADS * LANES
    cur = lambda w: pl.BlockSpec((1, tq, w), lambda bi, i: (bi, i, 0))
    prev = lambda w: pl.BlockSpec((1, BLOCK, w), lambda bi, i: (bi, jnp.maximum(i * nblk - 1, 0), 0))
    nxt = lambda w: pl.BlockSpec((1, BLOCK, w), lambda bi, i: (bi, jnp.minimum((i + 1) * nblk, nb - 1), 0))
    ctx = lambda w: pl.BlockSpec((1, lc, w), lambda bi, i: (bi, 0, 0))
    smem = pl.BlockSpec(memory_space=pltpu.SMEM)
    return pl.pallas_call(
        functools.partial(_win_kernel, nblk=nblk),
        grid=(b, n // tq),
        in_specs=[smem, cur(256), prev(kw), cur(kw), nxt(kw), prev(vw), cur(vw), nxt(vw), ctx(kw), ctx(vw)],
        out_specs=cur(256),
        out_shape=jax.ShapeDtypeStruct((b, n, 256), BF16),
        compiler_params=_cparams("parallel", "arbitrary"),
        name="window_attn",
    )(sink, q, k, k, k, v, v, v, kc, vc)


def _window_context(q, kc, vc, sink):
    b, lc, _ = q.shape
    kvw = WIN_KV_HEADS * WIN_HEAD_DIM
    blk = lambda w: pl.BlockSpec((1, lc, w), lambda bi: (bi, 0, 0))
    smem = pl.BlockSpec(memory_space=pltpu.SMEM)
    return pl.pallas_call(
        functools.partial(_winctx_kernel, nblk=lc // BLOCK),
        grid=(b,),
        in_specs=[smem, blk(256), blk(kvw), blk(WIN_KV_HEADS * LANES)],
        out_specs=blk(256),
        out_shape=jax.ShapeDtypeStruct((b, lc, 256), BF16),
        compiler_params=_cparams("parallel"),
        name="window_attn_ctx",
    )(sink, q, kc, vc)


def _mla_reduce(state, s, v):
    m, acc = state
    m_new = jnp.maximum(m, jnp.max(s, axis=-1, keepdims=True))
    alpha = jnp.exp2(m - m_new)
    p = jnp.exp2(s - m_new).astype(BF16)
    return m_new, alpha * acc + _dot(p, v)


def _mla_finish(acc):
    return (acc[:, :MLA_V] / acc[:, MLA_V:MLA_V + 1]).astype(BF16)


def _mla_kernel(q_ref, k_ref, v_ref, kc_ref, vc_ref, o_ref, s_scr, sc_scr, *, tk, nk):
    tq = q_ref.shape[1]
    hsl = [slice(h * MLA_HEAD_PAD, (h + 1) * MLA_HEAD_PAD) for h in range(2)]
    qs = [q_ref[0, :, sl] for sl in hsl]

    def scores(c, slot):
        off = pl.multiple_of(c * tk, tk)
        for h in range(2):
            s_scr[slot, h] = _dot_nt(qs[h], k_ref[0, pl.ds(off, tk), hsl[h]])

    def reduce(c, slot, states):
        off = pl.multiple_of(c * tk, tk)
        return tuple(_mla_reduce(states[h], s_scr[slot, h], v_ref[0, pl.ds(off, tk), hsl[h]]) for h in range(2))

    scores(0, 0)

    def body(i, states):
        scores(2 * i + 1, 1)
        states = reduce(2 * i, 0, states)
        scores(2 * i + 2, 0)
        return reduce(2 * i + 1, 1, states)

    init = tuple((jnp.full((tq, 1), NEG_INF, F32), jnp.zeros((tq, MLA_HEAD_PAD), F32)) for _ in range(2))
    states = init
    for pair in range(nk // 2 - 1):
        states = body(pair, states)
    scores(nk - 1, 1)
    states = reduce(nk - 2, 0, states)
    for h in range(2):
        sc_scr[h] = _dot_nt(qs[h], kc_ref[0, :, hsl[h]])
    states = reduce(nk - 1, 1, states)
    for h in range(2):
        _, acc = _mla_reduce(states[h], sc_scr[h], vc_ref[0, :, hsl[h]])
        o_ref[0, :, h * MLA_V:(h + 1) * MLA_V] = _mla_finish(acc)


def _mla_small_kernel(q_ref, k_ref, v_ref, o_ref):
    tq = q_ref.shape[1]
    for h in range(2):
        sl = slice(h * MLA_HEAD_PAD, (h + 1) * MLA_HEAD_PAD)
        init = (jnp.full((tq, 1), NEG_INF, F32), jnp.zeros((tq, MLA_HEAD_PAD), F32))
        _, acc = _mla_reduce(init, _dot_nt(q_ref[0, :, sl], k_ref[0, :, sl]), v_ref[0, :, sl])
        o_ref[0, :, h * MLA_V:(h + 1) * MLA_V] = _mla_finish(acc)


def _mla_attention(q, k, v, kc, vc):
    b, nq, _ = q.shape
    nkeys = k.shape[1]
    lc = kc.shape[1]
    tq = min(512, nq)
    tk = min(1024, nkeys // 2)
    assert nkeys % (2 * tk) == 0 and nq % tq == 0
    w2 = 2 * MLA_HEAD_PAD
    res = lambda n: pl.BlockSpec((1, n, w2), lambda bi, h, i: (bi, 0, h))
    return pl.pallas_call(
        functools.partial(_mla_kernel, tk=tk, nk=nkeys // tk),
        grid=(b, MLA_HEADS // 2, nq // tq),
        in_specs=[pl.BlockSpec((1, tq, w2), lambda bi, h, i: (bi, i, h)), res(nkeys), res(nkeys), res(lc), res(lc)],
        out_specs=pl.BlockSpec((1, tq, 2 * MLA_V), lambda bi, h, i: (bi, i, h)),
        out_shape=jax.ShapeDtypeStruct((b, nq, MLA_HEADS * MLA_V), BF16),
        scratch_shapes=[pltpu.VMEM((2, 2, tq, tk), F32), pltpu.VMEM((2, tq, lc), F32)],
        compiler_params=_cparams("parallel", "parallel", "arbitrary"),
        name="mla_attn",
    )(q, k, v, kc, vc)


def _mla_attention_small(q, k, v):
    b, n, _ = q.shape
    w2 = 2 * MLA_HEAD_PAD
    blk = pl.BlockSpec((1, n, w2), lambda bi, h: (bi, 0, h))
    return pl.pallas_call(
        _mla_small_kernel,
        grid=(b, MLA_HEADS // 2),
        in_specs=[blk, blk, blk],
        out_specs=pl.BlockSpec((1, n, 2 * MLA_V), lambda bi, h: (bi, 0, h)),
        out_shape=jax.ShapeDtypeStruct((b, n, MLA_HEADS * MLA_V), BF16),
        compiler_params=_cparams("parallel", "parallel"),
        name="mla_attn_ctx",
    )(q, k, v)


def _sgu_kernel(u_ref, v_ref, w_ref, b_ref, o_ref, *, nchunk):
    lane = lax.broadcasted_iota(jnp.int32, (SGU_CHUNK, GROUP_W), 1) // SGU_CH
    bias = b_ref[...]
    for c in range(nchunk):
        sl = slice(c * SGU_CHUNK, (c + 1) * SGU_CHUNK)
        v = v_ref[0, sl, :]
        s = bias
        for g in range(SGU_GROUPS):
            s = s + _dot(w_ref[g], jnp.where(lane == g, v, jnp.zeros_like(v)))
        o_ref[0, sl, :] = (u_ref[0, sl, :].astype(F32) * s).astype(BF16)


def _sgu(u, vn, w_s, bias_full):
    b, n, gw = u.shape
    nchunk = min(8, n // SGU_CHUNK)
    tn = nchunk * SGU_CHUNK
    blk = pl.BlockSpec((1, tn, gw), lambda bi, i: (bi, i, 0))
    return pl.pallas_call(
        functools.partial(_sgu_kernel, nchunk=nchunk),
        grid=(b, n // tn),
        in_specs=[blk, blk, _full(w_s.shape), _full(bias_full.shape)],
        out_specs=blk,
        out_shape=jax.ShapeDtypeStruct((b, n, gw), BF16),
        compiler_params=_cparams("parallel", "parallel"),
        name="sgu",
    )(u, vn, w_s, bias_full)


def _outproj_kernel(*refs, with_router):
    if with_router:
        (p0, p1, p2, p3, w_ref, h_ref, gate_ref, gpost_ref, gpre_ref, sh_ref, sc_ref, wr_ref,
         hn_o, fx_o, comb_o) = refs
    else:
        p0, p1, p2, p3, w_ref, h_ref, gate_ref, gpost_ref, gpre_ref, sh_ref, sc_ref, hn_o, fx_o = refs
    mx = None
    for idx, p in enumerate((p0, p1, p2, p3)):
        t = _dot(p[...], w_ref[idx * GROUP_W:(idx + 1) * GROUP_W, :])
        mx = t if mx is None else mx + t
    hn = h_ref[...] + gate_ref[0] * _rms(mx, gpost_ref[...])
    hn_o[...] = hn
    fx = _rms(hn, gpre_ref[...]) * (1.0 + sc_ref[0]) + sh_ref[0]
    fx_o[...] = fx.astype(fx_o.dtype)
    if with_router:
        fx_hi = fx.astype(BF16)
        fx_lo = (fx - fx_hi.astype(F32)).astype(BF16)
        logits = _dot(fx_hi, wr_ref[0]) + (_dot(fx_lo, wr_ref[0]) + _dot(fx_hi, wr_ref[1]))
        lane = lax.broadcasted_iota(jnp.int32, logits.shape, 1).astype(F32)
        lg = jnp.where(lane < N_EXPERTS, logits, -jnp.inf)
        m1 = jnp.max(lg, axis=-1, keepdims=True)
        i1 = jnp.min(jnp.where(lg == m1, lane, float(LANES)), axis=-1, keepdims=True)
        lg2 = jnp.where(lane == i1, -jnp.inf, lg)
        m2 = jnp.max(lg2, axis=-1, keepdims=True)
        i2 = jnp.min(jnp.where(lg2 == m2, lane, float(LANES)), axis=-1, keepdims=True)
        e2 = jnp.exp(m2 - m1)
        den = 1.0 + e2
        flag = (lane == i1 + N_EXPERTS) | (lane == i2 + N_EXPERTS)
        comb_o[...] = (jnp.where(lane == i1, 1.0 / den, 0.0) + jnp.where(lane == i2, e2 / den, 0.0)
                       + jnp.where(flag, 1.0, 0.0))


def _outproj(parts, w_out, h, gate, g_post, g_pre, shift, scale, per_batch, tm, w_router=None):
    t, d = h.shape
    nmod = gate.shape[0]
    tpb = per_batch // tm
    with_router = w_router is not None

    def mod_idx(i):
        return ((i // tpb) if nmod > 1 else 0, 0, 0)

    tok = lambda w: pl.BlockSpec((tm, w), lambda i: (i, 0))
    mod = pl.BlockSpec((1, 1, d), mod_idx)
    in_specs = [tok(GROUP_W)] * 4 + [_full(w_out.shape), tok(d), mod, _full((1, d)), _full((1, d)), mod, mod]
    args = list(parts) + [w_out, h, gate, g_post.reshape(1, d), g_pre.reshape(1, d), shift, scale]
    out_specs = [tok(d), tok(d)]
    out_shape = [jax.ShapeDtypeStruct((t, d), F32), jax.ShapeDtypeStruct((t, d), F32 if with_router else BF16)]
    if with_router:
        in_specs.append(_full(w_router.shape))
        args.append(w_router)
        out_specs.append(tok(LANES))
        out_shape.append(jax.ShapeDtypeStruct((t, LANES), F32))
    return pl.pallas_call(
        functools.partial(_outproj_kernel, with_router=with_router),
        grid=(t // tm,),
        in_specs=in_specs,
        out_specs=out_specs,
        out_shape=out_shape,
        compiler_params=_cparams("parallel"),
        name="outproj",
    )(*args)


def _swiglu_partial(x, wg, wu, wd, tf):
    y = None
    for lo in range(0, tf, FFN_SUB):
        hi = lo + FFN_SUB
        g = _dot(x, wg(lo, hi))
        u = _dot(x, wu(lo, hi))
        hid = (g / (1.0 + jnp.exp(-g)) * u).astype(BF16)
        t = _dot(hid, wd(lo, hi))
        y = t if y is None else y + t
    return y


def _ffn_kernel(x_ref, wg_ref, wu_ref, wd_ref, h_ref, gate_ref, gpost_ref, o_ref):
    f = pl.program_id(1)
    last = pl.num_programs(1) - 1
    y = _swiglu_partial(x_ref[...], lambda lo, hi: wg_ref[:, lo:hi], lambda lo, hi: wu_ref[:, lo:hi],
                        lambda lo, hi: wd_ref[lo:hi, :], wd_ref.shape[0])

    @pl.when(f == 0)
    def _():
        o_ref[...] = y

    @pl.when((f > 0) & (f < last))
    def _():
        o_ref[...] += y

    @pl.when(f == last)
    def _():
        o_ref[...] = h_ref[...] + gate_ref[0] * _rms(o_ref[...] + y, gpost_ref[...])


def _ffn_dense(fx, wg, wu, wd, h, gate, g_post, per_batch, tm, tf):
    t, d = h.shape
    dff = wg.shape[1]
    nmod = gate.shape[0]
    tpb = per_batch // tm
    assert dff // tf >= 2 and tf % FFN_SUB == 0

    def mod_idx(i, f):
        return ((i // tpb) if nmod > 1 else 0, 0, 0)

    tok = pl.BlockSpec((tm, d), lambda i, f: (i, 0))
    return pl.pallas_call(
        _ffn_kernel,
        grid=(t // tm, dff // tf),
        in_specs=[tok, pl.BlockSpec((d, tf), lambda i, f: (0, f)), pl.BlockSpec((d, tf), lambda i, f: (0, f)),
                  pl.BlockSpec((tf, d), lambda i, f: (f, 0)), tok, pl.BlockSpec((1, 1, d), mod_idx),
                  pl.BlockSpec((1, d), lambda i, f: (0, 0))],
        out_specs=tok,
        out_shape=jax.ShapeDtypeStruct((t, d), F32),
        compiler_params=_cparams("parallel", "arbitrary"),
        name="ffn_dense",
    )(fx, wg, wu, wd, h, gate, g_post.reshape(1, d))


def _route(comb, tile):
    t = comb.shape[0]
    sel = (comb[:, N_EXPERTS:2 * N_EXPERTS] > 0.5).astype(jnp.int32)
    rank = jnp.cumsum(sel, axis=0) - sel
    padded = (jnp.sum(sel, axis=0) + tile - 1) // tile * tile
    ends = jnp.cumsum(padded)
    pos_e = (ends - padded)[None, :] + rank
    e_id = jnp.arange(N_EXPERTS, dtype=jnp.int32)[None, :]
    e_lo = jnp.min(jnp.where(sel > 0, e_id, N_EXPERTS - 1), axis=1)
    e_hi = jnp.max(sel * e_id, axis=1)
    take = lambda a, e: jnp.take_along_axis(a, e[:, None], axis=1)[:, 0]
    pos = jnp.stack([take(pos_e, e_lo), take(pos_e, e_hi)], axis=1).reshape(2 * t).astype(jnp.int32)
    w = comb[:, :N_EXPERTS]
    wts = jnp.pad(jnp.stack([take(w, e_lo), take(w, e_hi)], axis=1), ((0, 0), (0, LANES - 2)))
    ntiles = -(-2 * t // tile) + N_EXPERTS
    nact = (ends[-1] // tile).astype(jnp.int32)
    tile_id = jnp.arange(ntiles, dtype=jnp.int32)
    te = jnp.minimum(jnp.searchsorted(ends, tile_id * tile, side="right"), N_EXPERTS - 1).astype(jnp.int32)
    te = jnp.where(tile_id < nact, te, te[nact - 1])
    return pos, wts, te, nact.reshape(1), ntiles


def _dispatch_kernel(pos_ref, x_ref, xs_in_ref, xs_ref, sem, *, tm):
    del xs_in_ref
    base = pl.program_id(0) * tm

    def row_copy(t, p):
        return pltpu.make_async_copy(x_ref.at[pl.ds(t, 1)], xs_ref.at[pl.ds(p, 1)], sem)

    def issue(t, carry):
        row_copy(t, pos_ref[2 * (base + t)]).start(priority=0)
        row_copy(t, pos_ref[2 * (base + t) + 1]).start(priority=1)
        return carry

    def drain(t, carry):
        row_copy(0, 0).wait()
        row_copy(0, 0).wait()
        return carry

    lax.fori_loop(0, tm, issue, 0, unroll=8)
    lax.fori_loop(0, tm, drain, 0, unroll=8)


def _gmm_kernel(te_ref, nact_ref, x_ref, wg_ref, wu_ref, wd_ref, o_ref):
    del te_ref
    f = pl.program_id(1)
    active = pl.program_id(0) < nact_ref[0]

    @pl.when(jnp.logical_not(active) & (f == 0))
    def _():
        o_ref[...] = jnp.zeros_like(o_ref)

    @pl.when(active)
    def _():
        y = _swiglu_partial(x_ref[...].astype(BF16), lambda lo, hi: wg_ref[0, :, lo:hi],
                            lambda lo, hi: wu_ref[0, :, lo:hi], lambda lo, hi: wd_ref[0, lo:hi, :], wd_ref.shape[1])

        @pl.when(f == 0)
        def _():
            o_ref[...] = y

        @pl.when(f > 0)
        def _():
            o_ref[...] += y


def _combine_kernel(pos_ref, ys_ref, w_ref, h_ref, gate_ref, gpost_ref, o_ref, buf, sem, *, tm):
    base = pl.program_id(0) * tm

    def row_copy(choice, t, p):
        return pltpu.make_async_copy(ys_ref.at[pl.ds(p, 1)], buf.at[choice, pl.ds(t, 1)], sem)

    def issue(t, carry):
        row_copy(0, t, pos_ref[2 * (base + t)]).start(priority=0)
        row_copy(1, t, pos_ref[2 * (base + t) + 1]).start(priority=1)
        return carry

    def drain(t, carry):
        row_copy(0, 0, 0).wait()
        row_copy(1, 0, 0).wait()
        return carry

    lax.fori_loop(0, tm, issue, 0, unroll=8)
    lax.fori_loop(0, tm, drain, 0, unroll=8)
    w = w_ref[...]
    y = w[:, 0:1] * buf[0] + w[:, 1:2] * buf[1]
    o_ref[...] = h_ref[...] + gate_ref[0] * _rms(y, gpost_ref[...])


def _ffn_moe(fx, comb, wg, wu, wd, h, gate, g_post, per_batch, tf):
    t, d = h.shape
    dff = wg.shape[2]
    nf = dff // tf
    tile = MOE_TILE
    pos, wts, te, nact, ntiles = _route(comb, tile)
    ts = ntiles * tile
    any_spec = pl.BlockSpec(memory_space=pl.ANY)

    tm_d = min(512, per_batch)
    xs = pl.pallas_call(
        functools.partial(_dispatch_kernel, tm=tm_d),
        grid_spec=pltpu.PrefetchScalarGridSpec(
            num_scalar_prefetch=1, grid=(t // tm_d,),
            in_specs=[pl.BlockSpec((tm_d, d), lambda i, p: (i, 0)), any_spec],
            out_specs=any_spec,
            scratch_shapes=[pltpu.SemaphoreType.DMA(())]),
        out_shape=jax.ShapeDtypeStruct((ts, d), F32),
        input_output_aliases={2: 0},
        compiler_params=_cparams("arbitrary"),
        name="moe_dispatch",
    )(pos, fx, jnp.zeros((ts, d), F32))

    row = lambda i, f, te_r, na_r: (jnp.minimum(i, na_r[0] - 1), 0)
    fidx = lambda i, f, na_r: jnp.where(i < na_r[0], f, nf - 1)
    ys = pl.pallas_call(
        _gmm_kernel,
        grid_spec=pltpu.PrefetchScalarGridSpec(
            num_scalar_prefetch=2, grid=(ntiles, nf),
            in_specs=[pl.BlockSpec((tile, d), row),
                      pl.BlockSpec((1, d, tf), lambda i, f, te_r, na_r: (te_r[i], 0, fidx(i, f, na_r))),
                      pl.BlockSpec((1, d, tf), lambda i, f, te_r, na_r: (te_r[i], 0, fidx(i, f, na_r))),
                      pl.BlockSpec((1, tf, d), lambda i, f, te_r, na_r: (te_r[i], fidx(i, f, na_r), 0))],
            out_specs=pl.BlockSpec((tile, d), lambda i, f, te_r, na_r: (i, 0))),
        out_shape=jax.ShapeDtypeStruct((ts, d), F32),
        compiler_params=_cparams("arbitrary", "arbitrary"),
        name="moe_experts",
    )(te, nact, xs, wg, wu, wd)

    tm_c = min(512, per_batch)
    nmod = gate.shape[0]
    tpb = per_batch // tm_c
    tok = lambda w: pl.BlockSpec((tm_c, w), lambda i, p: (i, 0))
    return pl.pallas_call(
        functools.partial(_combine_kernel, tm=tm_c),
        grid_spec=pltpu.PrefetchScalarGridSpec(
            num_scalar_prefetch=1, grid=(t // tm_c,),
            in_specs=[any_spec, tok(LANES), tok(d),
                      pl.BlockSpec((1, 1, d), lambda i, p: ((i // tpb) if nmod > 1 else 0, 0, 0)),
                      pl.BlockSpec((1, d), lambda i, p: (0, 0))],
            out_specs=tok(d),
            scratch_shapes=[pltpu.VMEM((2, tm_c, d), F32), pltpu.SemaphoreType.DMA(())]),
        out_shape=jax.ShapeDtypeStruct((t, d), F32),
        compiler_params=_cparams("arbitrary"),
        name="moe_combine",
    )(pos, ys, wts, h, gate, g_post.reshape(1, d))


def _rope_tables(n, rot_dim):
    n_freq = rot_dim // 4
    inv_freq = ROPE_THETA ** (-jnp.arange(n_freq, dtype=F32) / n_freq)
    rows = n // GRID_W
    row = jnp.repeat(jnp.arange(rows, dtype=F32), GRID_W)
    col = jnp.tile(jnp.arange(GRID_W, dtype=F32), rows)
    ang = jnp.concatenate([row[:, None] * inv_freq, col[:, None] * inv_freq], axis=-1)
    sin = jnp.sin(ang)
    return jnp.repeat(jnp.cos(ang), 2, axis=-1), jnp.stack([-sin, sin], axis=-1).reshape(n, rot_dim)


def _latent_tables(n):
    cw, sw = _rope_tables(n, WIN_HEAD_DIM)
    cm, sm = _rope_tables(n, MLA_ROPE)
    one = jnp.ones((n, MLA_NOPE), F32)
    zero_n = jnp.zeros((n, MLA_NOPE), F32)
    zero_p = jnp.zeros((n, MLA_HEAD_PAD - MLA_QK), F32)
    return (jnp.tile(cw, (1, WIN_HEADS)), jnp.tile(sw, (1, WIN_HEADS)),
            jnp.concatenate([one, cm, zero_p], axis=1), jnp.concatenate([zero_n, sm, zero_p], axis=1))


def _context_tables(lc):
    one_w = jnp.ones((lc, WIN_HEADS * WIN_HEAD_DIM), F32)
    cm = jnp.concatenate([jnp.ones((lc, MLA_QK), F32), jnp.zeros((lc, MLA_HEAD_PAD - MLA_QK), F32)], axis=1)
    return one_w, jnp.zeros_like(one_w), cm, jnp.zeros_like(cm)


def _prep_layer_weights(w_in, mla_w_uq, mla_w_ukv):
    d = w_in.shape[0]
    o = np.cumsum((0, 256, 256, 128, 128, 256, 128, 32, 512))
    a, q, k, v, cq, ckv, kr, sg = [w_in[:, o[i]:o[i + 1]] for i in range(8)]
    zl = jnp.zeros((d, MLA_NOPE), w_in.dtype)
    zr = jnp.zeros((d, MLA_HEAD_PAD - MLA_QK), w_in.dtype)
    krp = jnp.concatenate([zl, kr, zr], axis=1)
    zh = jnp.zeros((d, LANES - WIN_HEAD_DIM), w_in.dtype)
    vp = jnp.concatenate([v[:, :WIN_HEAD_DIM], zh, v[:, WIN_HEAD_DIM:], zh], axis=1)
    w_aug = jnp.concatenate([a, q, k, vp, cq, ckv, krp, sg], axis=1).astype(BF16)

    wq = mla_w_uq.reshape(MLA_Q_RANK, MLA_HEADS, MLA_QK)
    zq = jnp.zeros((MLA_Q_RANK, MLA_HEADS, MLA_HEAD_PAD - MLA_QK), wq.dtype)
    wq = jnp.concatenate([wq, zq], axis=-1).reshape(MLA_Q_RANK, MLA_HEADS * MLA_HEAD_PAD).astype(BF16)

    wkv = mla_w_ukv.reshape(MLA_KV_RANK, MLA_HEADS, MLA_NOPE + MLA_V)
    zk = jnp.zeros((MLA_KV_RANK, MLA_HEADS, MLA_HEAD_PAD - MLA_NOPE), wkv.dtype)
    wk = jnp.concatenate([wkv[..., :MLA_NOPE], zk], axis=-1).reshape(MLA_KV_RANK, MLA_HEADS * MLA_HEAD_PAD).astype(BF16)
    zv = jnp.zeros((MLA_KV_RANK, MLA_HEADS, MLA_HEAD_PAD - MLA_V), wkv.dtype)
    wv = jnp.concatenate([wkv[..., MLA_NOPE:], zv], axis=-1).reshape(MLA_KV_RANK, MLA_HEADS * MLA_HEAD_PAD).astype(BF16)
    return w_aug, wq, wk, wv


def _token_tile(per_batch, pref):
    return min(pref, per_batch)


def kernel(x, c, ctx, c_ctx, w_mod, b_mod, g_pre_mix, g_post_mix, g_pre_ffn, g_post_ffn, w_in, w_out, w_fourier, win_sink, mla_g_q, mla_g_kv, mla_w_uq, mla_w_ukv, sgu_w, sgu_b, ffn_w_gate, ffn_w_up, ffn_w_down, moe_w_router, moe_w_gate, moe_w_up, moe_w_down):
    b, n, d = x.shape
    lc = ctx.shape[1]
    depth = w_mod.shape[0]
    tx, tcx = b * n, b * lc
    tm_x = _token_tile(n, 512)
    tm_c = _token_tile(lc, 512)
    tab_x = _latent_tables(n)
    tab_c = _context_tables(lc)
    cond = jnp.concatenate([c, c_ctx[None, :], jnp.zeros((8 - (b + 1) % 8 if (b + 1) % 8 else 0, d), F32)], axis=0)

    hx = x.reshape(tx, d)
    hc = ctx.reshape(tcx, d)
    for layer in range(depth):
        last = layer == depth - 1
        mod = _modulation(cond, w_mod[layer], b_mod[layer])
        mx6 = mod[:b].reshape(b, 1, 6, d)
        mc6 = mod[b:b + 1].reshape(1, 1, 6, d)
        sh_a, sc_a, gt_a, sh_f, sc_f, gt_f = [mx6[:, :, i] for i in range(6)]
        csh_a, csc_a, cgt_a, csh_f, csc_f, cgt_f = [mc6[:, :, i] for i in range(6)]

        w_aug, wq, wk, wv = _prep_layer_weights(w_in[layer], mla_w_uq[layer], mla_w_ukv[layer])
        w_f = w_fourier[layer].astype(BF16)
        w_o = w_out[layer].astype(BF16)
        w_s = sgu_w[layer].astype(BF16)
        bias_full = jnp.repeat(sgu_b[layer].T, SGU_CH, axis=1)
        sink = win_sink[layer]
        gq, gkv = mla_g_q[layer], mla_g_kv[layer]

        px = _inproj(hx, g_pre_mix[layer], sh_a, sc_a, w_aug, tab_x, gq, gkv, wq, wk, wv, n, tm_x)
        pc = _inproj(hc, g_pre_mix[layer], csh_a, csc_a, w_aug, tab_c, gq, gkv, wq, wk, wv, lc, tm_c)
        a_x, qw_x, kw_x, vw_x, qm_x, km_x, vm_x, u_x, vn_x = [t.reshape(b, n, -1) for t in px]
        a_c, qw_c, kw_c, vw_c, qm_c, km_c, vm_c, u_c, vn_c = [t.reshape(b, lc, -1) for t in pc]

        four_x = _fourier_latent(a_x, w_f)
        win_x = _window_latent(qw_x, kw_x, vw_x, kw_c, vw_c, sink)
        mla_x = _mla_attention(qm_x, km_x, vm_x, km_c, vm_c)
        sgu_x = _sgu(u_x, vn_x, w_s, bias_full)
        parts_x = [t.reshape(tx, GROUP_W) for t in (four_x, win_x, mla_x, sgu_x)]

        i = layer // 2
        dense = layer % 2 == 0
        dff = ffn_w_gate.shape[-1]
        tf = dff // 2 if (dff // 2) % FFN_SUB == 0 else FFN_SUB
        w_router = None
        if not dense:
            w_r = jnp.concatenate([moe_w_router[i], jnp.zeros((d, LANES - N_EXPERTS), F32)], axis=1)
            w_r_hi = w_r.astype(BF16)
            w_router = jnp.stack([w_r_hi, (w_r - w_r_hi.astype(F32)).astype(BF16)])
        res = _outproj(parts_x, w_o, hx, gt_a, g_post_mix[layer], g_pre_ffn[layer], sh_f, sc_f, n, tm_x, w_router)
        if dense:
            wg, wu, wd = ffn_w_gate[i].astype(BF16), ffn_w_up[i].astype(BF16), ffn_w_down[i].astype(BF16)
            hx = _ffn_dense(res[1], wg, wu, wd, res[0], gt_f, g_post_ffn[layer], n, min(1024, n), tf)
        else:
            wg, wu, wd = moe_w_gate[i].astype(BF16), moe_w_up[i].astype(BF16), moe_w_down[i].astype(BF16)
            hx = _ffn_moe(res[1], res[2], wg, wu, wd, res[0], gt_f, g_post_ffn[layer], n, tf)

        if not last:
            four_c = _fourier_context(a_c, w_f)
            win_c = _window_context(qw_c, kw_c, vw_c, sink)
            mla_c = _mla_attention_small(qm_c, km_c, vm_c)
            sgu_c = _sgu(u_c, vn_c, w_s, bias_full)
            parts_c = [t.reshape(tcx, GROUP_W) for t in (four_c, win_c, mla_c, sgu_c)]
            res_c = _outproj(parts_c, w_o, hc, cgt_a, g_post_mix[layer], g_pre_ffn[layer], csh_f, csc_f, lc, tm_c, w_router)
            if dense:
                hc = _ffn_dense(res_c[1], wg, wu, wd, res_c[0], cgt_f, g_post_ffn[layer], lc, tm_c, tf)
            else:
                hc = _ffn_moe(res_c[1], res_c[2], wg, wu, wd, res_c[0], cgt_f, g_post_ffn[layer], lc, tf)
    return hx.reshape(b, n, d)
```

```python
import functools

import numpy as np
import jax
import jax.numpy as jnp
from jax import lax
from jax.experimental import pallas as pl
from jax.experimental.pallas import tpu as pltpu

F32 = jnp.float32
BF16 = jnp.bfloat16
HIGHEST = lax.Precision.HIGHEST

GRID_W = 64
GROUP_W = 256
FNET_GROUPS = 4
FNET_CH = 64
WIN_HEADS = 4
WIN_KV_HEADS = 2
WIN_HEAD_DIM = 64
WINDOW = 128
BLOCK = 128
MLA_HEADS = 4
MLA_NOPE = 64
MLA_ROPE = 32
MLA_QK = MLA_NOPE + MLA_ROPE
MLA_V = 64
MLA_Q_RANK = 256
MLA_KV_RANK = 128
SGU_GROUPS = 4
SGU_CHUNK = 128
SGU_CH = 64
N_EXPERTS = 8
ROPE_THETA = 10000.0
EPS = 1e-6
NEG_INF = -1e30
LOG2E = 1.4426950408889634

LANES = 128
MLA_HEAD_PAD = 128
FFT_N1 = 128
MOE_TILE = 1024
FFN_SUB = 256
VMEM_LIMIT = 56 * 1024 * 1024

_O_A, _O_Q, _O_K, _O_V, _O_CQ, _O_CKV, _O_KRP, _O_SG, _O_END = (0, 256, 512, 640, 896, 1152, 1280, 1408, 1920)


def _cparams(*sem):
    return pltpu.CompilerParams(dimension_semantics=sem, vmem_limit_bytes=VMEM_LIMIT)


def _full(shape):
    nd = len(shape)
    return pl.BlockSpec(shape, lambda *_: (0,) * nd)


def _dot(a, b):
    return jnp.dot(a, b, preferred_element_type=F32)


def _dot_nt(a, b):
    return lax.dot_general(a, b, (((1,), (1,)), ((), ())), preferred_element_type=F32)


def _dot2(hi, lo, x):
    return _dot(hi, x) + _dot(lo, x)


def _split_hi_lo(m):
    m = np.asarray(m, np.float32)
    hi = jnp.asarray(m, F32).astype(BF16)
    lo = (jnp.asarray(m, F32) - hi.astype(F32)).astype(BF16)
    return hi, lo


def _rms(x, g):
    return x * lax.rsqrt(jnp.mean(x * x, axis=-1, keepdims=True) + EPS) * g


def _mod_kernel(c_ref, w_ref, b_ref, o_ref):
    c = c_ref[...]
    s = c / (1.0 + jnp.exp(-c))
    o_ref[...] = jnp.dot(s, w_ref[...], preferred_element_type=F32, precision=HIGHEST) + b_ref[...]


def _modulation(cond, w_mod, b_mod):
    r, d = cond.shape
    n6 = w_mod.shape[1]
    tn = 512
    return pl.pallas_call(
        _mod_kernel,
        grid=(n6 // tn,),
        in_specs=[_full((r, d)), pl.BlockSpec((d, tn), lambda j: (0, j)), pl.BlockSpec((1, tn), lambda j: (0, j))],
        out_specs=pl.BlockSpec((r, tn), lambda j: (0, j)),
        out_shape=jax.ShapeDtypeStruct((r, n6), F32),
        compiler_params=_cparams("arbitrary"),
        name="adaln_mod",
    )(cond, w_mod, b_mod.reshape(1, n6))


def _inproj_kernel(h_ref, g_ref, sh_ref, sc_ref, w_ref, cw_ref, sw_ref, cm_ref, sm_ref,
                   gq_ref, gkv_ref, wq_ref, wk_ref, wv_ref,
                   a_o, qw_o, kw_o, vw_o, qm_o, km_o, vm_o, u_o, vn_o):
    x = h_ref[...]
    ax = (_rms(x, g_ref[...]) * (1.0 + sc_ref[0]) + sh_ref[0]).astype(BF16)

    def proj(lo, hi):
        return _dot(ax, w_ref[:, lo:hi])

    def rope(z, cos, sin_signed):
        w = z.shape[1]
        even = lax.broadcasted_iota(jnp.int32, z.shape, 1) % 2 == 0
        partner = jnp.where(even, pltpu.roll(z, w - 1, 1), pltpu.roll(z, 1, 1))
        return z * cos + partner * sin_signed

    a_o[...] = proj(_O_A, _O_Q).astype(BF16)
    cw = cw_ref[...]
    sw = sw_ref[...]
    qw_o[...] = (rope(proj(_O_Q, _O_K), cw, sw) * ((WIN_HEAD_DIM ** -0.5) * LOG2E)).astype(BF16)
    nk = WIN_KV_HEADS * WIN_HEAD_DIM
    kw_o[...] = rope(proj(_O_K, _O_V), cw[:, :nk], sw[:, :nk]).astype(BF16)
    vw = proj(_O_V, _O_CQ)
    vw_lane = lax.broadcasted_iota(jnp.int32, vw.shape, 1) % LANES
    vw_o[...] = jnp.where(vw_lane == WIN_HEAD_DIM, 1.0, vw).astype(BF16)

    cm = cm_ref[...]
    sm = sm_ref[...]
    cqn = _rms(proj(_O_CQ, _O_CKV), gq_ref[...]).astype(BF16)
    y1 = _dot(cqn, wq_ref[...])
    scale = (MLA_QK ** -0.5) * LOG2E
    for h in range(MLA_HEADS):
        sl = slice(h * MLA_HEAD_PAD, (h + 1) * MLA_HEAD_PAD)
        qm_o[:, sl] = (rope(y1[:, sl], cm, sm) * scale).astype(BF16)

    ckvn = _rms(proj(_O_CKV, _O_KRP), gkv_ref[...]).astype(BF16)
    krp = rope(proj(_O_KRP, _O_SG), cm, sm)
    kk = _dot(ckvn, wk_ref[...])
    for h in range(MLA_HEADS):
        sl = slice(h * MLA_HEAD_PAD, (h + 1) * MLA_HEAD_PAD)
        km_o[:, sl] = (kk[:, sl] + krp).astype(BF16)
    vv = _dot(ckvn, wv_ref[...])
    lane = lax.broadcasted_iota(jnp.int32, vv.shape, 1) % MLA_HEAD_PAD
    vm_o[...] = jnp.where(lane == MLA_V, 1.0, vv).astype(BF16)

    sg = proj(_O_SG, _O_END)
    gl = 0.5 * sg * (1.0 + lax.erf(sg * (2.0 ** -0.5)))
    u_o[...] = gl[:, :GROUP_W].astype(BF16)
    v = gl[:, GROUP_W:]
    mu = jnp.mean(v, axis=-1, keepdims=True)
    vc = v - mu
    var = jnp.mean(vc * vc, axis=-1, keepdims=True)
    vn_o[...] = (vc * lax.rsqrt(var + EPS)).astype(BF16)


def _inproj(h, g, shift, scale, w_aug, tables, gq, gkv, wq, wk, wv, per_batch, tm):
    t, d = h.shape
    nmod = shift.shape[0]
    tpb = per_batch // tm
    cw, sw, cm, sm = tables

    def mod_idx(i):
        return ((i // tpb) if nmod > 1 else 0, 0, 0)

    tok = lambda w: pl.BlockSpec((tm, w), lambda i: (i, 0))
    tab = lambda w: pl.BlockSpec((tm, w), lambda i: (i % tpb, 0))
    in_specs = [tok(d), _full((1, d)), pl.BlockSpec((1, 1, d), mod_idx), pl.BlockSpec((1, 1, d), mod_idx),
                _full(w_aug.shape), tab(256), tab(256), tab(LANES), tab(LANES),
                _full((1, MLA_Q_RANK)), _full((1, MLA_KV_RANK)),
                _full(wq.shape), _full(wk.shape), _full(wv.shape)]
    widths = (256, 256, 128, 256, 512, 512, 512, 256, 256)
    return pl.pallas_call(
        _inproj_kernel,
        grid=(t // tm,),
        in_specs=in_specs,
        out_specs=[tok(w) for w in widths],
        out_shape=[jax.ShapeDtypeStruct((t, w), BF16) for w in widths],
        compiler_params=_cparams("parallel"),
        name="inproj",
    )(h, g.reshape(1, d), shift, scale, w_aug, cw, sw, cm, sm, gq.reshape(1, -1), gkv.reshape(1, -1), wq, wk, wv)


def _fft1_kernel(x_ref, mh_ref, ml_ref, tr_ref, ti_ref, or_ref, oi_ref):
    x = x_ref[0]
    z = _dot2(mh_ref[...], ml_ref[...], x)
    ar, ai = z[:FFT_N1], z[FFT_N1:]
    tr, ti = tr_ref[...], ti_ref[...]
    or_ref[0] = (ar * tr - ai * ti).astype(BF16)
    oi_ref[0] = (ar * ti + ai * tr).astype(BF16)


def _fft2_kernel(r_ref, i_ref, cs_ref, wch_ref, wcl_ref, wf_ref, o_ref, g_scr, *, nj, n2):
    cs = cs_ref[...]
    gw = o_ref.shape[-1]
    for j in range(nj):
        zr = _dot(cs, r_ref[0, j])
        zi = _dot(cs, i_ref[0, j])
        c_r, s_r = zr[:n2] + zr[n2:2 * n2], zr[2 * n2:3 * n2] + zr[3 * n2:]
        c_i, s_i = zi[:n2] + zi[n2:2 * n2], zi[2 * n2:3 * n2] + zi[3 * n2:]
        g_scr[j * n2:(j + 1) * n2, :gw] = (c_r + s_i).astype(BF16)
        g_scr[j * n2:(j + 1) * n2, gw:] = (c_i - s_r).astype(BF16)
    g = g_scr[...]
    y = _dot(g, wch_ref[...]) + _dot(g, wcl_ref[...])
    o = _dot(y.astype(BF16), wf_ref[...]).astype(BF16)
    for j in range(nj):
        o_ref[0, j] = o[j * n2:(j + 1) * n2]


def _dft_small_kernel(a_ref, ch_ref, cl_ref, sh_ref, sl_ref, bch_ref, bcl_ref, bsh_ref, bsl_ref, wf_ref, o_ref):
    a = a_ref[0]
    gr = _dot2(ch_ref[...], cl_ref[...], a).astype(BF16)
    gi = (-_dot2(sh_ref[...], sl_ref[...], a)).astype(BF16)
    y = _dot(gr, bch_ref[...]) + _dot(gr, bcl_ref[...]) + _dot(gi, bsh_ref[...]) + _dot(gi, bsl_ref[...])
    o_ref[0] = _dot(y.astype(BF16), wf_ref[...]).astype(BF16)


def _cos_sin(n):
    k = np.arange(n)
    ang = 2.0 * np.pi * ((k[:, None] * k[None, :]) % n) / n
    return np.cos(ang), np.sin(ang)


def _channel_dft(norm):
    c, s = _cos_sin(FNET_CH)
    bc = np.kron(np.eye(FNET_GROUPS), c) * norm
    bs = np.kron(np.eye(FNET_GROUPS), s) * norm
    return _split_hi_lo(bc) + _split_hi_lo(bs)


def _fourier_latent(a, w_f):
    b, n, gw = a.shape
    n2 = n // FFT_N1
    c1, s1 = _cos_sin(FFT_N1)
    mh, ml = _split_hi_lo(np.concatenate([c1, -s1], axis=0))
    k1 = np.arange(FFT_N1)[:, None]
    t2 = np.arange(n2)[None, :]
    ang = 2.0 * np.pi * ((k1 * t2) % n) / n
    tr = jnp.repeat(jnp.asarray(np.cos(ang), F32), gw, axis=1)
    ti = jnp.repeat(jnp.asarray(-np.sin(ang), F32), gw, axis=1)
    cols = n2 * gw
    tc = min(cols, 4096)
    x2 = a.reshape(b, FFT_N1, cols)
    blk = pl.BlockSpec((1, FFT_N1, tc), lambda bi, ci: (bi, 0, ci))
    tblk = pl.BlockSpec((FFT_N1, tc), lambda bi, ci: (0, ci))
    o_r, o_i = pl.pallas_call(
        _fft1_kernel,
        grid=(b, cols // tc),
        in_specs=[blk, _full(mh.shape), _full(ml.shape), tblk, tblk],
        out_specs=[blk, blk],
        out_shape=[jax.ShapeDtypeStruct((b, FFT_N1, cols), BF16)] * 2,
        compiler_params=_cparams("parallel", "parallel"),
        name="fft_stage1",
    )(x2, mh, ml, tr, ti)

    c2, s2 = _cos_sin(n2)
    cs = jnp.concatenate(_split_hi_lo(c2) + _split_hi_lo(s2), axis=0)
    bch, bcl, bsh, bsl = _channel_dft((n * FNET_CH) ** -0.5)
    wch = jnp.concatenate([bch, bsh], axis=0)
    wcl = jnp.concatenate([bcl, bsl], axis=0)
    nj = 8
    r4 = o_r.reshape(b, FFT_N1, n2, gw)
    i4 = o_i.reshape(b, FFT_N1, n2, gw)
    blk4 = pl.BlockSpec((1, nj, n2, gw), lambda bi, ki: (bi, ki, 0, 0))
    out = pl.pallas_call(
        functools.partial(_fft2_kernel, nj=nj, n2=n2),
        grid=(b, FFT_N1 // nj),
        in_specs=[blk4, blk4, _full(cs.shape), _full(wch.shape), _full(wcl.shape), _full(w_f.shape)],
        out_specs=blk4,
        out_shape=jax.ShapeDtypeStruct((b, FFT_N1, n2, gw), BF16),
        scratch_shapes=[pltpu.VMEM((nj * n2, 2 * gw), BF16)],
        compiler_params=_cparams("parallel", "parallel"),
        name="fft_stage2",
    )(r4, i4, cs, wch, wcl, w_f)
    return jnp.swapaxes(out, 1, 2).reshape(b, n, gw)


def _fourier_context(a, w_f):
    b, n, gw = a.shape
    c, s = _cos_sin(n)
    consts = _split_hi_lo(c) + _split_hi_lo(s) + _channel_dft((n * FNET_CH) ** -0.5)
    blk = pl.BlockSpec((1, n, gw), lambda bi: (bi, 0, 0))
    return pl.pallas_call(
        _dft_small_kernel,
        grid=(b,),
        in_specs=[blk] + [_full(m.shape) for m in consts] + [_full(w_f.shape)],
        out_specs=blk,
        out_shape=jax.ShapeDtypeStruct((b, n, gw), BF16),
        compiler_params=_cparams("parallel"),
        name="dft_context",
    )(a, *consts, w_f)


def _win_heads(q_blk, k_win, v_win, bias, kc, vc, sink_ref):
    rows = lax.broadcasted_iota(jnp.int32, (WIN_KV_HEADS * BLOCK, 1), 0)
    outs = [None] * WIN_HEADS
    d = WIN_HEAD_DIM
    for hk in range(WIN_KV_HEADS):
        h0 = 2 * hk
        q2 = jnp.concatenate([q_blk[:, h0 * d:(h0 + 1) * d], q_blk[:, (h0 + 1) * d:(h0 + 2) * d]], axis=0)
        ksl = slice(hk * d, (hk + 1) * d)
        vsl = slice(hk * LANES, (hk + 1) * LANES)
        sink = jnp.where(rows < BLOCK, sink_ref[h0] * LOG2E, sink_ref[h0 + 1] * LOG2E)
        s_c = _dot_nt(q2, kc[:, ksl])
        m = jnp.maximum(jnp.max(s_c, axis=-1, keepdims=True), sink)
        if k_win is not None:
            s_w = _dot_nt(q2, k_win[:, ksl]) + bias
            m = jnp.maximum(m, jnp.max(s_w, axis=-1, keepdims=True))
        r = _dot(jnp.exp2(s_c - m).astype(BF16), vc[:, vsl])
        if k_win is not None:
            r = r + _dot(jnp.exp2(s_w - m).astype(BF16), v_win[:, vsl])
        o = r[:, :d] / (r[:, d:d + 1] + jnp.exp2(sink - m))
        outs[h0] = o[:BLOCK]
        outs[h0 + 1] = o[BLOCK:]
    return jnp.concatenate(outs, axis=1)


def _win_kernel(sink_ref, q_ref, kp_ref, k_ref, kn_ref, vp_ref, v_ref, vn_ref, kc_ref, vc_ref, o_ref, *, nblk):
    i = pl.program_id(1)
    kfull = jnp.concatenate([kp_ref[0], k_ref[0], kn_ref[0]], axis=0)
    vfull = jnp.concatenate([vp_ref[0], v_ref[0], vn_ref[0]], axis=0)
    kc = kc_ref[0]
    vc = vc_ref[0]
    shape = (WIN_KV_HEADS * BLOCK, 3 * BLOCK)
    qi = lax.broadcasted_iota(jnp.int32, shape, 0) % BLOCK
    kj = lax.broadcasted_iota(jnp.int32, shape, 1)
    band = jnp.abs(kj - BLOCK - qi) <= WINDOW
    has_prev = jnp.logical_or(kj >= BLOCK, i > 0)
    has_next = jnp.logical_or(kj < 2 * BLOCK, i < pl.num_programs(1) - 1)
    bias_mid = jnp.where(band, 0.0, NEG_INF)
    for j in range(nblk):
        ok = band
        if j == 0:
            ok = ok & has_prev
        if j == nblk - 1:
            ok = ok & has_next
        bias = bias_mid if ok is band else jnp.where(ok, 0.0, NEG_INF)
        o = _win_heads(q_ref[0, j * BLOCK:(j + 1) * BLOCK, :], kfull[j * BLOCK:(j + 3) * BLOCK],
                       vfull[j * BLOCK:(j + 3) * BLOCK], bias, kc, vc, sink_ref)
        o_ref[0, j * BLOCK:(j + 1) * BLOCK, :] = o.astype(BF16)


def _winctx_kernel(sink_ref, q_ref, kc_ref, vc_ref, o_ref, *, nblk):
    kc = kc_ref[0]
    vc = vc_ref[0]
    for j in range(nblk):
        o = _win_heads(q_ref[0, j * BLOCK:(j + 1) * BLOCK, :], None, None, None, kc, vc, sink_ref)
        o_ref[0, j * BLOCK:(j + 1) * BLOCK, :] = o.astype(BF16)


def _window_latent(q, k, v, kc, vc, sink):
    b, n, _ = q.shape
    lc = kc.shape[1]
    nblk = min(8, n // BLOCK)
    tq = nblk * BLOCK
    nb = n // BLOCK
    kw = WIN_KV_HEADS * WIN_HEAD_DIM
    vw = WIN_KV_HEADS * LANES
    cur = lambda w: pl.BlockSpec((1, tq, w), lambda bi, i: (bi, i, 0))
    prev = lambda w: pl.BlockSpec((1, BLOCK, w), lambda bi, i: (bi, jnp.maximum(i * nblk - 1, 0), 0))
    nxt = lambda w: pl.BlockSpec((1, BLOCK, w), lambda bi, i: (bi, jnp.minimum((i + 1) * nblk, nb - 1), 0))
    ctx = lambda w: pl.BlockSpec((1, lc, w), lambda bi, i: (bi, 0, 0))
    smem = pl.BlockSpec(memory_space=pltpu.SMEM)
    return pl.pallas_call(
        functools.partial(_win_kernel, nblk=nblk),
        grid=(b, n // tq),
        in_specs=[smem, cur(256), prev(kw), cur(kw), nxt(kw), prev(vw), cur(vw), nxt(vw), ctx(kw), ctx(vw)],
        out_specs=cur(256),
        out_shape=jax.ShapeDtypeStruct((b, n, 256), BF16),
        compiler_params=_cparams("parallel", "arbitrary"),
        name="window_attn",
    )(sink, q, k, k, k, v, v, v, kc, vc)


def _window_context(q, kc, vc, sink):
    b, lc, _ = q.shape
    kvw = WIN_KV_HEADS * WIN_HEAD_DIM
    blk = lambda w: pl.BlockSpec((1, lc, w), lambda bi: (bi, 0, 0))
    smem = pl.BlockSpec(memory_space=pltpu.SMEM)
    return pl.pallas_call(
        functools.partial(_winctx_kernel, nblk=lc // BLOCK),
        grid=(b,),
        in_specs=[smem, blk(256), blk(kvw), blk(WIN_KV_HEADS * LANES)],
        out_specs=blk(256),
        out_shape=jax.ShapeDtypeStruct((b, lc, 256), BF16),
        compiler_params=_cparams("parallel"),
        name="window_attn_ctx",
    )(sink, q, kc, vc)


def _mla_reduce(state, s, v):
    m, acc = state
    m_new = jnp.maximum(m, jnp.max(s, axis=-1, keepdims=True))
    alpha = jnp.exp2(m - m_new)
    p = jnp.exp2(s - m_new).astype(BF16)
    return m_new, alpha * acc + _dot(p, v)


def _mla_finish(acc):
    return (acc[:, :MLA_V] / acc[:, MLA_V:MLA_V + 1]).astype(BF16)


def _mla_kernel(q_ref, k_ref, v_ref, kc_ref, vc_ref, o_ref, s_scr, sc_scr, *, tk, nk):
    tq = q_ref.shape[1]
    hsl = [slice(h * MLA_HEAD_PAD, (h + 1) * MLA_HEAD_PAD) for h in range(2)]
    qs = [q_ref[0, :, sl] for sl in hsl]

    def scores(c, slot):
        off = pl.multiple_of(c * tk, tk)
        for h in range(2):
            s_scr[slot, h] = _dot_nt(qs[h], k_ref[0, pl.ds(off, tk), hsl[h]])

    def reduce(c, slot, states):
        off = pl.multiple_of(c * tk, tk)
        return tuple(_mla_reduce(states[h], s_scr[slot, h], v_ref[0, pl.ds(off, tk), hsl[h]]) for h in range(2))

    scores(0, 0)

    def body(i, states):
        scores(2 * i + 1, 1)
        states = reduce(2 * i, 0, states)
        scores(2 * i + 2, 0)
        return reduce(2 * i + 1, 1, states)

    init = tuple((jnp.full((tq, 1), NEG_INF, F32), jnp.zeros((tq, MLA_HEAD_PAD), F32)) for _ in range(2))
    states = init
    for pair in range(nk // 2 - 1):
        states = body(pair, states)
    scores(nk - 1, 1)
    states = reduce(nk - 2, 0, states)
    for h in range(2):
        sc_scr[h] = _dot_nt(qs[h], kc_ref[0, :, hsl[h]])
    states = reduce(nk - 1, 1, states)
    for h in range(2):
        _, acc = _mla_reduce(states[h], sc_scr[h], vc_ref[0, :, hsl[h]])
        o_ref[0, :, h * MLA_V:(h + 1) * MLA_V] = _mla_finish(acc)


def _mla_small_kernel(q_ref, k_ref, v_ref, o_ref):
    tq = q_ref.shape[1]
    for h in range(2):
        sl = slice(h * MLA_HEAD_PAD, (h + 1) * MLA_HEAD_PAD)
        init = (jnp.full((tq, 1), NEG_INF, F32), jnp.zeros((tq, MLA_HEAD_PAD), F32))
        _, acc = _mla_reduce(init, _dot_nt(q_ref[0, :, sl], k_ref[0, :, sl]), v_ref[0, :, sl])
        o_ref[0, :, h * MLA_V:(h + 1) * MLA_V] = _mla_finish(acc)


def _mla_attention(q, k, v, kc, vc):
    b, nq, _ = q.shape
    nkeys = k.shape[1]
    lc = kc.shape[1]
    tq = min(1024, nq)
    tk = min(1024, nkeys // 2)
    assert nkeys % (2 * tk) == 0 and nq % tq == 0
    w2 = 2 * MLA_HEAD_PAD
    res = lambda n: pl.BlockSpec((1, n, w2), lambda bi, h, i: (bi, 0, h))
    return pl.pallas_call(
        functools.partial(_mla_kernel, tk=tk, nk=nkeys // tk),
        grid=(b, MLA_HEADS // 2, nq // tq),
        in_specs=[pl.BlockSpec((1, tq, w2), lambda bi, h, i: (bi, i, h)), res(nkeys), res(nkeys), res(lc), res(lc)],
        out_specs=pl.BlockSpec((1, tq, 2 * MLA_V), lambda bi, h, i: (bi, i, h)),
        out_shape=jax.ShapeDtypeStruct((b, nq, MLA_HEADS * MLA_V), BF16),
        scratch_shapes=[pltpu.VMEM((2, 2, tq, tk), F32), pltpu.VMEM((2, tq, lc), F32)],
        compiler_params=_cparams("parallel", "parallel", "arbitrary"),
        name="mla_attn",
    )(q, k, v, kc, vc)


def _mla_attention_small(q, k, v):
    b, n, _ = q.shape
    w2 = 2 * MLA_HEAD_PAD
    blk = pl.BlockSpec((1, n, w2), lambda bi, h: (bi, 0, h))
    return pl.pallas_call(
        _mla_small_kernel,
        grid=(b, MLA_HEADS // 2),
        in_specs=[blk, blk, blk],
        out_specs=pl.BlockSpec((1, n, 2 * MLA_V), lambda bi, h: (bi, 0, h)),
        out_shape=jax.ShapeDtypeStruct((b, n, MLA_HEADS * MLA_V), BF16),
        compiler_params=_cparams("parallel", "parallel"),
        name="mla_attn_ctx",
    )(q, k, v)


def _sgu_kernel(u_ref, v_ref, w_ref, b_ref, o_ref, *, nchunk):
    lane = lax.broadcasted_iota(jnp.int32, (SGU_CHUNK, GROUP_W), 1) // SGU_CH
    bias = b_ref[...]
    for c in range(nchunk):
        sl = slice(c * SGU_CHUNK, (c + 1) * SGU_CHUNK)
        v = v_ref[0, sl, :]
        s = bias
        for g in range(SGU_GROUPS):
            s = s + _dot(w_ref[g], jnp.where(lane == g, v, jnp.zeros_like(v)))
        o_ref[0, sl, :] = (u_ref[0, sl, :].astype(F32) * s).astype(BF16)


def _sgu(u, vn, w_s, bias_full):
    b, n, gw = u.shape
    nchunk = min(8, n // SGU_CHUNK)
    tn = nchunk * SGU_CHUNK
    blk = pl.BlockSpec((1, tn, gw), lambda bi, i: (bi, i, 0))
    return pl.pallas_call(
        functools.partial(_sgu_kernel, nchunk=nchunk),
        grid=(b, n // tn),
        in_specs=[blk, blk, _full(w_s.shape), _full(bias_full.shape)],
        out_specs=blk,
        out_shape=jax.ShapeDtypeStruct((b, n, gw), BF16),
        compiler_params=_cparams("parallel", "parallel"),
        name="sgu",
    )(u, vn, w_s, bias_full)


def _outproj_kernel(*refs, with_router):
    if with_router:
        (p0, p1, p2, p3, w_ref, h_ref, gate_ref, gpost_ref, gpre_ref, sh_ref, sc_ref, wr_ref,
         hn_o, fx_o, comb_o) = refs
    else:
        p0, p1, p2, p3, w_ref, h_ref, gate_ref, gpost_ref, gpre_ref, sh_ref, sc_ref, hn_o, fx_o = refs
    mx = None
    for idx, p in enumerate((p0, p1, p2, p3)):
        t = _dot(p[...], w_ref[idx * GROUP_W:(idx + 1) * GROUP_W, :])
        mx = t if mx is None else mx + t
    hn = h_ref[...] + gate_ref[0] * _rms(mx, gpost_ref[...])
    hn_o[...] = hn
    fx = _rms(hn, gpre_ref[...]) * (1.0 + sc_ref[0]) + sh_ref[0]
    fx_o[...] = fx.astype(fx_o.dtype)
    if with_router:
        fx_hi = fx.astype(BF16)
        fx_lo = (fx - fx_hi.astype(F32)).astype(BF16)
        logits = _dot(fx_hi, wr_ref[0]) + (_dot(fx_lo, wr_ref[0]) + _dot(fx_hi, wr_ref[1]))
        lane = lax.broadcasted_iota(jnp.int32, logits.shape, 1).astype(F32)
        lg = jnp.where(lane < N_EXPERTS, logits, -jnp.inf)
        m1 = jnp.max(lg, axis=-1, keepdims=True)
        i1 = jnp.min(jnp.where(lg == m1, lane, float(LANES)), axis=-1, keepdims=True)
        lg2 = jnp.where(lane == i1, -jnp.inf, lg)
        m2 = jnp.max(lg2, axis=-1, keepdims=True)
        i2 = jnp.min(jnp.where(lg2 == m2, lane, float(LANES)), axis=-1, keepdims=True)
        e2 = jnp.exp(m2 - m1)
        den = 1.0 + e2
        flag = (lane == i1 + N_EXPERTS) | (lane == i2 + N_EXPERTS)
        comb_o[...] = (jnp.where(lane == i1, 1.0 / den, 0.0) + jnp.where(lane == i2, e2 / den, 0.0)
                       + jnp.where(flag, 1.0, 0.0))


def _outproj(parts, w_out, h, gate, g_post, g_pre, shift, scale, per_batch, tm, w_router=None):
    t, d = h.shape
    nmod = gate.shape[0]
    tpb = per_batch // tm
    with_router = w_router is not None

    def mod_idx(i):
        return ((i // tpb) if nmod > 1 else 0, 0, 0)

    tok = lambda w: pl.BlockSpec((tm, w), lambda i: (i, 0))
    mod = pl.BlockSpec((1, 1, d), mod_idx)
    in_specs = [tok(GROUP_W)] * 4 + [_full(w_out.shape), tok(d), mod, _full((1, d)), _full((1, d)), mod, mod]
    args = list(parts) + [w_out, h, gate, g_post.reshape(1, d), g_pre.reshape(1, d), shift, scale]
    out_specs = [tok(d), tok(d)]
    out_shape = [jax.ShapeDtypeStruct((t, d), F32), jax.ShapeDtypeStruct((t, d), F32 if with_router else BF16)]
    if with_router:
        in_specs.append(_full(w_router.shape))
        args.append(w_router)
        out_specs.append(tok(LANES))
        out_shape.append(jax.ShapeDtypeStruct((t, LANES), F32))
    return pl.pallas_call(
        functools.partial(_outproj_kernel, with_router=with_router),
        grid=(t // tm,),
        in_specs=in_specs,
        out_specs=out_specs,
        out_shape=out_shape,
        compiler_params=_cparams("parallel"),
        name="outproj",
    )(*args)


def _swiglu_partial(x, wg, wu, wd, tf):
    y = None
    for lo in range(0, tf, FFN_SUB):
        hi = lo + FFN_SUB
        g = _dot(x, wg(lo, hi))
        u = _dot(x, wu(lo, hi))
        hid = (g / (1.0 + jnp.exp(-g)) * u).astype(BF16)
        t = _dot(hid, wd(lo, hi))
        y = t if y is None else y + t
    return y


def _ffn_kernel(x_ref, wg_ref, wu_ref, wd_ref, h_ref, gate_ref, gpost_ref, o_ref):
    f = pl.program_id(1)
    last = pl.num_programs(1) - 1
    y = _swiglu_partial(x_ref[...], lambda lo, hi: wg_ref[:, lo:hi], lambda lo, hi: wu_ref[:, lo:hi],
                        lambda lo, hi: wd_ref[lo:hi, :], wd_ref.shape[0])

    @pl.when(f == 0)
    def _():
        o_ref[...] = y

    @pl.when((f > 0) & (f < last))
    def _():
        o_ref[...] += y

    @pl.when(f == last)
    def _():
        o_ref[...] = h_ref[...] + gate_ref[0] * _rms(o_ref[...] + y, gpost_ref[...])


def _ffn_dense(fx, wg, wu, wd, h, gate, g_post, per_batch, tm, tf):
    t, d = h.shape
    dff = wg.shape[1]
    nmod = gate.shape[0]
    tpb = per_batch // tm
    assert dff // tf >= 2 and tf % FFN_SUB == 0

    def mod_idx(i, f):
        return ((i // tpb) if nmod > 1 else 0, 0, 0)

    tok = pl.BlockSpec((tm, d), lambda i, f: (i, 0))
    return pl.pallas_call(
        _ffn_kernel,
        grid=(t // tm, dff // tf),
        in_specs=[tok, pl.BlockSpec((d, tf), lambda i, f: (0, f)), pl.BlockSpec((d, tf), lambda i, f: (0, f)),
                  pl.BlockSpec((tf, d), lambda i, f: (f, 0)), tok, pl.BlockSpec((1, 1, d), mod_idx),
                  pl.BlockSpec((1, d), lambda i, f: (0, 0))],
        out_specs=tok,
        out_shape=jax.ShapeDtypeStruct((t, d), F32),
        compiler_params=_cparams("parallel", "arbitrary"),
        name="ffn_dense",
    )(fx, wg, wu, wd, h, gate, g_post.reshape(1, d))


def _route(comb, tile):
    t = comb.shape[0]
    sel = (comb[:, N_EXPERTS:2 * N_EXPERTS] > 0.5).astype(jnp.int32)
    rank = jnp.cumsum(sel, axis=0) - sel
    padded = (jnp.sum(sel, axis=0) + tile - 1) // tile * tile
    ends = jnp.cumsum(padded)
    pos_e = (ends - padded)[None, :] + rank
    e_id = jnp.arange(N_EXPERTS, dtype=jnp.int32)[None, :]
    e_lo = jnp.min(jnp.where(sel > 0, e_id, N_EXPERTS - 1), axis=1)
    e_hi = jnp.max(sel * e_id, axis=1)
    take = lambda a, e: jnp.take_along_axis(a, e[:, None], axis=1)[:, 0]
    pos = jnp.stack([take(pos_e, e_lo), take(pos_e, e_hi)], axis=1).reshape(2 * t).astype(jnp.int32)
    w = comb[:, :N_EXPERTS]
    wts = jnp.pad(jnp.stack([take(w, e_lo), take(w, e_hi)], axis=1), ((0, 0), (0, LANES - 2)))
    ntiles = -(-2 * t // tile) + N_EXPERTS
    nact = (ends[-1] // tile).astype(jnp.int32)
    tile_id = jnp.arange(ntiles, dtype=jnp.int32)
    te = jnp.minimum(jnp.searchsorted(ends, tile_id * tile, side="right"), N_EXPERTS - 1).astype(jnp.int32)
    te = jnp.where(tile_id < nact, te, te[nact - 1])
    return pos, wts, te, nact.reshape(1), ntiles


def _dispatch_kernel(pos_ref, x_ref, xs_in_ref, xs_ref, sem, *, tm):
    del xs_in_ref
    base = pl.program_id(0) * tm

    def row_copy(t, p):
        return pltpu.make_async_copy(x_ref.at[pl.ds(t, 1)], xs_ref.at[pl.ds(p, 1)], sem)

    def issue(t, carry):
        row_copy(t, pos_ref[2 * (base + t)]).start(priority=0)
        row_copy(t, pos_ref[2 * (base + t) + 1]).start(priority=1)
        return carry

    def drain(t, carry):
        row_copy(0, 0).wait()
        row_copy(0, 0).wait()
        return carry

    lax.fori_loop(0, tm, issue, 0, unroll=8)
    lax.fori_loop(0, tm, drain, 0, unroll=8)


def _gmm_kernel(te_ref, nact_ref, x_ref, wg_ref, wu_ref, wd_ref, o_ref):
    del te_ref
    f = pl.program_id(1)
    active = pl.program_id(0) < nact_ref[0]

    @pl.when(jnp.logical_not(active) & (f == 0))
    def _():
        o_ref[...] = jnp.zeros_like(o_ref)

    @pl.when(active)
    def _():
        y = _swiglu_partial(x_ref[...].astype(BF16), lambda lo, hi: wg_ref[0, :, lo:hi],
                            lambda lo, hi: wu_ref[0, :, lo:hi], lambda lo, hi: wd_ref[0, lo:hi, :], wd_ref.shape[1])

        @pl.when(f == 0)
        def _():
            o_ref[...] = y

        @pl.when(f > 0)
        def _():
            o_ref[...] += y


def _combine_kernel(pos_ref, ys_ref, w_ref, h_ref, gate_ref, gpost_ref, o_ref, buf, sem, *, tm):
    base = pl.program_id(0) * tm

    def row_copy(choice, t, p):
        return pltpu.make_async_copy(ys_ref.at[pl.ds(p, 1)], buf.at[choice, pl.ds(t, 1)], sem)

    def issue(t, carry):
        row_copy(0, t, pos_ref[2 * (base + t)]).start(priority=0)
        row_copy(1, t, pos_ref[2 * (base + t) + 1]).start(priority=1)
        return carry

    def drain(t, carry):
        row_copy(0, 0, 0).wait()
        row_copy(1, 0, 0).wait()
        return carry

    lax.fori_loop(0, tm, issue, 0, unroll=8)
    lax.fori_loop(0, tm, drain, 0, unroll=8)
    w = w_ref[...]
    y = w[:, 0:1] * buf[0] + w[:, 1:2] * buf[1]
    o_ref[...] = h_ref[...] + gate_ref[0] * _rms(y, gpost_ref[...])


def _ffn_moe(fx, comb, wg, wu, wd, h, gate, g_post, per_batch, tf):
    t, d = h.shape
    dff = wg.shape[2]
    nf = dff // tf
    tile = MOE_TILE
    pos, wts, te, nact, ntiles = _route(comb, tile)
    ts = ntiles * tile
    any_spec = pl.BlockSpec(memory_space=pl.ANY)

    tm_d = min(512, per_batch)
    xs = pl.pallas_call(
        functools.partial(_dispatch_kernel, tm=tm_d),
        grid_spec=pltpu.PrefetchScalarGridSpec(
            num_scalar_prefetch=1, grid=(t // tm_d,),
            in_specs=[pl.BlockSpec((tm_d, d), lambda i, p: (i, 0)), any_spec],
            out_specs=any_spec,
            scratch_shapes=[pltpu.SemaphoreType.DMA(())]),
        out_shape=jax.ShapeDtypeStruct((ts, d), F32),
        input_output_aliases={2: 0},
        compiler_params=_cparams("arbitrary"),
        name="moe_dispatch",
    )(pos, fx, jnp.zeros((ts, d), F32))

    row = lambda i, f, te_r, na_r: (jnp.minimum(i, na_r[0] - 1), 0)
    fidx = lambda i, f, na_r: jnp.where(i < na_r[0], f, nf - 1)
    ys = pl.pallas_call(
        _gmm_kernel,
        grid_spec=pltpu.PrefetchScalarGridSpec(
            num_scalar_prefetch=2, grid=(ntiles, nf),
            in_specs=[pl.BlockSpec((tile, d), row),
                      pl.BlockSpec((1, d, tf), lambda i, f, te_r, na_r: (te_r[i], 0, fidx(i, f, na_r))),
                      pl.BlockSpec((1, d, tf), lambda i, f, te_r, na_r: (te_r[i], 0, fidx(i, f, na_r))),
                      pl.BlockSpec((1, tf, d), lambda i, f, te_r, na_r: (te_r[i], fidx(i, f, na_r), 0))],
            out_specs=pl.BlockSpec((tile, d), lambda i, f, te_r, na_r: (i, 0))),
        out_shape=jax.ShapeDtypeStruct((ts, d), F32),
        compiler_params=_cparams("arbitrary", "arbitrary"),
        name="moe_experts",
    )(te, nact, xs, wg, wu, wd)

    tm_c = min(512, per_batch)
    nmod = gate.shape[0]
    tpb = per_batch // tm_c
    tok = lambda w: pl.BlockSpec((tm_c, w), lambda i, p: (i, 0))
    return pl.pallas_call(
        functools.partial(_combine_kernel, tm=tm_c),
        grid_spec=pltpu.PrefetchScalarGridSpec(
            num_scalar_prefetch=1, grid=(t // tm_c,),
            in_specs=[any_spec, tok(LANES), tok(d),
                      pl.BlockSpec((1, 1, d), lambda i, p: ((i // tpb) if nmod > 1 else 0, 0, 0)),
                      pl.BlockSpec((1, d), lambda i, p: (0, 0))],
            out_specs=tok(d),
            scratch_shapes=[pltpu.VMEM((2, tm_c, d), F32), pltpu.SemaphoreType.DMA(())]),
        out_shape=jax.ShapeDtypeStruct((t, d), F32),
        compiler_params=_cparams("arbitrary"),
        name="moe_combine",
    )(pos, ys, wts, h, gate, g_post.reshape(1, d))


def _rope_tables(n, rot_dim):
    n_freq = rot_dim // 4
    inv_freq = ROPE_THETA ** (-jnp.arange(n_freq, dtype=F32) / n_freq)
    rows = n // GRID_W
    row = jnp.repeat(jnp.arange(rows, dtype=F32), GRID_W)
    col = jnp.tile(jnp.arange(GRID_W, dtype=F32), rows)
    ang = jnp.concatenate([row[:, None] * inv_freq, col[:, None] * inv_freq], axis=-1)
    sin = jnp.sin(ang)
    return jnp.repeat(jnp.cos(ang), 2, axis=-1), jnp.stack([-sin, sin], axis=-1).reshape(n, rot_dim)


def _latent_tables(n):
    cw, sw = _rope_tables(n, WIN_HEAD_DIM)
    cm, sm = _rope_tables(n, MLA_ROPE)
    one = jnp.ones((n, MLA_NOPE), F32)
    zero_n = jnp.zeros((n, MLA_NOPE), F32)
    zero_p = jnp.zeros((n, MLA_HEAD_PAD - MLA_QK), F32)
    return (jnp.tile(cw, (1, WIN_HEADS)), jnp.tile(sw, (1, WIN_HEADS)),
            jnp.concatenate([one, cm, zero_p], axis=1), jnp.concatenate([zero_n, sm, zero_p], axis=1))


def _context_tables(lc):
    one_w = jnp.ones((lc, WIN_HEADS * WIN_HEAD_DIM), F32)
    cm = jnp.concatenate([jnp.ones((lc, MLA_QK), F32), jnp.zeros((lc, MLA_HEAD_PAD - MLA_QK), F32)], axis=1)
    return one_w, jnp.zeros_like(one_w), cm, jnp.zeros_like(cm)


def _prep_layer_weights(w_in, mla_w_uq, mla_w_ukv):
    d = w_in.shape[0]
    o = np.cumsum((0, 256, 256, 128, 128, 256, 128, 32, 512))
    a, q, k, v, cq, ckv, kr, sg = [w_in[:, o[i]:o[i + 1]] for i in range(8)]
    zl = jnp.zeros((d, MLA_NOPE), w_in.dtype)
    zr = jnp.zeros((d, MLA_HEAD_PAD - MLA_QK), w_in.dtype)
    krp = jnp.concatenate([zl, kr, zr], axis=1)
    zh = jnp.zeros((d, LANES - WIN_HEAD_DIM), w_in.dtype)
    vp = jnp.concatenate([v[:, :WIN_HEAD_DIM], zh, v[:, WIN_HEAD_DIM:], zh], axis=1)
    w_aug = jnp.concatenate([a, q, k, vp, cq, ckv, krp, sg], axis=1).astype(BF16)

    wq = mla_w_uq.reshape(MLA_Q_RANK, MLA_HEADS, MLA_QK)
    zq = jnp.zeros((MLA_Q_RANK, MLA_HEADS, MLA_HEAD_PAD - MLA_QK), wq.dtype)
    wq = jnp.concatenate([wq, zq], axis=-1).reshape(MLA_Q_RANK, MLA_HEADS * MLA_HEAD_PAD).astype(BF16)

    wkv = mla_w_ukv.reshape(MLA_KV_RANK, MLA_HEADS, MLA_NOPE + MLA_V)
    zk = jnp.zeros((MLA_KV_RANK, MLA_HEADS, MLA_HEAD_PAD - MLA_NOPE), wkv.dtype)
    wk = jnp.concatenate([wkv[..., :MLA_NOPE], zk], axis=-1).reshape(MLA_KV_RANK, MLA_HEADS * MLA_HEAD_PAD).astype(BF16)
    zv = jnp.zeros((MLA_KV_RANK, MLA_HEADS, MLA_HEAD_PAD - MLA_V), wkv.dtype)
    wv = jnp.concatenate([wkv[..., MLA_NOPE:], zv], axis=-1).reshape(MLA_KV_RANK, MLA_HEADS * MLA_HEAD_PAD).astype(BF16)
    return w_aug, wq, wk, wv


def _token_tile(per_batch, pref):
    return min(pref, per_batch)


def kernel(x, c, ctx, c_ctx, w_mod, b_mod, g_pre_mix, g_post_mix, g_pre_ffn, g_post_ffn, w_in, w_out, w_fourier, win_sink, mla_g_q, mla_g_kv, mla_w_uq, mla_w_ukv, sgu_w, sgu_b, ffn_w_gate, ffn_w_up, ffn_w_down, moe_w_router, moe_w_gate, moe_w_up, moe_w_down):
    b, n, d = x.shape
    lc = ctx.shape[1]
    depth = w_mod.shape[0]
    tx, tcx = b * n, b * lc
    tm_x = _token_tile(n, 512)
    tm_c = _token_tile(lc, 512)
    tab_x = _latent_tables(n)
    tab_c = _context_tables(lc)
    cond = jnp.concatenate([c, c_ctx[None, :], jnp.zeros((8 - (b + 1) % 8 if (b + 1) % 8 else 0, d), F32)], axis=0)

    hx = x.reshape(tx, d)
    hc = ctx.reshape(tcx, d)
    for layer in range(depth):
        last = layer == depth - 1
        mod = _modulation(cond, w_mod[layer], b_mod[layer])
        mx6 = mod[:b].reshape(b, 1, 6, d)
        mc6 = mod[b:b + 1].reshape(1, 1, 6, d)
        sh_a, sc_a, gt_a, sh_f, sc_f, gt_f = [mx6[:, :, i] for i in range(6)]
        csh_a, csc_a, cgt_a, csh_f, csc_f, cgt_f = [mc6[:, :, i] for i in range(6)]

        w_aug, wq, wk, wv = _prep_layer_weights(w_in[layer], mla_w_uq[layer], mla_w_ukv[layer])
        w_f = w_fourier[layer].astype(BF16)
        w_o = w_out[layer].astype(BF16)
        w_s = sgu_w[layer].astype(BF16)
        bias_full = jnp.repeat(sgu_b[layer].T, SGU_CH, axis=1)
        sink = win_sink[layer]
        gq, gkv = mla_g_q[layer], mla_g_kv[layer]

        px = _inproj(hx, g_pre_mix[layer], sh_a, sc_a, w_aug, tab_x, gq, gkv, wq, wk, wv, n, tm_x)
        pc = _inproj(hc, g_pre_mix[layer], csh_a, csc_a, w_aug, tab_c, gq, gkv, wq, wk, wv, lc, tm_c)
        a_x, qw_x, kw_x, vw_x, qm_x, km_x, vm_x, u_x, vn_x = [t.reshape(b, n, -1) for t in px]
        a_c, qw_c, kw_c, vw_c, qm_c, km_c, vm_c, u_c, vn_c = [t.reshape(b, lc, -1) for t in pc]

        four_x = _fourier_latent(a_x, w_f)
        win_x = _window_latent(qw_x, kw_x, vw_x, kw_c, vw_c, sink)
        mla_x = _mla_attention(qm_x, km_x, vm_x, km_c, vm_c)
        sgu_x = _sgu(u_x, vn_x, w_s, bias_full)
        parts_x = [t.reshape(tx, GROUP_W) for t in (four_x, win_x, mla_x, sgu_x)]

        i = layer // 2
        dense = layer % 2 == 0
        dff = ffn_w_gate.shape[-1]
        tf = dff // 2 if (dff // 2) % FFN_SUB == 0 else FFN_SUB
        w_router = None
        if not dense:
            w_r = jnp.concatenate([moe_w_router[i], jnp.zeros((d, LANES - N_EXPERTS), F32)], axis=1)
            w_r_hi = w_r.astype(BF16)
            w_router = jnp.stack([w_r_hi, (w_r - w_r_hi.astype(F32)).astype(BF16)])
        res = _outproj(parts_x, w_o, hx, gt_a, g_post_mix[layer], g_pre_ffn[layer], sh_f, sc_f, n, tm_x, w_router)
        if dense:
            wg, wu, wd = ffn_w_gate[i].astype(BF16), ffn_w_up[i].astype(BF16), ffn_w_down[i].astype(BF16)
            hx = _ffn_dense(res[1], wg, wu, wd, res[0], gt_f, g_post_ffn[layer], n, min(1024, n), tf)
        else:
            wg, wu, wd = moe_w_gate[i].astype(BF16), moe_w_up[i].astype(BF16), moe_w_down[i].astype(BF16)
            hx = _ffn_moe(res[1], res[2], wg, wu, wd, res[0], gt_f, g_post_ffn[layer], n, tf)

        if not last:
            four_c = _fourier_context(a_c, w_f)
            win_c = _window_context(qw_c, kw_c, vw_c, sink)
            mla_c = _mla_attention_small(qm_c, km_c, vm_c)
            sgu_c = _sgu(u_c, vn_c, w_s, bias_full)
            parts_c = [t.reshape(tcx, GROUP_W) for t in (four_c, win_c, mla_c, sgu_c)]
            res_c = _outproj(parts_c, w_o, hc, cgt_a, g_post_mix[layer], g_pre_ffn[layer], csh_f, csc_f, lc, tm_c, w_router)
            if dense:
                hc = _ffn_dense(res_c[1], wg, wu, wd, res_c[0], cgt_f, g_post_ffn[layer], lc, tm_c, tf)
            else:
                hc = _ffn_moe(res_c[1], res_c[2], wg, wu, wd, res_c[0], cgt_f, g_post_ffn[layer], lc, tf)
    return hx.reshape(b, n, d)
```

```python
import functools

import numpy as np
import jax
import jax.numpy as jnp
from jax import lax
from jax.experimental import pallas as pl
from jax.experimental.pallas import tpu as pltpu

F32 = jnp.float32
BF16 = jnp.bfloat16
HIGHEST = lax.Precision.HIGHEST

GRID_W = 64
GROUP_W = 256
FNET_GROUPS = 4
FNET_CH = 64
WIN_HEADS = 4
WIN_KV_HEADS = 2
WIN_HEAD_DIM = 64
WINDOW = 128
BLOCK = 128
MLA_HEADS = 4
MLA_NOPE = 64
MLA_ROPE = 32
MLA_QK = MLA_NOPE + MLA_ROPE
MLA_V = 64
MLA_Q_RANK = 256
MLA_KV_RANK = 128
SGU_GROUPS = 4
SGU_CHUNK = 128
SGU_CH = 64
N_EXPERTS = 8
ROPE_THETA = 10000.0
EPS = 1e-6
NEG_INF = -1e30
LOG2E = 1.4426950408889634

LANES = 128
MLA_HEAD_PAD = 128
FFT_N1 = 128
MOE_TILE = 1024
FFN_SUB = 256
VMEM_LIMIT = 56 * 1024 * 1024

_O_A, _O_Q, _O_K, _O_V, _O_CQ, _O_CKV, _O_KRP, _O_SG, _O_END = (0, 256, 512, 640, 896, 1152, 1280, 1408, 1920)


def _cparams(*sem):
    return pltpu.CompilerParams(dimension_semantics=sem, vmem_limit_bytes=VMEM_LIMIT)


def _full(shape):
    nd = len(shape)
    return pl.BlockSpec(shape, lambda *_: (0,) * nd)


def _dot(a, b):
    return jnp.dot(a, b, preferred_element_type=F32)


def _dot_nt(a, b):
    return lax.dot_general(a, b, (((1,), (1,)), ((), ())), preferred_element_type=F32)


def _dot2(hi, lo, x):
    return _dot(hi, x) + _dot(lo, x)


def _split_hi_lo(m):
    m = np.asarray(m, np.float32)
    hi = jnp.asarray(m, F32).astype(BF16)
    lo = (jnp.asarray(m, F32) - hi.astype(F32)).astype(BF16)
    return hi, lo


def _rms(x, g):
    return x * lax.rsqrt(jnp.mean(x * x, axis=-1, keepdims=True) + EPS) * g


def _mod_kernel(c_ref, w_ref, b_ref, o_ref):
    c = c_ref[...]
    s = c / (1.0 + jnp.exp(-c))
    o_ref[...] = jnp.dot(s, w_ref[...], preferred_element_type=F32, precision=HIGHEST) + b_ref[...]


def _modulation(cond, w_mod, b_mod):
    r, d = cond.shape
    n6 = w_mod.shape[1]
    tn = 512
    return pl.pallas_call(
        _mod_kernel,
        grid=(n6 // tn,),
        in_specs=[_full((r, d)), pl.BlockSpec((d, tn), lambda j: (0, j)), pl.BlockSpec((1, tn), lambda j: (0, j))],
        out_specs=pl.BlockSpec((r, tn), lambda j: (0, j)),
        out_shape=jax.ShapeDtypeStruct((r, n6), F32),
        compiler_params=_cparams("arbitrary"),
        name="adaln_mod",
    )(cond, w_mod, b_mod.reshape(1, n6))


def _inproj_kernel(h_ref, g_ref, sh_ref, sc_ref, w_ref, cw_ref, sw_ref, cm_ref, sm_ref,
                   gq_ref, gkv_ref, wq_ref, wk_ref, wv_ref,
                   a_o, qw_o, kw_o, vw_o, qm_o, km_o, vm_o, u_o, vn_o):
    x = h_ref[...]
    ax = (_rms(x, g_ref[...]) * (1.0 + sc_ref[0]) + sh_ref[0]).astype(BF16)

    def proj(lo, hi):
        return _dot(ax, w_ref[:, lo:hi])

    def rope(z, cos, sin_signed):
        w = z.shape[1]
        even = lax.broadcasted_iota(jnp.int32, z.shape, 1) % 2 == 0
        partner = jnp.where(even, pltpu.roll(z, w - 1, 1), pltpu.roll(z, 1, 1))
        return z * cos + partner * sin_signed

    a_o[...] = proj(_O_A, _O_Q).astype(BF16)
    cw = cw_ref[...]
    sw = sw_ref[...]
    qw_o[...] = (rope(proj(_O_Q, _O_K), cw, sw) * ((WIN_HEAD_DIM ** -0.5) * LOG2E)).astype(BF16)
    nk = WIN_KV_HEADS * WIN_HEAD_DIM
    kw_o[...] = rope(proj(_O_K, _O_V), cw[:, :nk], sw[:, :nk]).astype(BF16)
    vw = proj(_O_V, _O_CQ)
    vw_lane = lax.broadcasted_iota(jnp.int32, vw.shape, 1) % LANES
    vw_o[...] = jnp.where(vw_lane == WIN_HEAD_DIM, 1.0, vw).astype(BF16)

    cm = cm_ref[...]
    sm = sm_ref[...]
    cqn = _rms(proj(_O_CQ, _O_CKV), gq_ref[...]).astype(BF16)
    y1 = _dot(cqn, wq_ref[...])
    scale = (MLA_QK ** -0.5) * LOG2E
    for h in range(MLA_HEADS):
        sl = slice(h * MLA_HEAD_PAD, (h + 1) * MLA_HEAD_PAD)
        qm_o[:, sl] = (rope(y1[:, sl], cm, sm) * scale).astype(BF16)

    ckvn = _rms(proj(_O_CKV, _O_KRP), gkv_ref[...]).astype(BF16)
    krp = rope(proj(_O_KRP, _O_SG), cm, sm)
    kk = _dot(ckvn, wk_ref[...])
    for h in range(MLA_HEADS):
        sl = slice(h * MLA_HEAD_PAD, (h + 1) * MLA_HEAD_PAD)
        km_o[:, sl] = (kk[:, sl] + krp).astype(BF16)
    vv = _dot(ckvn, wv_ref[...])
    lane = lax.broadcasted_iota(jnp.int32, vv.shape, 1) % MLA_HEAD_PAD
    vm_o[...] = jnp.where(lane == MLA_V, 1.0, vv).astype(BF16)

    sg = proj(_O_SG, _O_END)
    gl = 0.5 * sg * (1.0 + lax.erf(sg * (2.0 ** -0.5)))
    u_o[...] = gl[:, :GROUP_W].astype(BF16)
    v = gl[:, GROUP_W:]
    mu = jnp.mean(v, axis=-1, keepdims=True)
    vc = v - mu
    var = jnp.mean(vc * vc, axis=-1, keepdims=True)
    vn_o[...] = (vc * lax.rsqrt(var + EPS)).astype(BF16)


def _inproj(h, g, shift, scale, w_aug, tables, gq, gkv, wq, wk, wv, per_batch, tm):
    t, d = h.shape
    nmod = shift.shape[0]
    tpb = per_batch // tm
    cw, sw, cm, sm = tables

    def mod_idx(i):
        return ((i // tpb) if nmod > 1 else 0, 0, 0)

    tok = lambda w: pl.BlockSpec((tm, w), lambda i: (i, 0))
    tab = lambda w: pl.BlockSpec((tm, w), lambda i: (i % tpb, 0))
    in_specs = [tok(d), _full((1, d)), pl.BlockSpec((1, 1, d), mod_idx), pl.BlockSpec((1, 1, d), mod_idx),
                _full(w_aug.shape), tab(256), tab(256), tab(LANES), tab(LANES),
                _full((1, MLA_Q_RANK)), _full((1, MLA_KV_RANK)),
                _full(wq.shape), _full(wk.shape), _full(wv.shape)]
    widths = (256, 256, 128, 256, 512, 512, 512, 256, 256)
    return pl.pallas_call(
        _inproj_kernel,
        grid=(t // tm,),
        in_specs=in_specs,
        out_specs=[tok(w) for w in widths],
        out_shape=[jax.ShapeDtypeStruct((t, w), BF16) for w in widths],
        compiler_params=_cparams("parallel"),
        name="inproj",
    )(h, g.reshape(1, d), shift, scale, w_aug, cw, sw, cm, sm, gq.reshape(1, -1), gkv.reshape(1, -1), wq, wk, wv)


def _fft1_kernel(x_ref, mh_ref, ml_ref, tr_ref, ti_ref, or_ref, oi_ref):
    x = x_ref[0]
    z = _dot2(mh_ref[...], ml_ref[...], x)
    ar, ai = z[:FFT_N1], z[FFT_N1:]
    tr, ti = tr_ref[...], ti_ref[...]
    or_ref[0] = (ar * tr - ai * ti).astype(BF16)
    oi_ref[0] = (ar * ti + ai * tr).astype(BF16)


def _fft2_kernel(r_ref, i_ref, cs_ref, wch_ref, wcl_ref, wf_ref, o_ref, g_scr, *, nj, n2):
    cs = cs_ref[...]
    gw = o_ref.shape[-1]
    for j in range(nj):
        zr = _dot(cs, r_ref[0, j])
        zi = _dot(cs, i_ref[0, j])
        c_r, s_r = zr[:n2] + zr[n2:2 * n2], zr[2 * n2:3 * n2] + zr[3 * n2:]
        c_i, s_i = zi[:n2] + zi[n2:2 * n2], zi[2 * n2:3 * n2] + zi[3 * n2:]
        g_scr[j * n2:(j + 1) * n2, :gw] = (c_r + s_i).astype(BF16)
        g_scr[j * n2:(j + 1) * n2, gw:] = (c_i - s_r).astype(BF16)
    g = g_scr[...]
    y = _dot(g, wch_ref[...]) + _dot(g, wcl_ref[...])
    o = _dot(y.astype(BF16), wf_ref[...]).astype(BF16)
    for j in range(nj):
        o_ref[0, j] = o[j * n2:(j + 1) * n2]


def _dft_small_kernel(a_ref, ch_ref, cl_ref, sh_ref, sl_ref, bch_ref, bcl_ref, bsh_ref, bsl_ref, wf_ref, o_ref):
    a = a_ref[0]
    gr = _dot2(ch_ref[...], cl_ref[...], a).astype(BF16)
    gi = (-_dot2(sh_ref[...], sl_ref[...], a)).astype(BF16)
    y = _dot(gr, bch_ref[...]) + _dot(gr, bcl_ref[...]) + _dot(gi, bsh_ref[...]) + _dot(gi, bsl_ref[...])
    o_ref[0] = _dot(y.astype(BF16), wf_ref[...]).astype(BF16)


def _cos_sin(n):
    k = np.arange(n)
    ang = 2.0 * np.pi * ((k[:, None] * k[None, :]) % n) / n
    return np.cos(ang), np.sin(ang)


def _channel_dft(norm):
    c, s = _cos_sin(FNET_CH)
    bc = np.kron(np.eye(FNET_GROUPS), c) * norm
    bs = np.kron(np.eye(FNET_GROUPS), s) * norm
    return _split_hi_lo(bc) + _split_hi_lo(bs)


def _fourier_latent(a, w_f):
    b, n, gw = a.shape
    n2 = n // FFT_N1
    c1, s1 = _cos_sin(FFT_N1)
    mh, ml = _split_hi_lo(np.concatenate([c1, -s1], axis=0))
    k1 = np.arange(FFT_N1)[:, None]
    t2 = np.arange(n2)[None, :]
    ang = 2.0 * np.pi * ((k1 * t2) % n) / n
    tr = jnp.repeat(jnp.asarray(np.cos(ang), F32), gw, axis=1)
    ti = jnp.repeat(jnp.asarray(-np.sin(ang), F32), gw, axis=1)
    cols = n2 * gw
    tc = min(cols, 4096)
    x2 = a.reshape(b, FFT_N1, cols)
    blk = pl.BlockSpec((1, FFT_N1, tc), lambda bi, ci: (bi, 0, ci))
    tblk = pl.BlockSpec((FFT_N1, tc), lambda bi, ci: (0, ci))
    o_r, o_i = pl.pallas_call(
        _fft1_kernel,
        grid=(b, cols // tc),
        in_specs=[blk, _full(mh.shape), _full(ml.shape), tblk, tblk],
        out_specs=[blk, blk],
        out_shape=[jax.ShapeDtypeStruct((b, FFT_N1, cols), BF16)] * 2,
        compiler_params=_cparams("parallel", "parallel"),
        name="fft_stage1",
    )(x2, mh, ml, tr, ti)

    c2, s2 = _cos_sin(n2)
    cs = jnp.concatenate(_split_hi_lo(c2) + _split_hi_lo(s2), axis=0)
    bch, bcl, bsh, bsl = _channel_dft((n * FNET_CH) ** -0.5)
    wch = jnp.concatenate([bch, bsh], axis=0)
    wcl = jnp.concatenate([bcl, bsl], axis=0)
    nj = 8
    r4 = o_r.reshape(b, FFT_N1, n2, gw)
    i4 = o_i.reshape(b, FFT_N1, n2, gw)
    blk4 = pl.BlockSpec((1, nj, n2, gw), lambda bi, ki: (bi, ki, 0, 0))
    out = pl.pallas_call(
        functools.partial(_fft2_kernel, nj=nj, n2=n2),
        grid=(b, FFT_N1 // nj),
        in_specs=[blk4, blk4, _full(cs.shape), _full(wch.shape), _full(wcl.shape), _full(w_f.shape)],
        out_specs=blk4,
        out_shape=jax.ShapeDtypeStruct((b, FFT_N1, n2, gw), BF16),
        scratch_shapes=[pltpu.VMEM((nj * n2, 2 * gw), BF16)],
        compiler_params=_cparams("parallel", "parallel"),
        name="fft_stage2",
    )(r4, i4, cs, wch, wcl, w_f)
    return jnp.swapaxes(out, 1, 2).reshape(b, n, gw)


def _fourier_context(a, w_f):
    b, n, gw = a.shape
    c, s = _cos_sin(n)
    consts = _split_hi_lo(c) + _split_hi_lo(s) + _channel_dft((n * FNET_CH) ** -0.5)
    blk = pl.BlockSpec((1, n, gw), lambda bi: (bi, 0, 0))
    return pl.pallas_call(
        _dft_small_kernel,
        grid=(b,),
        in_specs=[blk] + [_full(m.shape) for m in consts] + [_full(w_f.shape)],
        out_specs=blk,
        out_shape=jax.ShapeDtypeStruct((b, n, gw), BF16),
        compiler_params=_cparams("parallel"),
        name="dft_context",
    )(a, *consts, w_f)


def _win_heads(q_blk, k_win, v_win, bias, kc, vc, sink_ref):
    rows = lax.broadcasted_iota(jnp.int32, (WIN_KV_HEADS * BLOCK, 1), 0)
    outs = [None] * WIN_HEADS
    d = WIN_HEAD_DIM
    for hk in range(WIN_KV_HEADS):
        h0 = 2 * hk
        q2 = jnp.concatenate([q_blk[:, h0 * d:(h0 + 1) * d], q_blk[:, (h0 + 1) * d:(h0 + 2) * d]], axis=0)
        ksl = slice(hk * d, (hk + 1) * d)
        vsl = slice(hk * LANES, (hk + 1) * LANES)
        sink = jnp.where(rows < BLOCK, sink_ref[h0] * LOG2E, sink_ref[h0 + 1] * LOG2E)
        s_c = _dot_nt(q2, kc[:, ksl])
        m = jnp.maximum(jnp.max(s_c, axis=-1, keepdims=True), sink)
        if k_win is not None:
            s_w = _dot_nt(q2, k_win[:, ksl]) + bias
            m = jnp.maximum(m, jnp.max(s_w, axis=-1, keepdims=True))
        r = _dot(jnp.exp2(s_c - m).astype(BF16), vc[:, vsl])
        if k_win is not None:
            r = r + _dot(jnp.exp2(s_w - m).astype(BF16), v_win[:, vsl])
        o = r[:, :d] / (r[:, d:d + 1] + jnp.exp2(sink - m))
        outs[h0] = o[:BLOCK]
        outs[h0 + 1] = o[BLOCK:]
    return jnp.concatenate(outs, axis=1)


def _win_kernel(sink_ref, q_ref, kp_ref, k_ref, kn_ref, vp_ref, v_ref, vn_ref, kc_ref, vc_ref, o_ref, *, nblk):
    i = pl.program_id(1)
    kfull = jnp.concatenate([kp_ref[0], k_ref[0], kn_ref[0]], axis=0)
    vfull = jnp.concatenate([vp_ref[0], v_ref[0], vn_ref[0]], axis=0)
    kc = kc_ref[0]
    vc = vc_ref[0]
    shape = (WIN_KV_HEADS * BLOCK, 3 * BLOCK)
    qi = lax.broadcasted_iota(jnp.int32, shape, 0) % BLOCK
    kj = lax.broadcasted_iota(jnp.int32, shape, 1)
    band = jnp.abs(kj - BLOCK - qi) <= WINDOW
    has_prev = jnp.logical_or(kj >= BLOCK, i > 0)
    has_next = jnp.logical_or(kj < 2 * BLOCK, i < pl.num_programs(1) - 1)
    bias_mid = jnp.where(band, 0.0, NEG_INF)
    for j in range(nblk):
        ok = band
        if j == 0:
            ok = ok & has_prev
        if j == nblk - 1:
            ok = ok & has_next
        bias = bias_mid if ok is band else jnp.where(ok, 0.0, NEG_INF)
        o = _win_heads(q_ref[0, j * BLOCK:(j + 1) * BLOCK, :], kfull[j * BLOCK:(j + 3) * BLOCK],
                       vfull[j * BLOCK:(j + 3) * BLOCK], bias, kc, vc, sink_ref)
        o_ref[0, j * BLOCK:(j + 1) * BLOCK, :] = o.astype(BF16)


def _winctx_kernel(sink_ref, q_ref, kc_ref, vc_ref, o_ref, *, nblk):
    kc = kc_ref[0]
    vc = vc_ref[0]
    for j in range(nblk):
        o = _win_heads(q_ref[0, j * BLOCK:(j + 1) * BLOCK, :], None, None, None, kc, vc, sink_ref)
        o_ref[0, j * BLOCK:(j + 1) * BLOCK, :] = o.astype(BF16)


def _window_latent(q, k, v, kc, vc, sink):
    b, n, _ = q.shape
    lc = kc.shape[1]
    nblk = min(8, n // BLOCK)
    tq = nblk * BLOCK
    nb = n // BLOCK
    kw = WIN_KV_HEADS * WIN_HEAD_DIM
    vw = WIN_KV_HEADS * LANES
    cur = lambda w: pl.BlockSpec((1, tq, w), lambda bi, i: (bi, i, 0))
    prev = lambda w: pl.BlockSpec((1, BLOCK, w), lambda bi, i: (bi, jnp.maximum(i * nblk - 1, 0), 0))
    nxt = lambda w: pl.BlockSpec((1, BLOCK, w), lambda bi, i: (bi, jnp.minimum((i + 1) * nblk, nb - 1), 0))
    ctx = lambda w: pl.BlockSpec((1, lc, w), lambda bi, i: (bi, 0, 0))
    smem = pl.BlockSpec(memory_space=pltpu.SMEM)
    return pl.pallas_call(
        functools.partial(_win_kernel, nblk=nblk),
        grid=(b, n // tq),
        in_specs=[smem, cur(256), prev(kw), cur(kw), nxt(kw), prev(vw), cur(vw), nxt(vw), ctx(kw), ctx(vw)],
        out_specs=cur(256),
        out_shape=jax.ShapeDtypeStruct((b, n, 256), BF16),
        compiler_params=_cparams("parallel", "arbitrary"),
        name="window_attn",
    )(sink, q, k, k, k, v, v, v, kc, vc)


def _window_context(q, kc, vc, sink):
    b, lc, _ = q.shape
    kvw = WIN_KV_HEADS * WIN_HEAD_DIM
    blk = lambda w: pl.BlockSpec((1, lc, w), lambda bi: (bi, 0, 0))
    smem = pl.BlockSpec(memory_space=pltpu.SMEM)
    return pl.pallas_call(
        functools.partial(_winctx_kernel, nblk=lc // BLOCK),
        grid=(b,),
        in_specs=[smem, blk(256), blk(kvw), blk(WIN_KV_HEADS * LANES)],
        out_specs=blk(256),
        out_shape=jax.ShapeDtypeStruct((b, lc, 256), BF16),
        compiler_params=_cparams("parallel"),
        name="window_attn_ctx",
    )(sink, q, kc, vc)


def _mla_reduce(state, s, v):
    m, acc = state
    m_new = jnp.maximum(m, jnp.max(s, axis=-1, keepdims=True))
    alpha = jnp.exp2(m - m_new)
    p = jnp.exp2(s - m_new).astype(BF16)
    return m_new, alpha * acc + _dot(p, v)


def _mla_finish(acc):
    return (acc[:, :MLA_V] / acc[:, MLA_V:MLA_V + 1]).astype(BF16)


def _mla_kernel(q_ref, k_ref, v_ref, kc_ref, vc_ref, o_ref, s_scr, sc_scr, *, tk, nk):
    tq = q_ref.shape[1]
    hsl = [slice(h * MLA_HEAD_PAD, (h + 1) * MLA_HEAD_PAD) for h in range(2)]
    qs = [q_ref[0, :, sl] for sl in hsl]

    def scores(c, slot):
        off = pl.multiple_of(c * tk, tk)
        for h in range(2):
            s_scr[slot, h] = _dot_nt(qs[h], k_ref[0, pl.ds(off, tk), hsl[h]])

    def reduce(c, slot, states):
        off = pl.multiple_of(c * tk, tk)
        return tuple(_mla_reduce(states[h], s_scr[slot, h], v_ref[0, pl.ds(off, tk), hsl[h]]) for h in range(2))

    scores(0, 0)

    def body(i, states):
        scores(2 * i + 1, 1)
        states = reduce(2 * i, 0, states)
        scores(2 * i + 2, 0)
        return reduce(2 * i + 1, 1, states)

    init = tuple((jnp.full((tq, 1), NEG_INF, F32), jnp.zeros((tq, MLA_HEAD_PAD), F32)) for _ in range(2))
    states = init
    for pair in range(nk // 2 - 1):
        states = body(pair, states)
    scores(nk - 1, 1)
    states = reduce(nk - 2, 0, states)
    for h in range(2):
        sc_scr[h] = _dot_nt(qs[h], kc_ref[0, :, hsl[h]])
    states = reduce(nk - 1, 1, states)
    for h in range(2):
        _, acc = _mla_reduce(states[h], sc_scr[h], vc_ref[0, :, hsl[h]])
        o_ref[0, :, h * MLA_V:(h + 1) * MLA_V] = _mla_finish(acc)


def _mla_small_kernel(q_ref, k_ref, v_ref, o_ref):
    tq = q_ref.shape[1]
    for h in range(2):
        sl = slice(h * MLA_HEAD_PAD, (h + 1) * MLA_HEAD_PAD)
        init = (jnp.full((tq, 1), NEG_INF, F32), jnp.zeros((tq, MLA_HEAD_PAD), F32))
        _, acc = _mla_reduce(init, _dot_nt(q_ref[0, :, sl], k_ref[0, :, sl]), v_ref[0, :, sl])
        o_ref[0, :, h * MLA_V:(h + 1) * MLA_V] = _mla_finish(acc)


def _mla_attention(q, k, v, kc, vc):
    b, nq, _ = q.shape
    nkeys = k.shape[1]
    lc = kc.shape[1]
    tq = min(1024, nq)
    tk = min(1024, nkeys // 2)
    assert nkeys % (2 * tk) == 0 and nq % tq == 0
    w2 = 2 * MLA_HEAD_PAD
    res = lambda n: pl.BlockSpec((1, n, w2), lambda bi, h, i: (bi, 0, h))
    return pl.pallas_call(
        functools.partial(_mla_kernel, tk=tk, nk=nkeys // tk),
        grid=(b, MLA_HEADS // 2, nq // tq),
        in_specs=[pl.BlockSpec((1, tq, w2), lambda bi, h, i: (bi, i, h)), res(nkeys), res(nkeys), res(lc), res(lc)],
        out_specs=pl.BlockSpec((1, tq, 2 * MLA_V), lambda bi, h, i: (bi, i, h)),
        out_shape=jax.ShapeDtypeStruct((b, nq, MLA_HEADS * MLA_V), BF16),
        scratch_shapes=[pltpu.VMEM((2, 2, tq, tk), F32), pltpu.VMEM((2, tq, lc), F32)],
        compiler_params=_cparams("parallel", "parallel", "arbitrary"),
        name="mla_attn",
    )(q, k, v, kc, vc)


def _mla_attention_small(q, k, v):
    b, n, _ = q.shape
    w2 = 2 * MLA_HEAD_PAD
    blk = pl.BlockSpec((1, n, w2), lambda bi, h: (bi, 0, h))
    return pl.pallas_call(
        _mla_small_kernel,
        grid=(b, MLA_HEADS // 2),
        in_specs=[blk, blk, blk],
        out_specs=pl.BlockSpec((1, n, 2 * MLA_V), lambda bi, h: (bi, 0, h)),
        out_shape=jax.ShapeDtypeStruct((b, n, MLA_HEADS * MLA_V), BF16),
        compiler_params=_cparams("parallel", "parallel"),
        name="mla_attn_ctx",
    )(q, k, v)


def _sgu_kernel(u_ref, v_ref, w_ref, b_ref, o_ref, *, nchunk):
    lane = lax.broadcasted_iota(jnp.int32, (SGU_CHUNK, GROUP_W), 1) // SGU_CH
    bias = b_ref[...]
    for c in range(nchunk):
        sl = slice(c * SGU_CHUNK, (c + 1) * SGU_CHUNK)
        v = v_ref[0, sl, :]
        s = bias
        for g in range(SGU_GROUPS):
            s = s + _dot(w_ref[g], jnp.where(lane == g, v, jnp.zeros_like(v)))
        o_ref[0, sl, :] = (u_ref[0, sl, :].astype(F32) * s).astype(BF16)


def _sgu(u, vn, w_s, bias_full):
    b, n, gw = u.shape
    nchunk = min(8, n // SGU_CHUNK)
    tn = nchunk * SGU_CHUNK
    blk = pl.BlockSpec((1, tn, gw), lambda bi, i: (bi, i, 0))
    return pl.pallas_call(
        functools.partial(_sgu_kernel, nchunk=nchunk),
        grid=(b, n // tn),
        in_specs=[blk, blk, _full(w_s.shape), _full(bias_full.shape)],
        out_specs=blk,
        out_shape=jax.ShapeDtypeStruct((b, n, gw), BF16),
        compiler_params=_cparams("parallel", "parallel"),
        name="sgu",
    )(u, vn, w_s, bias_full)


def _outproj_kernel(*refs, with_router):
    if with_router:
        (p0, p1, p2, p3, w_ref, h_ref, gate_ref, gpost_ref, gpre_ref, sh_ref, sc_ref, wr_ref,
         hn_o, fx_o, comb_o) = refs
    else:
        p0, p1, p2, p3, w_ref, h_ref, gate_ref, gpost_ref, gpre_ref, sh_ref, sc_ref, hn_o, fx_o = refs
    mx = None
    for idx, p in enumerate((p0, p1, p2, p3)):
        t = _dot(p[...], w_ref[idx * GROUP_W:(idx + 1) * GROUP_W, :])
        mx = t if mx is None else mx + t
    hn = h_ref[...] + gate_ref[0] * _rms(mx, gpost_ref[...])
    hn_o[...] = hn
    fx = _rms(hn, gpre_ref[...]) * (1.0 + sc_ref[0]) + sh_ref[0]
    fx_o[...] = fx.astype(fx_o.dtype)
    if with_router:
        fx_hi = fx.astype(BF16)
        fx_lo = (fx - fx_hi.astype(F32)).astype(BF16)
        logits = _dot(fx_hi, wr_ref[0]) + (_dot(fx_lo, wr_ref[0]) + _dot(fx_hi, wr_ref[1]))
        lane = lax.broadcasted_iota(jnp.int32, logits.shape, 1).astype(F32)
        lg = jnp.where(lane < N_EXPERTS, logits, -jnp.inf)
        m1 = jnp.max(lg, axis=-1, keepdims=True)
        i1 = jnp.min(jnp.where(lg == m1, lane, float(LANES)), axis=-1, keepdims=True)
        lg2 = jnp.where(lane == i1, -jnp.inf, lg)
        m2 = jnp.max(lg2, axis=-1, keepdims=True)
        i2 = jnp.min(jnp.where(lg2 == m2, lane, float(LANES)), axis=-1, keepdims=True)
        e2 = jnp.exp(m2 - m1)
        den = 1.0 + e2
        flag = (lane == i1 + N_EXPERTS) | (lane == i2 + N_EXPERTS)
        comb_o[...] = (jnp.where(lane == i1, 1.0 / den, 0.0) + jnp.where(lane == i2, e2 / den, 0.0)
                       + jnp.where(flag, 1.0, 0.0))


def _outproj(parts, w_out, h, gate, g_post, g_pre, shift, scale, per_batch, tm, w_router=None):
    t, d = h.shape
    nmod = gate.shape[0]
    tpb = per_batch // tm
    with_router = w_router is not None

    def mod_idx(i):
        return ((i // tpb) if nmod > 1 else 0, 0, 0)

    tok = lambda w: pl.BlockSpec((tm, w), lambda i: (i, 0))
    mod = pl.BlockSpec((1, 1, d), mod_idx)
    in_specs = [tok(GROUP_W)] * 4 + [_full(w_out.shape), tok(d), mod, _full((1, d)), _full((1, d)), mod, mod]
    args = list(parts) + [w_out, h, gate, g_post.reshape(1, d), g_pre.reshape(1, d), shift, scale]
    out_specs = [tok(d), tok(d)]
    out_shape = [jax.ShapeDtypeStruct((t, d), F32), jax.ShapeDtypeStruct((t, d), F32 if with_router else BF16)]
    if with_router:
        in_specs.append(_full(w_router.shape))
        args.append(w_router)
        out_specs.append(tok(LANES))
        out_shape.append(jax.ShapeDtypeStruct((t, LANES), F32))
    return pl.pallas_call(
        functools.partial(_outproj_kernel, with_router=with_router),
        grid=(t // tm,),
        in_specs=in_specs,
        out_specs=out_specs,
        out_shape=out_shape,
        compiler_params=_cparams("parallel"),
        name="outproj",
    )(*args)


def _swiglu_partial(x, wg, wu, wd, tf):
    y = None
    for lo in range(0, tf, FFN_SUB):
        hi = lo + FFN_SUB
        g = _dot(x, wg(lo, hi))
        u = _dot(x, wu(lo, hi))
        hid = (g / (1.0 + jnp.exp(-g)) * u).astype(BF16)
        t = _dot(hid, wd(lo, hi))
        y = t if y is None else y + t
    return y


def _ffn_kernel(x_ref, wg_ref, wu_ref, wd_ref, h_ref, gate_ref, gpost_ref, o_ref):
    f = pl.program_id(1)
    last = pl.num_programs(1) - 1
    y = _swiglu_partial(x_ref[...], lambda lo, hi: wg_ref[:, lo:hi], lambda lo, hi: wu_ref[:, lo:hi],
                        lambda lo, hi: wd_ref[lo:hi, :], wd_ref.shape[0])

    @pl.when(f == 0)
    def _():
        o_ref[...] = y

    @pl.when((f > 0) & (f < last))
    def _():
        o_ref[...] += y

    @pl.when(f == last)
    def _():
        o_ref[...] = h_ref[...] + gate_ref[0] * _rms(o_ref[...] + y, gpost_ref[...])


def _ffn_dense(fx, wg, wu, wd, h, gate, g_post, per_batch, tm, tf):
    t, d = h.shape
    dff = wg.shape[1]
    nmod = gate.shape[0]
    tpb = per_batch // tm
    assert dff // tf >= 2 and tf % FFN_SUB == 0

    def mod_idx(i, f):
        return ((i // tpb) if nmod > 1 else 0, 0, 0)

    tok = pl.BlockSpec((tm, d), lambda i, f: (i, 0))
    return pl.pallas_call(
        _ffn_kernel,
        grid=(t // tm, dff // tf),
        in_specs=[tok, pl.BlockSpec((d, tf), lambda i, f: (0, f)), pl.BlockSpec((d, tf), lambda i, f: (0, f)),
                  pl.BlockSpec((tf, d), lambda i, f: (f, 0)), tok, pl.BlockSpec((1, 1, d), mod_idx),
                  pl.BlockSpec((1, d), lambda i, f: (0, 0))],
        out_specs=tok,
        out_shape=jax.ShapeDtypeStruct((t, d), F32),
        compiler_params=_cparams("parallel", "arbitrary"),
        name="ffn_dense",
    )(fx, wg, wu, wd, h, gate, g_post.reshape(1, d))


def _route(comb, tile):
    t = comb.shape[0]
    sel = (comb[:, N_EXPERTS:2 * N_EXPERTS] > 0.5).astype(jnp.int32)
    rank = jnp.cumsum(sel, axis=0) - sel
    padded = (jnp.sum(sel, axis=0) + tile - 1) // tile * tile
    ends = jnp.cumsum(padded)
    pos_e = (ends - padded)[None, :] + rank
    e_id = jnp.arange(N_EXPERTS, dtype=jnp.int32)[None, :]
    e_lo = jnp.min(jnp.where(sel > 0, e_id, N_EXPERTS - 1), axis=1)
    e_hi = jnp.max(sel * e_id, axis=1)
    take = lambda a, e: jnp.take_along_axis(a, e[:, None], axis=1)[:, 0]
    pos = jnp.stack([take(pos_e, e_lo), take(pos_e, e_hi)], axis=1).reshape(2 * t).astype(jnp.int32)
    w = comb[:, :N_EXPERTS]
    wts = jnp.pad(jnp.stack([take(w, e_lo), take(w, e_hi)], axis=1), ((0, 0), (0, LANES - 2)))
    ntiles = -(-2 * t // tile) + N_EXPERTS
    nact = (ends[-1] // tile).astype(jnp.int32)
    tile_id = jnp.arange(ntiles, dtype=jnp.int32)
    te = jnp.minimum(jnp.searchsorted(ends, tile_id * tile, side="right"), N_EXPERTS - 1).astype(jnp.int32)
    te = jnp.where(tile_id < nact, te, te[nact - 1])
    return pos, wts, te, nact.reshape(1), ntiles


def _dispatch_kernel(pos_ref, x_ref, xs_in_ref, xs_ref, sem, *, tm):
    del xs_in_ref
    base = pl.program_id(0) * tm

    def row_copy(t, p):
        return pltpu.make_async_copy(x_ref.at[pl.ds(t, 1)], xs_ref.at[pl.ds(p, 1)], sem)

    def issue(t, carry):
        row_copy(t, pos_ref[2 * (base + t)]).start(priority=0)
        row_copy(t, pos_ref[2 * (base + t) + 1]).start(priority=1)
        return carry

    def drain(t, carry):
        row_copy(0, 0).wait()
        row_copy(0, 0).wait()
        return carry

    lax.fori_loop(0, tm, issue, 0, unroll=8)
    lax.fori_loop(0, tm, drain, 0, unroll=8)


def _gmm_kernel(te_ref, nact_ref, x_ref, wg_ref, wu_ref, wd_ref, o_ref):
    del te_ref
    f = pl.program_id(1)
    active = pl.program_id(0) < nact_ref[0]

    @pl.when(jnp.logical_not(active) & (f == 0))
    def _():
        o_ref[...] = jnp.zeros_like(o_ref)

    @pl.when(active)
    def _():
        y = _swiglu_partial(x_ref[...].astype(BF16), lambda lo, hi: wg_ref[0, :, lo:hi],
                            lambda lo, hi: wu_ref[0, :, lo:hi], lambda lo, hi: wd_ref[0, lo:hi, :], wd_ref.shape[1])

        @pl.when(f == 0)
        def _():
            o_ref[...] = y

        @pl.when(f > 0)
        def _():
            o_ref[...] += y


def _combine_kernel(pos_ref, ys_ref, w_ref, h_ref, gate_ref, gpost_ref, o_ref, buf, sem, *, tm):
    step = pl.program_id(0)
    slot = step % 2

    def row_copy(slot_, choice, t, p):
        return pltpu.make_async_copy(ys_ref.at[pl.ds(p, 1)], buf.at[slot_, choice, pl.ds(t, 1)], sem.at[slot_])

    def start_gathers(step_, slot_):
        base = step_ * tm

        def issue(t, carry):
            row_copy(slot_, 0, t, pos_ref[2 * (base + t)]).start(priority=0)
            row_copy(slot_, 1, t, pos_ref[2 * (base + t) + 1]).start(priority=1)
            return carry

        lax.fori_loop(0, tm, issue, 0, unroll=8)

    @pl.when(step == 0)
    def _():
        start_gathers(0, 0)

    @pl.when(step + 1 < pl.num_programs(0))
    def _():
        start_gathers(step + 1, 1 - slot)

    def drain(t, carry):
        row_copy(slot, 0, 0, 0).wait()
        row_copy(slot, 1, 0, 0).wait()
        return carry

    lax.fori_loop(0, tm, drain, 0, unroll=8)
    w = w_ref[...]
    y = w[:, 0:1] * buf[slot, 0] + w[:, 1:2] * buf[slot, 1]
    o_ref[...] = h_ref[...] + gate_ref[0] * _rms(y, gpost_ref[...])


def _ffn_moe(fx, comb, wg, wu, wd, h, gate, g_post, per_batch, tf):
    t, d = h.shape
    dff = wg.shape[2]
    nf = dff // tf
    tile = MOE_TILE
    pos, wts, te, nact, ntiles = _route(comb, tile)
    ts = ntiles * tile
    any_spec = pl.BlockSpec(memory_space=pl.ANY)

    tm_d = min(512, per_batch)
    xs = pl.pallas_call(
        functools.partial(_dispatch_kernel, tm=tm_d),
        grid_spec=pltpu.PrefetchScalarGridSpec(
            num_scalar_prefetch=1, grid=(t // tm_d,),
            in_specs=[pl.BlockSpec((tm_d, d), lambda i, p: (i, 0)), any_spec],
            out_specs=any_spec,
            scratch_shapes=[pltpu.SemaphoreType.DMA(())]),
        out_shape=jax.ShapeDtypeStruct((ts, d), F32),
        input_output_aliases={2: 0},
        compiler_params=_cparams("arbitrary"),
        name="moe_dispatch",
    )(pos, fx, jnp.zeros((ts, d), F32))

    row = lambda i, f, te_r, na_r: (jnp.minimum(i, na_r[0] - 1), 0)
    fidx = lambda i, f, na_r: jnp.where(i < na_r[0], f, nf - 1)
    ys = pl.pallas_call(
        _gmm_kernel,
        grid_spec=pltpu.PrefetchScalarGridSpec(
            num_scalar_prefetch=2, grid=(ntiles, nf),
            in_specs=[pl.BlockSpec((tile, d), row),
                      pl.BlockSpec((1, d, tf), lambda i, f, te_r, na_r: (te_r[i], 0, fidx(i, f, na_r))),
                      pl.BlockSpec((1, d, tf), lambda i, f, te_r, na_r: (te_r[i], 0, fidx(i, f, na_r))),
                      pl.BlockSpec((1, tf, d), lambda i, f, te_r, na_r: (te_r[i], fidx(i, f, na_r), 0))],
            out_specs=pl.BlockSpec((tile, d), lambda i, f, te_r, na_r: (i, 0))),
        out_shape=jax.ShapeDtypeStruct((ts, d), F32),
        compiler_params=_cparams("arbitrary", "arbitrary"),
        name="moe_experts",
    )(te, nact, xs, wg, wu, wd)

    tm_c = min(512, per_batch)
    nmod = gate.shape[0]
    tpb = per_batch // tm_c
    tok = lambda w: pl.BlockSpec((tm_c, w), lambda i, p: (i, 0))
    return pl.pallas_call(
        functools.partial(_combine_kernel, tm=tm_c),
        grid_spec=pltpu.PrefetchScalarGridSpec(
            num_scalar_prefetch=1, grid=(t // tm_c,),
            in_specs=[any_spec, tok(LANES), tok(d),
                      pl.BlockSpec((1, 1, d), lambda i, p: ((i // tpb) if nmod > 1 else 0, 0, 0)),
                      pl.BlockSpec((1, d), lambda i, p: (0, 0))],
            out_specs=tok(d),
            scratch_shapes=[pltpu.VMEM((2, 2, tm_c, d), F32), pltpu.SemaphoreType.DMA((2,))]),
        out_shape=jax.ShapeDtypeStruct((t, d), F32),
        compiler_params=_cparams("arbitrary"),
        name="moe_combine",
    )(pos, ys, wts, h, gate, g_post.reshape(1, d))


def _rope_tables(n, rot_dim):
    n_freq = rot_dim // 4
    inv_freq = ROPE_THETA ** (-jnp.arange(n_freq, dtype=F32) / n_freq)
    rows = n // GRID_W
    row = jnp.repeat(jnp.arange(rows, dtype=F32), GRID_W)
    col = jnp.tile(jnp.arange(GRID_W, dtype=F32), rows)
    ang = jnp.concatenate([row[:, None] * inv_freq, col[:, None] * inv_freq], axis=-1)
    sin = jnp.sin(ang)
    return jnp.repeat(jnp.cos(ang), 2, axis=-1), jnp.stack([-sin, sin], axis=-1).reshape(n, rot_dim)


def _latent_tables(n):
    cw, sw = _rope_tables(n, WIN_HEAD_DIM)
    cm, sm = _rope_tables(n, MLA_ROPE)
    one = jnp.ones((n, MLA_NOPE), F32)
    zero_n = jnp.zeros((n, MLA_NOPE), F32)
    zero_p = jnp.zeros((n, MLA_HEAD_PAD - MLA_QK), F32)
    return (jnp.tile(cw, (1, WIN_HEADS)), jnp.tile(sw, (1, WIN_HEADS)),
            jnp.concatenate([one, cm, zero_p], axis=1), jnp.concatenate([zero_n, sm, zero_p], axis=1))


def _context_tables(lc):
    one_w = jnp.ones((lc, WIN_HEADS * WIN_HEAD_DIM), F32)
    cm = jnp.concatenate([jnp.ones((lc, MLA_QK), F32), jnp.zeros((lc, MLA_HEAD_PAD - MLA_QK), F32)], axis=1)
    return one_w, jnp.zeros_like(one_w), cm, jnp.zeros_like(cm)


def _prep_layer_weights(w_in, mla_w_uq, mla_w_ukv):
    d = w_in.shape[0]
    o = np.cumsum((0, 256, 256, 128, 128, 256, 128, 32, 512))
    a, q, k, v, cq, ckv, kr, sg = [w_in[:, o[i]:o[i + 1]] for i in range(8)]
    zl = jnp.zeros((d, MLA_NOPE), w_in.dtype)
    zr = jnp.zeros((d, MLA_HEAD_PAD - MLA_QK), w_in.dtype)
    krp = jnp.concatenate([zl, kr, zr], axis=1)
    zh = jnp.zeros((d, LANES - WIN_HEAD_DIM), w_in.dtype)
    vp = jnp.concatenate([v[:, :WIN_HEAD_DIM], zh, v[:, WIN_HEAD_DIM:], zh], axis=1)
    w_aug = jnp.concatenate([a, q, k, vp, cq, ckv, krp, sg], axis=1).astype(BF16)

    wq = mla_w_uq.reshape(MLA_Q_RANK, MLA_HEADS, MLA_QK)
    zq = jnp.zeros((MLA_Q_RANK, MLA_HEADS, MLA_HEAD_PAD - MLA_QK), wq.dtype)
    wq = jnp.concatenate([wq, zq], axis=-1).reshape(MLA_Q_RANK, MLA_HEADS * MLA_HEAD_PAD).astype(BF16)

    wkv = mla_w_ukv.reshape(MLA_KV_RANK, MLA_HEADS, MLA_NOPE + MLA_V)
    zk = jnp.zeros((MLA_KV_RANK, MLA_HEADS, MLA_HEAD_PAD - MLA_NOPE), wkv.dtype)
    wk = jnp.concatenate([wkv[..., :MLA_NOPE], zk], axis=-1).reshape(MLA_KV_RANK, MLA_HEADS * MLA_HEAD_PAD).astype(BF16)
    zv = jnp.zeros((MLA_KV_RANK, MLA_HEADS, MLA_HEAD_PAD - MLA_V), wkv.dtype)
    wv = jnp.concatenate([wkv[..., MLA_NOPE:], zv], axis=-1).reshape(MLA_KV_RANK, MLA_HEADS * MLA_HEAD_PAD).astype(BF16)
    return w_aug, wq, wk, wv


def _token_tile(per_batch, pref):
    return min(pref, per_batch)


def kernel(x, c, ctx, c_ctx, w_mod, b_mod, g_pre_mix, g_post_mix, g_pre_ffn, g_post_ffn, w_in, w_out, w_fourier, win_sink, mla_g_q, mla_g_kv, mla_w_uq, mla_w_ukv, sgu_w, sgu_b, ffn_w_gate, ffn_w_up, ffn_w_down, moe_w_router, moe_w_gate, moe_w_up, moe_w_down):
    b, n, d = x.shape
    lc = ctx.shape[1]
    depth = w_mod.shape[0]
    tx, tcx = b * n, b * lc
    tm_x = _token_tile(n, 512)
    tm_c = _token_tile(lc, 512)
    tab_x = _latent_tables(n)
    tab_c = _context_tables(lc)
    cond = jnp.concatenate([c, c_ctx[None, :], jnp.zeros((8 - (b + 1) % 8 if (b + 1) % 8 else 0, d), F32)], axis=0)

    hx = x.reshape(tx, d)
    hc = ctx.reshape(tcx, d)
    for layer in range(depth):
        last = layer == depth - 1
        mod = _modulation(cond, w_mod[layer], b_mod[layer])
        mx6 = mod[:b].reshape(b, 1, 6, d)
        mc6 = mod[b:b + 1].reshape(1, 1, 6, d)
        sh_a, sc_a, gt_a, sh_f, sc_f, gt_f = [mx6[:, :, i] for i in range(6)]
        csh_a, csc_a, cgt_a, csh_f, csc_f, cgt_f = [mc6[:, :, i] for i in range(6)]

        w_aug, wq, wk, wv = _prep_layer_weights(w_in[layer], mla_w_uq[layer], mla_w_ukv[layer])
        w_f = w_fourier[layer].astype(BF16)
        w_o = w_out[layer].astype(BF16)
        w_s = sgu_w[layer].astype(BF16)
        bias_full = jnp.repeat(sgu_b[layer].T, SGU_CH, axis=1)
        sink = win_sink[layer]
        gq, gkv = mla_g_q[layer], mla_g_kv[layer]

        px = _inproj(hx, g_pre_mix[layer], sh_a, sc_a, w_aug, tab_x, gq, gkv, wq, wk, wv, n, tm_x)
        pc = _inproj(hc, g_pre_mix[layer], csh_a, csc_a, w_aug, tab_c, gq, gkv, wq, wk, wv, lc, tm_c)
        a_x, qw_x, kw_x, vw_x, qm_x, km_x, vm_x, u_x, vn_x = [t.reshape(b, n, -1) for t in px]
        a_c, qw_c, kw_c, vw_c, qm_c, km_c, vm_c, u_c, vn_c = [t.reshape(b, lc, -1) for t in pc]

        four_x = _fourier_latent(a_x, w_f)
        win_x = _window_latent(qw_x, kw_x, vw_x, kw_c, vw_c, sink)
        mla_x = _mla_attention(qm_x, km_x, vm_x, km_c, vm_c)
        sgu_x = _sgu(u_x, vn_x, w_s, bias_full)
        parts_x = [t.reshape(tx, GROUP_W) for t in (four_x, win_x, mla_x, sgu_x)]

        i = layer // 2
        dense = layer % 2 == 0
        dff = ffn_w_gate.shape[-1]
        tf = dff // 2 if (dff // 2) % FFN_SUB == 0 else FFN_SUB
        w_router = None
        if not dense:
            w_r = jnp.concatenate([moe_w_router[i], jnp.zeros((d, LANES - N_EXPERTS), F32)], axis=1)
            w_r_hi = w_r.astype(BF16)
            w_router = jnp.stack([w_r_hi, (w_r - w_r_hi.astype(F32)).astype(BF16)])
        res = _outproj(parts_x, w_o, hx, gt_a, g_post_mix[layer], g_pre_ffn[layer], sh_f, sc_f, n, tm_x, w_router)
        if dense:
            wg, wu, wd = ffn_w_gate[i].astype(BF16), ffn_w_up[i].astype(BF16), ffn_w_down[i].astype(BF16)
            hx = _ffn_dense(res[1], wg, wu, wd, res[0], gt_f, g_post_ffn[layer], n, min(1024, n), tf)
        else:
            wg, wu, wd = moe_w_gate[i].astype(BF16), moe_w_up[i].astype(BF16), moe_w_down[i].astype(BF16)
            hx = _ffn_moe(res[1], res[2], wg, wu, wd, res[0], gt_f, g_post_ffn[layer], n, tf)

        if not last:
            four_c = _fourier_context(a_c, w_f)
            win_c = _window_context(qw_c, kw_c, vw_c, sink)
            mla_c = _mla_attention_small(qm_c, km_c, vm_c)
            sgu_c = _sgu(u_c, vn_c, w_s, bias_full)
            parts_c = [t.reshape(tcx, GROUP_W) for t in (four_c, win_c, mla_c, sgu_c)]
            res_c = _outproj(parts_c, w_o, hc, cgt_a, g_post_mix[layer], g_pre_ffn[layer], csh_f, csc_f, lc, tm_c, w_router)
            if dense:
                hc = _ffn_dense(res_c[1], wg, wu, wd, res_c[0], cgt_f, g_post_ffn[layer], lc, tm_c, tf)
            else:
                hc = _ffn_moe(res_c[1], res_c[2], wg, wu, wd, res_c[0], cgt_f, g_post_ffn[layer], lc, tf)
    return hx.reshape(b, n, d)
```
